```python
import jax, jax.numpy as jnp
from jax import lax
import numpy as np

D_MODEL = 2048
BATCH = 4
SEQ = 2048
DEPTH = 1
DEC_BATCH = 128
DEC_SEQ = 1
PAST_LEN = 16384
PAGE_SIZE = 128

HEAD_DIM = 128
A_Q_HEADS = 8
A_KV_HEADS = 2
A_GQA = A_Q_HEADS // A_KV_HEADS
A_WINDOW = 128
B_GROUPS = ((128, 1), (512, 4), (2048, 16))
B_HEADS_PER_GROUP = 4
N_B_GROUPS = len(B_GROUPS)
B_HEADS = N_B_GROUPS * B_HEADS_PER_GROUP
BAND_BLOCK = 128
A_Q_W = A_Q_HEADS * HEAD_DIM
A_KV_W = A_KV_HEADS * HEAD_DIM
B_W = B_HEADS * HEAD_DIM
B_OUT_W = B_HEADS_PER_GROUP * HEAD_DIM
IN_SPLITS = (A_Q_W, A_KV_W, A_KV_W, B_W, B_W, B_W, D_MODEL, D_MODEL)
N_IN = sum(IN_SPLITS)
N_ALIBI_HEADS = A_Q_HEADS + B_HEADS
ATTN_SCALE = HEAD_DIM ** -0.5
PEER_HEADS = 8
PEER_NKEYS = 128
PEER_EXPERTS = PEER_NKEYS * PEER_NKEYS
PEER_DKEY = 128
PEER_TOPK = 16
PEER_CHUNK = 128
NORM_EPS = 1e-6
NEG_INF = -1e30

kernel_name = 'hybrid_swa_sink_dilated_peer_step'


def rmsnorm(x, w):
    xf = x.astype(jnp.float32)
    y = xf * lax.rsqrt(jnp.mean(xf * xf, axis=-1, keepdims=True) + NORM_EPS)
    return (y * w.astype(jnp.float32)).astype(x.dtype)


def alibi_slopes():
    return 2.0 ** (-8.0 * jnp.arange(1, N_ALIBI_HEADS + 1, dtype=jnp.float32) / N_ALIBI_HEADS)


def project(xn, w_in, q_norm_a, k_norm_a, q_norm_b, k_norm_b):
    lead = xn.shape[:-1]
    z = xn @ w_in
    cuts = [int(c) for c in np.cumsum(IN_SPLITS)[:-1]]
    qa, ka, va, qb, kb, vb, ga, gb = jnp.split(z, cuts, axis=-1)
    qa = rmsnorm(qa.reshape(*lead, A_KV_HEADS, A_GQA, HEAD_DIM), q_norm_a)
    ka = rmsnorm(ka.reshape(*lead, A_KV_HEADS, HEAD_DIM), k_norm_a)
    va = va.reshape(*lead, A_KV_HEADS, HEAD_DIM)
    qb = rmsnorm(qb.reshape(*lead, N_B_GROUPS, B_HEADS_PER_GROUP, HEAD_DIM), q_norm_b)
    kb = rmsnorm(kb.reshape(*lead, N_B_GROUPS, B_HEADS_PER_GROUP, HEAD_DIM), k_norm_b)
    vb = vb.reshape(*lead, N_B_GROUPS, B_HEADS_PER_GROUP, HEAD_DIM)
    return qa, ka, va, qb, kb, vb, ga, gb


def masked_softmax(logits, mask, sink):
    logits = jnp.where(mask, logits, NEG_INF)
    m = jnp.max(logits, axis=-1)
    if sink is not None:
        m = jnp.maximum(m, sink)
    p = jnp.exp(logits - m[..., None])
    denom = jnp.sum(p, axis=-1)
    if sink is not None:
        denom = denom + jnp.exp(sink - m)
    return p / denom[..., None], m + jnp.log(denom)


def banded_window_attention(q, k, v, slopes, window_idx, dil, sink):
    N, n, Hk, G, hd = q.shape
    blk = BAND_BLOCK
    nb = -(-n // blk)
    n_pad = nb * blk
    pad = n_pad - n
    qb = jnp.pad(q, ((0, 0), (0, pad), (0, 0), (0, 0), (0, 0))).reshape(N, nb, blk, Hk, G, hd)

    def band(t):
        tp = jnp.pad(t, ((0, 0), (blk, pad), (0, 0), (0, 0)))
        prev = tp[:, :n_pad].reshape(N, nb, blk, Hk, hd)
        cur = tp[:, blk:].reshape(N, nb, blk, Hk, hd)
        return jnp.concatenate([prev, cur], axis=2)

    kw, vw = band(k), band(v)
    s = jnp.einsum('nbqhgd,nbshd->nbhgqs', qb, kw, preferred_element_type=jnp.float32) * ATTN_SCALE
    qi = jnp.arange(blk)[:, None]
    sj = jnp.arange(2 * blk)[None, :]
    dist = qi - sj + blk
    key_pos = (jnp.arange(nb) * blk - blk)[:, None, None] + sj[None]
    mask = (dist >= 0) & (dist <= window_idx) & (key_pos >= 0)
    logits = s - slopes[:, :, None, None] * (dil * dist).astype(jnp.float32)
    p, lse = masked_softmax(logits, mask[:, None, None], None if sink is None else sink[:, :, None])
    o = jnp.einsum('nbhgqs,nbshd->nbqhgd', p, vw.astype(jnp.float32)).astype(q.dtype)
    o = o.reshape(N, n_pad, Hk, G, hd)[:, :n]
    lse = lse.transpose(0, 1, 4, 2, 3).reshape(N, n_pad, Hk, G)[:, :n]
    return o, lse


def gathered_window_attention(q, k_new, v_new, cache_kv, slopes, window, dil, sink):
    L = cache_kv.shape[1]
    ds = q.shape[1]
    kcat = jnp.concatenate([cache_kv[:, :, 0], k_new], axis=1)
    vcat = jnp.concatenate([cache_kv[:, :, 1], v_new], axis=1)
    steps = jnp.arange(window // dil + 1)
    idx = L + jnp.arange(ds)[:, None] - steps[None, :] * dil
    valid = idx >= 0
    idx = jnp.maximum(idx, 0)
    kg, vg = kcat[:, idx], vcat[:, idx]
    s = jnp.einsum('bjhgd,bjshd->bjhgs', q, kg, preferred_element_type=jnp.float32) * ATTN_SCALE
    logits = s - slopes[:, :, None] * (steps * dil).astype(jnp.float32)
    p, lse = masked_softmax(logits, valid[None, :, None, None, :], sink)
    o = jnp.einsum('bjhgs,bjshd->bjhgd', p, vg.astype(jnp.float32)).astype(q.dtype)
    new_buf = jnp.stack([kcat, vcat], axis=2)[:, -L:]
    return o, lse, new_buf


def to_residue(t, d):
    b, s = t.shape[:2]
    rest = t.shape[2:]
    return t.reshape(b, s // d, d, *rest).swapaxes(1, 2).reshape(b * d, s // d, *rest)


def from_residue(t, d, b):
    n = t.shape[1]
    rest = t.shape[2:]
    return t.reshape(b, d, n, *rest).swapaxes(1, 2).reshape(b, n * d, *rest)


def window_rows(k, v, window):
    L = min(window, k.shape[1])
    return jnp.stack([k[:, -L:], v[:, -L:]], axis=2)


def combine_dilated(outs, lses):
    o = jnp.stack(outs, axis=-3)
    w = jax.nn.softmax(jnp.stack(lses, axis=-2), axis=-2)
    return jnp.sum(w[..., None] * o.astype(jnp.float32), axis=-3).astype(outs[0].dtype)


def dilated_prompt(qb, kb, vb, slopes_b):
    b = qb.shape[0]
    outs, lses = [], []
    for g, (win, dil) in enumerate(B_GROUPS):
        o, lse = banded_window_attention(
            to_residue(qb[:, :, g, :, None, :], dil), to_residue(kb[:, :, g], dil),
            to_residue(vb[:, :, g], dil), slopes_b[g], win // dil, dil, None)
        outs.append(from_residue(o[:, :, :, 0], dil, b))
        lses.append(from_residue(lse[:, :, :, 0], dil, b))
    return combine_dilated(outs, lses)


def dilated_sample(qb, kb, vb, caches, slopes_b):
    outs, lses, bufs = [], [], []
    for g, (win, dil) in enumerate(B_GROUPS):
        o, lse, buf = gathered_window_attention(
            qb[:, :, g, :, None, :], kb[:, :, g], vb[:, :, g], caches[g], slopes_b[g], win, dil, None)
        outs.append(o[:, :, :, 0])
        lses.append(lse[:, :, :, 0])
        bufs.append(buf)
    return combine_dilated(outs, lses), bufs


def peer_route(hn, wq, subkeys):
    t = hn.shape[0]
    q = (hn @ wq).reshape(t, PEER_HEADS, 2, PEER_DKEY // 2)
    s = jnp.einsum('thcd,cnd->thcn', q, subkeys, preferred_element_type=jnp.float32)
    half_s, half_i = lax.top_k(s, PEER_TOPK)
    cand = (half_s[:, :, 0, :, None] + half_s[:, :, 1, None, :]).reshape(t, PEER_HEADS, PEER_TOPK * PEER_TOPK)
    best_s, best_c = lax.top_k(cand, PEER_TOPK)
    i1 = jnp.take_along_axis(half_i[:, :, 0], best_c // PEER_TOPK, axis=-1)
    i2 = jnp.take_along_axis(half_i[:, :, 1], best_c % PEER_TOPK, axis=-1)
    experts = (i1 * PEER_NKEYS + i2).reshape(t, -1)
    return experts, jax.nn.softmax(best_s, axis=-1).reshape(t, -1)


def peer_experts(hn, experts, gates, u, v):
    t, d = hn.shape
    n_chunks = -(-t // PEER_CHUNK)
    pad = n_chunks * PEER_CHUNK - t

    def chunked(a):
        return jnp.pad(a, ((0, pad), (0, 0))).reshape(n_chunks, PEER_CHUNK, a.shape[-1])

    def one_chunk(args):
        xc, ec, gc = args
        act = jax.nn.gelu(jnp.einsum('cd,ced->ce', xc, u[ec], preferred_element_type=jnp.float32),
                          approximate=False)
        return jnp.einsum('ce,ced->cd', (gc * act).astype(xc.dtype), v[ec])

    out = lax.map(one_chunk, (chunked(hn), chunked(experts), chunked(gates)))
    return out.reshape(n_chunks * PEER_CHUNK, d)[:t]


def block_output(x, oa, ob, ga, gb, w_branch_a, w_branch_b, w_out,
                 norm2_w, peer_wq, peer_subkeys, peer_u, peer_v):
    lead = x.shape[:-1]
    ya = oa.reshape(*lead, A_Q_W) @ w_branch_a
    yb = ob.reshape(*lead, B_OUT_W) @ w_branch_b
    h = x + (jax.nn.sigmoid(ga) * ya + jax.nn.sigmoid(gb) * yb) @ w_out
    hn = rmsnorm(h, norm2_w).reshape(-1, D_MODEL)
    experts, gates = peer_route(hn, peer_wq, peer_subkeys)
    return h + peer_experts(hn, experts, gates, peer_u, peer_v).reshape(h.shape)


def setup_inputs(seed: int = 0) -> dict:
    key = jax.random.key(seed)
    ks = jax.random.split(key, 24)

    def nrm(k, shape, scale):
        return jax.random.normal(k, shape, jnp.float32) * scale

    def gain(k, shape):
        return 1.0 + 0.02 * jax.random.normal(k, shape, jnp.float32)

    la = min(A_WINDOW, PAST_LEN)
    l1, l2, l3 = (min(w, PAST_LEN) for w, _ in B_GROUPS)
    return {
        'x_prompt': nrm(ks[0], (BATCH, SEQ, D_MODEL), 1.0),
        'x_sample': nrm(ks[1], (DEC_BATCH, DEC_SEQ, D_MODEL), 1.0),
        'cache_a_kv': nrm(ks[2], (DEPTH, DEC_BATCH, la, 2, A_KV_HEADS, HEAD_DIM), 1.0),
        'cache_b1_kv': nrm(ks[3], (DEPTH, DEC_BATCH, l1, 2, B_HEADS_PER_GROUP, HEAD_DIM), 1.0),
        'cache_b2_kv': nrm(ks[4], (DEPTH, DEC_BATCH, l2, 2, B_HEADS_PER_GROUP, HEAD_DIM), 1.0),
        'cache_b3_kv': nrm(ks[5], (DEPTH, DEC_BATCH, l3, 2, B_HEADS_PER_GROUP, HEAD_DIM), 1.0),
        'norm1_w': gain(ks[6], (DEPTH, D_MODEL)),
        'w_in': nrm(ks[7], (DEPTH, D_MODEL, N_IN), D_MODEL ** -0.5),
        'q_norm_a': gain(ks[8], (DEPTH, HEAD_DIM)),
        'k_norm_a': gain(ks[9], (DEPTH, HEAD_DIM)),
        'sink_a': nrm(ks[10], (DEPTH, A_Q_HEADS), 0.5),
        'q_norm_b': gain(ks[11], (DEPTH, HEAD_DIM)),
        'k_norm_b': gain(ks[12], (DEPTH, HEAD_DIM)),
        'w_branch_a': nrm(ks[13], (DEPTH, A_Q_W, D_MODEL), A_Q_W ** -0.5),
        'w_branch_b': nrm(ks[14], (DEPTH, B_OUT_W, D_MODEL), B_OUT_W ** -0.5),
        'w_out': nrm(ks[15], (DEPTH, D_MODEL, D_MODEL), D_MODEL ** -0.5),
        'norm2_w': gain(ks[16], (DEPTH, D_MODEL)),
        'peer_wq': nrm(ks[17], (DEPTH, D_MODEL, PEER_HEADS * PEER_DKEY), D_MODEL ** -0.5),
        'peer_subkeys': nrm(ks[18], (DEPTH, 2, PEER_NKEYS, PEER_DKEY // 2), (PEER_DKEY // 2) ** -0.5),
        'peer_u': nrm(ks[19], (DEPTH, PEER_EXPERTS, D_MODEL), D_MODEL ** -0.5),
        'peer_v': nrm(ks[20], (DEPTH, PEER_EXPERTS, D_MODEL), PEER_HEADS ** -0.5),
    }


def reference(x_prompt, x_sample, cache_a_kv, cache_b1_kv, cache_b2_kv, cache_b3_kv,
              norm1_w, w_in, q_norm_a, k_norm_a, sink_a, q_norm_b, k_norm_b,
              w_branch_a, w_branch_b, w_out, norm2_w, peer_wq, peer_subkeys, peer_u, peer_v):
    slopes = alibi_slopes()
    slopes_a = slopes[:A_Q_HEADS].reshape(A_KV_HEADS, A_GQA)
    slopes_b = slopes[A_Q_HEADS:].reshape(N_B_GROUPS, B_HEADS_PER_GROUP, 1)
    hp, hs = x_prompt, x_sample
    sa_p, sa_s = [], []
    sb_p = [[] for _ in B_GROUPS]
    sb_s = [[] for _ in B_GROUPS]
    for l in range(DEPTH):
        sink = sink_a[l].astype(jnp.float32).reshape(A_KV_HEADS, A_GQA)
        ffn_w = (norm2_w[l], peer_wq[l], peer_subkeys[l], peer_u[l], peer_v[l])
        qa, ka, va, qb, kb, vb, ga, gb = project(rmsnorm(hp, norm1_w[l]), w_in[l],
                                                 q_norm_a[l], k_norm_a[l], q_norm_b[l], k_norm_b[l])
        oa, _ = banded_window_attention(qa, ka, va, slopes_a, A_WINDOW, 1, sink)
        ob = dilated_prompt(qb, kb, vb, slopes_b)
        sa_p.append(window_rows(ka, va, A_WINDOW))
        for g, (win, _) in enumerate(B_GROUPS):
            sb_p[g].append(window_rows(kb[:, :, g], vb[:, :, g], win))
        hp = block_output(hp, oa, ob, ga, gb, w_branch_a[l], w_branch_b[l], w_out[l], *ffn_w)
        qa, ka, va, qb, kb, vb, ga, gb = project(rmsnorm(hs, norm1_w[l]), w_in[l],
                                                 q_norm_a[l], k_norm_a[l], q_norm_b[l], k_norm_b[l])
        oa, _, buf_a = gathered_window_attention(qa, ka, va, cache_a_kv[l], slopes_a, A_WINDOW, 1, sink)
        ob, bufs_b = dilated_sample(qb, kb, vb, (cache_b1_kv[l], cache_b2_kv[l], cache_b3_kv[l]), slopes_b)
        sa_s.append(buf_a)
        for g in range(N_B_GROUPS):
            sb_s[g].append(bufs_b[g])
        hs = block_output(hs, oa, ob, ga, gb, w_branch_a[l], w_branch_b[l], w_out[l], *ffn_w)
    return (hp, hs,
            jnp.stack(sa_p), jnp.stack(sb_p[0]), jnp.stack(sb_p[1]), jnp.stack(sb_p[2]),
            jnp.stack(sa_s), jnp.stack(sb_s[0]), jnp.stack(sb_s[1]), jnp.stack(sb_s[2]))
```

```python
import functools
import math

import jax
import jax.numpy as jnp
from jax import lax
from jax.experimental import pallas as pl
from jax.experimental.pallas import tpu as pltpu

F32 = jnp.float32
BF16 = jnp.bfloat16

D_MODEL = 2048
HEAD_DIM = 128
A_Q_HEADS = 8
A_KV_HEADS = 2
A_GQA = A_Q_HEADS // A_KV_HEADS
A_WINDOW = 128
B_GROUPS = ((128, 1), (512, 4), (2048, 16))
B_HEADS_PER_GROUP = 4
N_B_GROUPS = len(B_GROUPS)
B_HEADS = N_B_GROUPS * B_HEADS_PER_GROUP
BAND = 128
N_ALIBI_HEADS = A_Q_HEADS + B_HEADS
ATTN_SCALE = HEAD_DIM ** -0.5
PEER_HEADS = 8
PEER_NKEYS = 128
PEER_EXPERTS = PEER_NKEYS * PEER_NKEYS
PEER_DKEY = 128
PEER_TOPK = 16
NORM_EPS = 1e-6
NEG_INF = -1e30

QKV_W = (A_Q_HEADS + 2 * A_KV_HEADS + 3 * B_HEADS) * HEAD_DIM
GATE_W = 2 * D_MODEL
HEAD_QA, HEAD_KA, HEAD_VA = 0, A_Q_HEADS, A_Q_HEADS + A_KV_HEADS
HEAD_QB = A_Q_HEADS + 2 * A_KV_HEADS
HEAD_KB = HEAD_QB + B_HEADS
HEAD_VB = HEAD_KB + B_HEADS
N_QKV_HEADS = QKV_W // HEAD_DIM

VMEM_LIMIT_BYTES = 56 * 1024 * 1024
LANES = 128

NT_DIMS = (((1,), (1,)), ((), ()))
TN_DIMS = (((0,), (0,)), ((), ()))


def _params(n_grid_axes):
    return pltpu.CompilerParams(
        dimension_semantics=("arbitrary",) * n_grid_axes,
        vmem_limit_bytes=VMEM_LIMIT_BYTES)


def _rms(x):
    return x * lax.rsqrt(jnp.mean(x * x, axis=-1, keepdims=True) + NORM_EPS)


PROJ_TN = 512
PROJ_HEADS_PER_TILE = PROJ_TN // HEAD_DIM


def _proj_kernel(x_ref, n1_ref, w_ref, cw_ref, o_ref, xn_ref, *, kinds):
    j = pl.program_id(1)

    @pl.when(j == 0)
    def _():
        xn_ref[...] = (_rms(x_ref[...]) * n1_ref[...]).astype(BF16)

    z = jnp.dot(xn_ref[...], w_ref[...], preferred_element_type=F32)

    def epilogue(head_is_normed):
        for hh, normed in enumerate(head_is_normed):
            cs = slice(hh * HEAD_DIM, (hh + 1) * HEAD_DIM)
            zh = z[:, cs]
            if normed:
                zh = _rms(zh) * cw_ref[:, cs]
            o_ref[:, cs] = zh.astype(o_ref.dtype)

    for kind in sorted(set(kinds)):
        cond = functools.reduce(jnp.logical_or, [j == jj for jj, k in enumerate(kinds) if k == kind])
        if kind == "gate":
            @pl.when(cond)
            def _():
                o_ref[...] = jax.nn.sigmoid(z).astype(o_ref.dtype)
        else:
            @pl.when(cond)
            def _(kind=kind):
                epilogue(kind)


def _proj(x2d, n1, w, cw, kinds, out_dtype, tm, name):
    t, d = x2d.shape
    n = w.shape[1]
    assert t % tm == 0 and n == PROJ_TN * len(kinds)
    return pl.pallas_call(
        functools.partial(_proj_kernel, kinds=kinds),
        grid=(t // tm, len(kinds)),
        in_specs=[
            pl.BlockSpec((tm, d), lambda i, j: (i, 0)),
            pl.BlockSpec((1, d), lambda i, j: (0, 0)),
            pl.BlockSpec((d, PROJ_TN), lambda i, j: (0, j)),
            pl.BlockSpec((1, PROJ_TN), lambda i, j: (0, j)),
        ],
        out_specs=pl.BlockSpec((tm, PROJ_TN), lambda i, j: (i, j)),
        out_shape=jax.ShapeDtypeStruct((t, n), out_dtype),
        scratch_shapes=[pltpu.VMEM((tm, d), BF16)],
        compiler_params=_params(2),
        name=name,
    )(x2d, n1, w, cw)


_N4, _I4 = (True,) * 4, (False,) * 4
QKV_KINDS = (_N4, _N4, (True, True, False, False)) + (_N4,) * 6 + (_I4,) * 3
GATE_KINDS = ("gate",) * (GATE_W // PROJ_TN)


def _band_softmax(q, k, v, slope_dist, mask, sink):
    s = lax.dot_general(q, k, NT_DIMS, preferred_element_type=F32) * ATTN_SCALE
    logits = jnp.where(mask, s - slope_dist, NEG_INF)
    m = jnp.max(logits, axis=-1, keepdims=True)
    if sink is not None:
        m = jnp.maximum(m, sink)
    p = jnp.exp(logits - m)
    denom = jnp.sum(p, axis=-1, keepdims=True)
    if sink is not None:
        denom = denom + jnp.exp(sink - m)
    o = jnp.dot(p.astype(BF16), v, preferred_element_type=F32) / denom
    return o, m + jnp.log(denom)


def _band_geometry(has_prev):
    nk = 2 * BAND if has_prev else BAND
    qi = lax.broadcasted_iota(jnp.int32, (BAND, nk), 0)
    sj = lax.broadcasted_iota(jnp.int32, (BAND, nk), 1)
    dist = qi - sj + (BAND if has_prev else 0)
    return dist, sj


def _attn_a_kernel(slopes_ref, sink_ref, q_ref, k_ref, v_ref, o_ref):
    kvh = pl.program_id(1)
    n_blocks = q_ref.shape[1] // BAND
    dist, sj = _band_geometry(True)
    in_window = (dist >= 0) & (dist <= A_WINDOW)
    distf = dist.astype(F32)

    def block(blk, carry):
        cur = pl.multiple_of(blk * BAND, BAND)
        prev = pl.multiple_of(jnp.maximum(blk - 1, 0) * BAND, BAND)
        k = jnp.concatenate([k_ref[0, pl.ds(prev, BAND), :], k_ref[0, pl.ds(cur, BAND), :]], axis=0).astype(BF16)
        v = jnp.concatenate([v_ref[0, pl.ds(prev, BAND), :], v_ref[0, pl.ds(cur, BAND), :]], axis=0).astype(BF16)
        mask = in_window & (sj >= jnp.where(blk > 0, 0, BAND))
        for g in range(A_GQA):
            cs = slice(g * HEAD_DIM, (g + 1) * HEAD_DIM)
            head = kvh * A_GQA + g
            q = q_ref[0, pl.ds(cur, BAND), cs].astype(BF16)
            o, _ = _band_softmax(q, k, v, slopes_ref[head] * distf, mask, sink_ref[head])
            o_ref[0, pl.ds(cur, BAND), cs] = o.astype(o_ref.dtype)
        return carry

    lax.fori_loop(0, n_blocks, block, 0)


def _attn_a(zq3, slopes_a, sink):
    b, s, _ = zq3.shape
    gw = A_GQA * HEAD_DIM
    smem = pl.BlockSpec(memory_space=pltpu.SMEM)
    return pl.pallas_call(
        _attn_a_kernel,
        grid=(b, A_KV_HEADS),
        in_specs=[
            smem, smem,
            pl.BlockSpec((1, s, gw), lambda bi, h: (bi, 0, h)),
            pl.BlockSpec((1, s, HEAD_DIM), lambda bi, h: (bi, 0, HEAD_KA + h)),
            pl.BlockSpec((1, s, HEAD_DIM), lambda bi, h: (bi, 0, HEAD_VA + h)),
        ],
        out_specs=pl.BlockSpec((1, s, gw), lambda bi, h: (bi, 0, h)),
        out_shape=jax.ShapeDtypeStruct((b, s, A_Q_HEADS * HEAD_DIM), BF16),
        compiler_params=_params(2),
        name="attn_a",
    )(slopes_a, sink, zq3, zq3, zq3)


def _attn_b_kernel(slopes_ref, *refs):
    q_refs, k_refs, v_refs = refs[0:3], refs[3:6], refs[6:9]
    o_ref, og_ref, lse_ref = refs[9:12]
    h = pl.program_id(1)
    s_len = o_ref.shape[1]

    for g, (win, dil) in enumerate(B_GROUPS):
        n_blocks = s_len // (BAND * dil)
        slope = slopes_ref[g * B_HEADS_PER_GROUP + h]
        for has_prev in (False, True):
            dist, _ = _band_geometry(has_prev)
            mask = (dist >= 0) & (dist <= win // dil)
            slope_dist = slope * (dil * dist).astype(F32)
            for blk in range(n_blocks):
                if (blk > 0) != has_prev:
                    continue
                for r in range(dil):
                    def rows(ref, b0):
                        start = BAND * dil * b0 + r
                        if dil == 1:
                            return ref[0, pl.ds(start, BAND), :]
                        return ref[0, pl.ds(start, BAND, stride=dil), :]

                    q = rows(q_refs[g], blk).astype(BF16)
                    k = rows(k_refs[g], blk)
                    v = rows(v_refs[g], blk)
                    if has_prev:
                        k = jnp.concatenate([rows(k_refs[g], blk - 1), k], axis=0)
                        v = jnp.concatenate([rows(v_refs[g], blk - 1), v], axis=0)
                    o, lse = _band_softmax(q, k.astype(BF16), v.astype(BF16), slope_dist, mask, None)
                    start = BAND * dil * blk + r
                    idx = pl.ds(start, BAND) if dil == 1 else pl.ds(start, BAND, stride=dil)
                    og_ref[g, idx, :] = o
                    lse_ref[g, idx, :] = jnp.broadcast_to(lse, (BAND, HEAD_DIM))

    def combine(c, carry):
        rs = pl.ds(pl.multiple_of(c * BAND, BAND), BAND)
        lses = [lse_ref[g, rs, :] for g in range(N_B_GROUPS)]
        mx = functools.reduce(jnp.maximum, lses)
        ws = [jnp.exp(l - mx) for l in lses]
        acc = functools.reduce(lambda a, b_: a + b_, [w * og_ref[g, rs, :] for g, w in enumerate(ws)])
        o_ref[0, rs, :] = (acc / functools.reduce(lambda a, b_: a + b_, ws)).astype(o_ref.dtype)
        return carry

    lax.fori_loop(0, s_len // BAND, combine, 0)


def _attn_b(zq3, slopes_b):
    b, s, _ = zq3.shape
    smem = pl.BlockSpec(memory_space=pltpu.SMEM)

    def head_spec(base):
        return [pl.BlockSpec((1, s, HEAD_DIM), lambda bi, h, c=base + g * B_HEADS_PER_GROUP: (bi, 0, c + h))
                for g in range(N_B_GROUPS)]

    return pl.pallas_call(
        _attn_b_kernel,
        grid=(b, B_HEADS_PER_GROUP),
        in_specs=[smem] + head_spec(HEAD_QB) + head_spec(HEAD_KB) + head_spec(HEAD_VB),
        out_specs=pl.BlockSpec((1, s, HEAD_DIM), lambda bi, h: (bi, 0, h)),
        out_shape=jax.ShapeDtypeStruct((b, s, B_HEADS_PER_GROUP * HEAD_DIM), BF16),
        scratch_shapes=[pltpu.VMEM((N_B_GROUPS, s, HEAD_DIM), F32),
                        pltpu.VMEM((N_B_GROUPS, s, HEAD_DIM), F32)],
        compiler_params=_params(2),
        name="attn_b",
    )(slopes_b, *([zq3] * 9))


SAMPLE_BS = 8


def _window_read(q, k_new, v_new, c_ref, slope, dil, sink):
    n = c_ref.shape[1]
    k = c_ref[:, :, 0]
    v = c_ref[:, :, 1]
    s = jnp.sum(k * q[:, None], axis=-1, keepdims=True) * ATTN_SCALE
    steps = (n - lax.broadcasted_iota(jnp.int32, s.shape, 1)).astype(F32)
    logits = s - slope[:, None] * (steps * float(dil))
    s_new = jnp.sum(k_new * q, axis=-1, keepdims=True) * ATTN_SCALE
    m = jnp.maximum(jnp.max(logits, axis=1), s_new)
    if sink is not None:
        m = jnp.maximum(m, sink)
    p = jnp.exp(logits - m[:, None])
    p_new = jnp.exp(s_new - m)
    denom = jnp.sum(p, axis=1) + p_new
    if sink is not None:
        denom = denom + jnp.exp(sink - m)
    o = (jnp.sum(p * v, axis=1) + p_new * v_new) / denom
    return o, m + jnp.log(denom)


def _sample_attn_kernel(z_ref, ca_ref, cb1_ref, cb2_ref, cb3_ref, slope_a_ref, sink_ref, slope_b_ref,
                        oa_ref, ob_ref):
    z = z_ref[...]
    k_new = z[:, HEAD_KA:HEAD_KA + A_KV_HEADS]
    v_new = z[:, HEAD_VA:HEAD_VA + A_KV_HEADS]
    for g in range(A_GQA):
        q = jnp.concatenate([z[:, kv * A_GQA + g:kv * A_GQA + g + 1] for kv in range(A_KV_HEADS)], axis=1)
        o, _ = _window_read(q, k_new, v_new, ca_ref, slope_a_ref[g], 1, sink_ref[g])
        oa_ref[:, g] = o
    outs, lses = [], []
    for g, (c_ref, (_, dil)) in enumerate(zip((cb1_ref, cb2_ref, cb3_ref), B_GROUPS)):
        hs = lambda base: slice(base + g * B_HEADS_PER_GROUP, base + (g + 1) * B_HEADS_PER_GROUP)
        o, lse = _window_read(z[:, hs(HEAD_QB)], z[:, hs(HEAD_KB)], z[:, hs(HEAD_VB)], c_ref,
                              slope_b_ref[g], dil, None)
        outs.append(o)
        lses.append(lse)
    mx = functools.reduce(jnp.maximum, lses)
    ws = [jnp.exp(l - mx) for l in lses]
    acc = functools.reduce(lambda a, b_: a + b_, [w * o for w, o in zip(ws, outs)])
    ob_ref[...] = acc / functools.reduce(lambda a, b_: a + b_, ws)


def _sample_attn(zs3, cache_a, cache_b1, cache_b2, cache_b3, slope_a, sink, slope_b):
    n = zs3.shape[0]
    bs = SAMPLE_BS

    def strided_rows(c, dil):
        nb, l, two, hh, hd = c.shape
        assert l == A_WINDOW * dil
        c6 = c.reshape(nb, l // dil, dil, two, hh, hd)
        spec = pl.BlockSpec((bs, l // dil, None, two, hh, hd), lambda i: (i, 0, 0, 0, 0, 0))
        return c6, spec

    ca, ca_spec = strided_rows(cache_a, 1)
    cb = [strided_rows(c, dil) for c, (_, dil) in zip((cache_b1, cache_b2, cache_b3), B_GROUPS)]
    full = lambda a: pl.BlockSpec(a.shape, lambda i: (0,) * a.ndim)
    return pl.pallas_call(
        _sample_attn_kernel,
        grid=(n // bs,),
        in_specs=[pl.BlockSpec((bs, N_QKV_HEADS, HEAD_DIM), lambda i: (i, 0, 0)),
                  ca_spec, cb[0][1], cb[1][1], cb[2][1],
                  full(slope_a), full(sink), full(slope_b)],
        out_specs=[pl.BlockSpec((bs, A_GQA, A_KV_HEADS, HEAD_DIM), lambda i: (i, 0, 0, 0)),
                   pl.BlockSpec((bs, B_HEADS_PER_GROUP, HEAD_DIM), lambda i: (i, 0, 0))],
        out_shape=[jax.ShapeDtypeStruct((n, A_GQA, A_KV_HEADS, HEAD_DIM), F32),
                   jax.ShapeDtypeStruct((n, B_HEADS_PER_GROUP, HEAD_DIM), F32)],
        compiler_params=_params(1),
        name="sample_attn",
    )(zs3, ca, cb[0][0], cb[1][0], cb[2][0], slope_a, sink, slope_b)


CACHE_COPY_CHUNKS = 8


def _cache_copies(caches, news, outs, sems):
    copies = []
    for ci, (c, nw, o) in enumerate(zip(caches, news, outs)):
        n, l = c.shape[0], c.shape[1]
        step = n // CACHE_COPY_CHUNKS
        for j in range(CACHE_COPY_CHUNKS):
            seqs = pl.ds(j * step, step)
            copies.append(pltpu.make_async_copy(
                c.at[seqs, pl.ds(1, l - 1)], o.at[seqs, pl.ds(0, l - 1)], sems.at[ci, j]))
        copies.append(pltpu.make_async_copy(nw, o.at[:, l - 1], sems.at[ci, CACHE_COPY_CHUNKS]))
    return copies


def _cache_kernel(*refs):
    n = (len(refs) - 1) // 3
    caches, news, outs, sems = refs[:n], refs[n:2 * n], refs[2 * n:3 * n], refs[3 * n]
    copies = _cache_copies(caches, news, outs, sems)
    for cp in copies:
        cp.start()
    for cp in copies:
        cp.wait()


def _cache_update(caches, news):
    n = len(caches)
    any_spec = pl.BlockSpec(memory_space=pl.ANY)
    return pl.pallas_call(
        _cache_kernel,
        in_specs=[any_spec] * (2 * n),
        out_specs=[any_spec] * n,
        out_shape=[jax.ShapeDtypeStruct(c.shape, c.dtype) for c in caches],
        scratch_shapes=[pltpu.SemaphoreType.DMA((n, CACHE_COPY_CHUNKS + 1))],
        name="cache_update",
    )(*caches, *news)


def _block_kernel(x_ref, oa_ref, ob_ref, ga_ref, gb_ref, wa_ref, wb_ref, wo_ref, n2_ref, h_ref, hn_ref):
    ya = jnp.dot(oa_ref[...], wa_ref[...], preferred_element_type=F32)
    yb = jnp.dot(ob_ref[...], wb_ref[...], preferred_element_type=F32)
    mix = (ga_ref[...].astype(F32) * ya + gb_ref[...].astype(F32) * yb).astype(BF16)
    h = x_ref[...] + jnp.dot(mix, wo_ref[...], preferred_element_type=F32)
    h_ref[...] = h
    hn_ref[...] = (_rms(h) * n2_ref[...]).astype(BF16)


def _block(x2d, oa, ob, zg, wa, wb, wo, n2, tm):
    t, d = x2d.shape
    row = lambda w: pl.BlockSpec((tm, w), lambda i: (i, 0))
    full = lambda a: pl.BlockSpec(a.shape, lambda i: (0, 0))
    return pl.pallas_call(
        _block_kernel,
        grid=(t // tm,),
        in_specs=[row(d), row(oa.shape[1]), row(ob.shape[1]),
                  pl.BlockSpec((tm, d), lambda i: (i, 0)), pl.BlockSpec((tm, d), lambda i: (i, 1)),
                  full(wa), full(wb), full(wo), full(n2)],
        out_specs=[row(d), row(d)],
        out_shape=[jax.ShapeDtypeStruct((t, d), F32), jax.ShapeDtypeStruct((t, d), BF16)],
        compiler_params=_params(1),
        name="block",
    )(x2d, oa, ob, zg, zg, wa, wb, wo, n2)


RANK_NONE = float(PEER_NKEYS)
CAND_ROWS = tuple(PEER_TOPK // (a + 1) for a in range(PEER_TOPK))
CAND_PAD = tuple(-(-r // 8) * 8 for r in CAND_ROWS)


def _extract_top(work, n_take):
    rows = lax.broadcasted_iota(jnp.int32, work.shape, 0).astype(F32)
    rank = jnp.full(work.shape, RANK_NONE, F32)
    vals = []
    for kk in range(n_take):
        mx = jnp.max(work, axis=0, keepdims=True)
        first = jnp.min(jnp.where(work == mx, rows, float(work.shape[0])), axis=0, keepdims=True)
        sel = rows == first
        rank = jnp.where(sel, float(kk), rank)
        work = jnp.where(sel, -jnp.inf, work)
        vals.append(mx)
    return rank, vals


def _route_kernel(hn_ref, wq_ref, sk_ref, rank1_ref, p1_ref, m_ref, p0_ref, s_scr):
    q = jnp.dot(hn_ref[...], wq_ref[...], preferred_element_type=F32).astype(BF16)
    s_scr[...] = lax.dot_general(sk_ref[...], q, NT_DIMS, preferred_element_type=F32)

    def head(hh, carry):
        base = pl.multiple_of(hh * 2 * PEER_NKEYS, 2 * PEER_NKEYS)
        s0 = s_scr[pl.ds(base, PEER_NKEYS), :]
        s1 = s_scr[pl.ds(base + PEER_NKEYS, PEER_NKEYS), :]
        rank0, vals0 = _extract_top(s0, PEER_TOPK)
        rank1, vals1 = _extract_top(s1, PEER_TOPK)
        v1 = jnp.concatenate(vals1, axis=0)
        cand = []
        for a in range(PEER_TOPK):
            blk = vals0[a] + v1[:CAND_PAD[a]]
            rr = lax.broadcasted_iota(jnp.int32, blk.shape, 0)
            cand.append(jnp.where(rr < CAND_ROWS[a], blk, -jnp.inf))
        taken, best = _extract_top(jnp.concatenate(cand, axis=0), PEER_TOPK)
        zsum = functools.reduce(lambda x, y: x + y, [jnp.exp(b - best[0]) for b in best])
        m = jnp.zeros_like(s0)
        off = 0
        for a in range(PEER_TOPK):
            cnt = jnp.sum(jnp.where(taken[off:off + CAND_PAD[a]] < RANK_NONE, 1.0, 0.0), axis=0, keepdims=True)
            m = jnp.where(rank0 == float(a), cnt, m)
            off += CAND_PAD[a]
        rank1_ref[hh] = rank1
        p1_ref[hh] = jnp.exp(s1 - vals1[0])
        m_ref[hh] = m
        p0_ref[hh] = jnp.exp(s0 - vals0[0]) / zsum
        return carry

    lax.fori_loop(0, PEER_HEADS, head, 0)


def _route(hn, wq, sk, tr):
    t, d = hn.shape
    table = jax.ShapeDtypeStruct((PEER_HEADS, PEER_NKEYS, t), F32)
    tspec = pl.BlockSpec((PEER_HEADS, PEER_NKEYS, tr), lambda i: (0, 0, i))
    full = lambda a: pl.BlockSpec(a.shape, lambda i: (0, 0))
    return pl.pallas_call(
        _route_kernel,
        grid=(t // tr,),
        in_specs=[pl.BlockSpec((tr, d), lambda i: (i, 0)), full(wq), full(sk)],
        out_specs=[tspec] * 4,
        out_shape=[table] * 4,
        scratch_shapes=[pltpu.VMEM((PEER_HEADS * 2 * PEER_NKEYS, tr), F32)],
        compiler_params=_params(1),
        name="route",
    )(hn, wq, sk)


PEER_TE = 512
PEER_KEYS_PER_BLOCK = PEER_TE // PEER_NKEYS
PEER_TABLE_KEYS = 8
PEER_BLOCKS_PER_TABLE = PEER_TABLE_KEYS // PEER_KEYS_PER_BLOCK
assert PEER_BLOCKS_PER_TABLE == 2
SQRT_HALF = math.sqrt(0.5)


def _peer_kernel(h_ref, hn_ref, u_ref, v_ref, rank1_ref, p1_ref, m_ref, p0_ref, o_ref, act_scr, w_scr):
    e = pl.program_id(1)
    tm = hn_ref.shape[0]
    act_scr[...] = lax.dot_general(u_ref[...], hn_ref[...], NT_DIMS, preferred_element_type=F32)
    upper = (e % PEER_BLOCKS_PER_TABLE) == 1

    def key_row(ref, hh, j, ls):
        return jnp.where(upper, ref[hh, PEER_KEYS_PER_BLOCK + j:PEER_KEYS_PER_BLOCK + j + 1, ls],
                         ref[hh, j:j + 1, ls])

    for j in range(PEER_KEYS_PER_BLOCK):
        rs = slice(j * PEER_NKEYS, (j + 1) * PEER_NKEYS)
        for c in range(tm // LANES):
            ls = slice(c * LANES, (c + 1) * LANES)
            gate = jnp.zeros((PEER_NKEYS, LANES), F32)
            for hh in range(PEER_HEADS):
                taken = rank1_ref[hh, :, ls] < key_row(m_ref, hh, j, ls)
                gate = gate + jnp.where(taken, p1_ref[hh, :, ls], 0.0) * key_row(p0_ref, hh, j, ls)
            a = act_scr[rs, ls]
            gelu = 0.5 * a * (1.0 + lax.erf(a * SQRT_HALF))
            w_scr[rs, ls] = (gate * gelu).astype(BF16)
    y = lax.dot_general(w_scr[...], v_ref[...], TN_DIMS, preferred_element_type=F32)

    @pl.when(e == 0)
    def _():
        o_ref[...] = h_ref[...] + y

    @pl.when(e > 0)
    def _():
        o_ref[...] += y


def _peer(h, hn, u, v, tables, tm):
    t, d = h.shape
    n_e = u.shape[0]
    tspec = pl.BlockSpec((PEER_HEADS, PEER_NKEYS, tm), lambda i, e: (0, 0, i))
    kspec = pl.BlockSpec((PEER_HEADS, PEER_TABLE_KEYS, tm), lambda i, e: (0, e // PEER_BLOCKS_PER_TABLE, i))
    return pl.pallas_call(
        _peer_kernel,
        grid=(t // tm, n_e // PEER_TE),
        in_specs=[pl.BlockSpec((tm, d), lambda i, e: (i, 0)),
                  pl.BlockSpec((tm, d), lambda i, e: (i, 0)),
                  pl.BlockSpec((PEER_TE, d), lambda i, e: (e, 0)),
                  pl.BlockSpec((PEER_TE, d), lambda i, e: (e, 0)), tspec, tspec, kspec, kspec],
        out_specs=pl.BlockSpec((tm, d), lambda i, e: (i, 0)),
        out_shape=jax.ShapeDtypeStruct((t, d), F32),
        scratch_shapes=[pltpu.VMEM((PEER_TE, tm), F32), pltpu.VMEM((PEER_TE, tm), BF16)],
        compiler_params=_params(2),
        name="peer",
    )(h, hn, u, v, *tables)


def _alibi_slopes():
    return 2.0 ** (-8.0 * jnp.arange(1, N_ALIBI_HEADS + 1, dtype=F32) / N_ALIBI_HEADS)


def _subkey_matrix(subkeys):
    two, nk, dh = subkeys.shape
    eye = jnp.eye(PEER_HEADS * two, dtype=subkeys.dtype).reshape(PEER_HEADS, two, PEER_HEADS, two)
    sk = jnp.einsum("hcgb,cnd->hcngbd", eye, subkeys)
    return sk.reshape(PEER_HEADS * two * nk, PEER_HEADS * two * dh)


def _tail(x2d, oa, ob, zg, w, tm_block, tr, tm_peer):
    h, hn = _block(x2d, oa, ob, zg, w["wa"], w["wb"], w["wo"], w["n2"], tm_block)
    tables = _route(hn, w["wq"], w["sk"], tr)
    return _peer(h, hn, w["u"], w["v"], tables, tm_peer)


def kernel(x_prompt, x_sample, cache_a_kv, cache_b1_kv, cache_b2_kv, cache_b3_kv, norm1_w, w_in, q_norm_a,
           k_norm_a, sink_a, q_norm_b, k_norm_b, w_branch_a, w_branch_b, w_out, norm2_w, peer_wq,
           peer_subkeys, peer_u, peer_v):
    assert norm1_w.shape[0] == 1, "single layer"
    b, s, d = x_prompt.shape
    n_dec = x_sample.shape[0]
    assert x_sample.shape[1] == 1

    slopes = _alibi_slopes()
    slopes_a, slopes_b = slopes[:A_Q_HEADS], slopes[A_Q_HEADS:]
    sink = sink_a[0].astype(F32)
    ones = jnp.ones((HEAD_DIM,), F32)
    col_w = jnp.concatenate(
        [jnp.tile(q_norm_a[0], A_Q_HEADS), jnp.tile(k_norm_a[0], A_KV_HEADS), jnp.tile(ones, A_KV_HEADS),
         jnp.tile(q_norm_b[0], B_HEADS), jnp.tile(k_norm_b[0], B_HEADS), jnp.tile(ones, B_HEADS)]
    ).astype(F32)[None]
    gate_w = jnp.ones((1, GATE_W), F32)
    n1 = norm1_w[0].astype(F32)[None]
    w_qkv = w_in[0, :, :QKV_W].astype(BF16)
    w_gate = w_in[0, :, QKV_W:].astype(BF16)
    w = dict(wa=w_branch_a[0].astype(BF16), wb=w_branch_b[0].astype(BF16), wo=w_out[0].astype(BF16),
             n2=norm2_w[0].astype(F32)[None], wq=peer_wq[0].astype(BF16),
             sk=_subkey_matrix(peer_subkeys[0]).astype(BF16),
             u=peer_u[0].astype(BF16), v=peer_v[0].astype(BF16))

    xp = x_prompt.reshape(b * s, d)
    zq = _proj(xp, n1, w_qkv, col_w, QKV_KINDS, F32, 1024, "proj_qkv")
    zg = _proj(xp, n1, w_gate, gate_w, GATE_KINDS, BF16, 1024, "proj_gate")
    zq3 = zq.reshape(b, s, QKV_W)
    oa = _attn_a(zq3, slopes_a, sink).reshape(b * s, A_Q_HEADS * HEAD_DIM)
    ob = _attn_b(zq3, slopes_b).reshape(b * s, B_HEADS_PER_GROUP * HEAD_DIM)
    y_prompt = _tail(xp, oa, ob, zg, w, 256, 256, 512).reshape(b, s, d)

    def window(k0, v0, nh, length):
        part = lambda h0: zq3[:, s - length:, h0 * HEAD_DIM:(h0 + nh) * HEAD_DIM].reshape(b, length, nh, HEAD_DIM)
        return jnp.stack([part(k0), part(v0)], axis=2)[None]

    kv_prompt = [window(HEAD_KA, HEAD_VA, A_KV_HEADS, min(A_WINDOW, s))]
    for g, (win, _) in enumerate(B_GROUPS):
        o4 = g * B_HEADS_PER_GROUP
        kv_prompt.append(window(HEAD_KB + o4, HEAD_VB + o4, B_HEADS_PER_GROUP, min(win, s)))

    xs = x_sample.reshape(n_dec, d)
    zs = _proj(xs, n1, w_qkv, col_w, QKV_KINDS, F32, n_dec, "proj_qkv_s")
    zgs = _proj(xs, n1, w_gate, gate_w, GATE_KINDS, BF16, n_dec, "proj_gate_s")
    zs3 = zs.reshape(n_dec, N_QKV_HEADS, HEAD_DIM)
    bcast = lambda a: jnp.broadcast_to(a[..., None], a.shape + (HEAD_DIM,))
    slope_a4 = bcast(slopes_a.reshape(A_KV_HEADS, A_GQA).T)[:, None]
    sink4 = bcast(sink.reshape(A_KV_HEADS, A_GQA).T)[:, None]
    slope_b4 = bcast(slopes_b.reshape(N_B_GROUPS, B_HEADS_PER_GROUP))[:, None]
    caches = (cache_a_kv[0], cache_b1_kv[0], cache_b2_kv[0], cache_b3_kv[0])
    oa_s, ob_s = _sample_attn(zs3, *caches, slope_a4, sink4, slope_b4)
    oa_s = oa_s.transpose(0, 2, 1, 3).reshape(n_dec, A_Q_HEADS * HEAD_DIM).astype(BF16)
    ob_s = ob_s.reshape(n_dec, B_HEADS_PER_GROUP * HEAD_DIM).astype(BF16)
    y_sample = _tail(xs, oa_s, ob_s, zgs, w, n_dec, n_dec, n_dec).reshape(n_dec, 1, d)

    news = [jnp.stack([zs3[:, HEAD_KA:HEAD_KA + A_KV_HEADS], zs3[:, HEAD_VA:HEAD_VA + A_KV_HEADS]], axis=1)]
    for g in range(N_B_GROUPS):
        o4 = g * B_HEADS_PER_GROUP
        news.append(jnp.stack([zs3[:, HEAD_KB + o4:HEAD_KB + o4 + B_HEADS_PER_GROUP],
                               zs3[:, HEAD_VB + o4:HEAD_VB + o4 + B_HEADS_PER_GROUP]], axis=1))
    kv_sample = [o[None] for o in _cache_update(caches, news)]

    return (y_prompt, y_sample, *kv_prompt, *kv_sample)
```

```python
import functools
import math

import jax
import jax.numpy as jnp
from jax import lax
from jax.experimental import pallas as pl
from jax.experimental.pallas import tpu as pltpu

F32 = jnp.float32
BF16 = jnp.bfloat16

D_MODEL = 2048
HEAD_DIM = 128
A_Q_HEADS = 8
A_KV_HEADS = 2
A_GQA = A_Q_HEADS // A_KV_HEADS
A_WINDOW = 128
B_GROUPS = ((128, 1), (512, 4), (2048, 16))
B_HEADS_PER_GROUP = 4
N_B_GROUPS = len(B_GROUPS)
B_HEADS = N_B_GROUPS * B_HEADS_PER_GROUP
BAND = 128
N_ALIBI_HEADS = A_Q_HEADS + B_HEADS
ATTN_SCALE = HEAD_DIM ** -0.5
PEER_HEADS = 8
PEER_NKEYS = 128
PEER_EXPERTS = PEER_NKEYS * PEER_NKEYS
PEER_DKEY = 128
PEER_TOPK = 16
NORM_EPS = 1e-6
NEG_INF = -1e30

QKV_W = (A_Q_HEADS + 2 * A_KV_HEADS + 3 * B_HEADS) * HEAD_DIM
GATE_W = 2 * D_MODEL
HEAD_QA, HEAD_KA, HEAD_VA = 0, A_Q_HEADS, A_Q_HEADS + A_KV_HEADS
HEAD_QB = A_Q_HEADS + 2 * A_KV_HEADS
HEAD_KB = HEAD_QB + B_HEADS
HEAD_VB = HEAD_KB + B_HEADS
N_QKV_HEADS = QKV_W // HEAD_DIM

VMEM_LIMIT_BYTES = 56 * 1024 * 1024
LANES = 128

NT_DIMS = (((1,), (1,)), ((), ()))
TN_DIMS = (((0,), (0,)), ((), ()))


def _params(n_grid_axes):
    return pltpu.CompilerParams(
        dimension_semantics=("arbitrary",) * n_grid_axes,
        vmem_limit_bytes=VMEM_LIMIT_BYTES)


def _rms(x):
    return x * lax.rsqrt(jnp.mean(x * x, axis=-1, keepdims=True) + NORM_EPS)


PROJ_TN = 512
PROJ_HEADS_PER_TILE = PROJ_TN // HEAD_DIM


def _proj_kernel(x_ref, n1_ref, w_ref, cw_ref, o_ref, xn_ref, *, kinds):
    j = pl.program_id(1)

    @pl.when(j == 0)
    def _():
        xn_ref[...] = (_rms(x_ref[...]) * n1_ref[...]).astype(BF16)

    z = jnp.dot(xn_ref[...], w_ref[...], preferred_element_type=F32)

    def epilogue(head_is_normed):
        for hh, normed in enumerate(head_is_normed):
            cs = slice(hh * HEAD_DIM, (hh + 1) * HEAD_DIM)
            zh = z[:, cs]
            if normed:
                zh = _rms(zh) * cw_ref[:, cs]
            o_ref[:, cs] = zh.astype(o_ref.dtype)

    for kind in sorted(set(kinds)):
        cond = functools.reduce(jnp.logical_or, [j == jj for jj, k in enumerate(kinds) if k == kind])
        if kind == "gate":
            @pl.when(cond)
            def _():
                o_ref[...] = jax.nn.sigmoid(z).astype(o_ref.dtype)
        else:
            @pl.when(cond)
            def _(kind=kind):
                epilogue(kind)


def _proj(x2d, n1, w, cw, kinds, out_dtype, tm, name):
    t, d = x2d.shape
    n = w.shape[1]
    assert t % tm == 0 and n == PROJ_TN * len(kinds)
    return pl.pallas_call(
        functools.partial(_proj_kernel, kinds=kinds),
        grid=(t // tm, len(kinds)),
        in_specs=[
            pl.BlockSpec((tm, d), lambda i, j: (i, 0)),
            pl.BlockSpec((1, d), lambda i, j: (0, 0)),
            pl.BlockSpec((d, PROJ_TN), lambda i, j: (0, j)),
            pl.BlockSpec((1, PROJ_TN), lambda i, j: (0, j)),
        ],
        out_specs=pl.BlockSpec((tm, PROJ_TN), lambda i, j: (i, j)),
        out_shape=jax.ShapeDtypeStruct((t, n), out_dtype),
        scratch_shapes=[pltpu.VMEM((tm, d), BF16)],
        compiler_params=_params(2),
        name=name,
    )(x2d, n1, w, cw)


_N4, _I4 = (True,) * 4, (False,) * 4
QKV_KINDS = (_N4, _N4, (True, True, False, False)) + (_N4,) * 6 + (_I4,) * 3
GATE_KINDS = ("gate",) * (GATE_W // PROJ_TN)


def _band_softmax(q, k, v, slope_dist, mask, sink):
    s = lax.dot_general(q, k, NT_DIMS, preferred_element_type=F32) * ATTN_SCALE
    logits = jnp.where(mask, s - slope_dist, NEG_INF)
    m = jnp.max(logits, axis=-1, keepdims=True)
    if sink is not None:
        m = jnp.maximum(m, sink)
    p = jnp.exp(logits - m)
    denom = jnp.sum(p, axis=-1, keepdims=True)
    if sink is not None:
        denom = denom + jnp.exp(sink - m)
    o = jnp.dot(p.astype(BF16), v, preferred_element_type=F32) / denom
    return o, m + jnp.log(denom)


def _band_geometry(has_prev):
    nk = 2 * BAND if has_prev else BAND
    qi = lax.broadcasted_iota(jnp.int32, (BAND, nk), 0)
    sj = lax.broadcasted_iota(jnp.int32, (BAND, nk), 1)
    dist = qi - sj + (BAND if has_prev else 0)
    return dist, sj


def _attn_a_kernel(slopes_ref, sink_ref, q_ref, k_ref, v_ref, o_ref):
    kvh = pl.program_id(1)
    n_blocks = q_ref.shape[1] // BAND
    dist, sj = _band_geometry(True)
    in_window = (dist >= 0) & (dist <= A_WINDOW)
    distf = dist.astype(F32)

    def block(blk, carry):
        cur = pl.multiple_of(blk * BAND, BAND)
        prev = pl.multiple_of(jnp.maximum(blk - 1, 0) * BAND, BAND)
        k = jnp.concatenate([k_ref[0, pl.ds(prev, BAND), :], k_ref[0, pl.ds(cur, BAND), :]], axis=0).astype(BF16)
        v = jnp.concatenate([v_ref[0, pl.ds(prev, BAND), :], v_ref[0, pl.ds(cur, BAND), :]], axis=0).astype(BF16)
        mask = in_window & (sj >= jnp.where(blk > 0, 0, BAND))
        for g in range(A_GQA):
            cs = slice(g * HEAD_DIM, (g + 1) * HEAD_DIM)
            head = kvh * A_GQA + g
            q = q_ref[0, pl.ds(cur, BAND), cs].astype(BF16)
            o, _ = _band_softmax(q, k, v, slopes_ref[head] * distf, mask, sink_ref[head])
            o_ref[0, pl.ds(cur, BAND), cs] = o.astype(o_ref.dtype)
        return carry

    lax.fori_loop(0, n_blocks, block, 0)


def _attn_a(zq3, slopes_a, sink):
    b, s, _ = zq3.shape
    gw = A_GQA * HEAD_DIM
    smem = pl.BlockSpec(memory_space=pltpu.SMEM)
    return pl.pallas_call(
        _attn_a_kernel,
        grid=(b, A_KV_HEADS),
        in_specs=[
            smem, smem,
            pl.BlockSpec((1, s, gw), lambda bi, h: (bi, 0, h)),
            pl.BlockSpec((1, s, HEAD_DIM), lambda bi, h: (bi, 0, HEAD_KA + h)),
            pl.BlockSpec((1, s, HEAD_DIM), lambda bi, h: (bi, 0, HEAD_VA + h)),
        ],
        out_specs=pl.BlockSpec((1, s, gw), lambda bi, h: (bi, 0, h)),
        out_shape=jax.ShapeDtypeStruct((b, s, A_Q_HEADS * HEAD_DIM), BF16),
        compiler_params=_params(2),
        name="attn_a",
    )(slopes_a, sink, zq3, zq3, zq3)


def _attn_b_kernel(slopes_ref, *refs):
    q_refs, k_refs, v_refs = refs[0:3], refs[3:6], refs[6:9]
    o_ref, og_ref, lse_ref = refs[9:12]
    h = pl.program_id(1)
    s_len = o_ref.shape[1]

    for g, (win, dil) in enumerate(B_GROUPS):
        n_blocks = s_len // (BAND * dil)
        slope = slopes_ref[g * B_HEADS_PER_GROUP + h]
        for has_prev in (False, True):
            dist, _ = _band_geometry(has_prev)
            mask = (dist >= 0) & (dist <= win // dil)
            slope_dist = slope * (dil * dist).astype(F32)
            for blk in range(n_blocks):
                if (blk > 0) != has_prev:
                    continue
                for r in range(dil):
                    def rows(ref, b0):
                        start = BAND * dil * b0 + r
                        if dil == 1:
                            return ref[0, pl.ds(start, BAND), :]
                        return ref[0, pl.ds(start, BAND, stride=dil), :]

                    q = rows(q_refs[g], blk).astype(BF16)
                    k = rows(k_refs[g], blk)
                    v = rows(v_refs[g], blk)
                    if has_prev:
                        k = jnp.concatenate([rows(k_refs[g], blk - 1), k], axis=0)
                        v = jnp.concatenate([rows(v_refs[g], blk - 1), v], axis=0)
                    o, lse = _band_softmax(q, k.astype(BF16), v.astype(BF16), slope_dist, mask, None)
                    start = BAND * dil * blk + r
                    idx = pl.ds(start, BAND) if dil == 1 else pl.ds(start, BAND, stride=dil)
                    og_ref[g, idx, :] = o
                    lse_ref[g, idx, :] = jnp.broadcast_to(lse, (BAND, HEAD_DIM))

    def combine(c, carry):
        rs = pl.ds(pl.multiple_of(c * BAND, BAND), BAND)
        lses = [lse_ref[g, rs, :] for g in range(N_B_GROUPS)]
        mx = functools.reduce(jnp.maximum, lses)
        ws = [jnp.exp(l - mx) for l in lses]
        acc = functools.reduce(lambda a, b_: a + b_, [w * og_ref[g, rs, :] for g, w in enumerate(ws)])
        o_ref[0, rs, :] = (acc / functools.reduce(lambda a, b_: a + b_, ws)).astype(o_ref.dtype)
        return carry

    lax.fori_loop(0, s_len // BAND, combine, 0)


def _attn_b(zq3, slopes_b):
    b, s, _ = zq3.shape
    smem = pl.BlockSpec(memory_space=pltpu.SMEM)

    def head_spec(base):
        return [pl.BlockSpec((1, s, HEAD_DIM), lambda bi, h, c=base + g * B_HEADS_PER_GROUP: (bi, 0, c + h))
                for g in range(N_B_GROUPS)]

    return pl.pallas_call(
        _attn_b_kernel,
        grid=(b, B_HEADS_PER_GROUP),
        in_specs=[smem] + head_spec(HEAD_QB) + head_spec(HEAD_KB) + head_spec(HEAD_VB),
        out_specs=pl.BlockSpec((1, s, HEAD_DIM), lambda bi, h: (bi, 0, h)),
        out_shape=jax.ShapeDtypeStruct((b, s, B_HEADS_PER_GROUP * HEAD_DIM), BF16),
        scratch_shapes=[pltpu.VMEM((N_B_GROUPS, s, HEAD_DIM), F32),
                        pltpu.VMEM((N_B_GROUPS, s, HEAD_DIM), F32)],
        compiler_params=_params(2),
        name="attn_b",
    )(slopes_b, *([zq3] * 9))


SAMPLE_BS = 8


def _window_read(q, k_new, v_new, c_ref, slope, dil, sink):
    n = c_ref.shape[1]
    k = c_ref[:, :, 0]
    v = c_ref[:, :, 1]
    s = jnp.sum(k * q[:, None], axis=-1, keepdims=True) * ATTN_SCALE
    steps = (n - lax.broadcasted_iota(jnp.int32, s.shape, 1)).astype(F32)
    logits = s - slope[:, None] * (steps * float(dil))
    s_new = jnp.sum(k_new * q, axis=-1, keepdims=True) * ATTN_SCALE
    m = jnp.maximum(jnp.max(logits, axis=1), s_new)
    if sink is not None:
        m = jnp.maximum(m, sink)
    p = jnp.exp(logits - m[:, None])
    p_new = jnp.exp(s_new - m)
    denom = jnp.sum(p, axis=1) + p_new
    if sink is not None:
        denom = denom + jnp.exp(sink - m)
    o = (jnp.sum(p * v, axis=1) + p_new * v_new) / denom
    return o, m + jnp.log(denom)


def _sample_attn_kernel(z_ref, ca_ref, cb1_ref, cb2_ref, cb3_ref, slope_a_ref, sink_ref, slope_b_ref,
                        oa_ref, ob_ref):
    z = z_ref[...]
    k_new = z[:, HEAD_KA:HEAD_KA + A_KV_HEADS]
    v_new = z[:, HEAD_VA:HEAD_VA + A_KV_HEADS]
    for g in range(A_GQA):
        q = jnp.concatenate([z[:, kv * A_GQA + g:kv * A_GQA + g + 1] for kv in range(A_KV_HEADS)], axis=1)
        o, _ = _window_read(q, k_new, v_new, ca_ref, slope_a_ref[g], 1, sink_ref[g])
        oa_ref[:, g] = o
    outs, lses = [], []
    for g, (c_ref, (_, dil)) in enumerate(zip((cb1_ref, cb2_ref, cb3_ref), B_GROUPS)):
        hs = lambda base: slice(base + g * B_HEADS_PER_GROUP, base + (g + 1) * B_HEADS_PER_GROUP)
        o, lse = _window_read(z[:, hs(HEAD_QB)], z[:, hs(HEAD_KB)], z[:, hs(HEAD_VB)], c_ref,
                              slope_b_ref[g], dil, None)
        outs.append(o)
        lses.append(lse)
    mx = functools.reduce(jnp.maximum, lses)
    ws = [jnp.exp(l - mx) for l in lses]
    acc = functools.reduce(lambda a, b_: a + b_, [w * o for w, o in zip(ws, outs)])
    ob_ref[...] = acc / functools.reduce(lambda a, b_: a + b_, ws)


def _sample_attn(zs3, cache_a, cache_b1, cache_b2, cache_b3, slope_a, sink, slope_b):
    n = zs3.shape[0]
    bs = SAMPLE_BS

    def strided_rows(c, dil):
        nb, l, two, hh, hd = c.shape
        assert l == A_WINDOW * dil
        c6 = c.reshape(nb, l // dil, dil, two, hh, hd)
        spec = pl.BlockSpec((bs, l // dil, None, two, hh, hd), lambda i: (i, 0, 0, 0, 0, 0))
        return c6, spec

    ca, ca_spec = strided_rows(cache_a, 1)
    cb = [strided_rows(c, dil) for c, (_, dil) in zip((cache_b1, cache_b2, cache_b3), B_GROUPS)]
    full = lambda a: pl.BlockSpec(a.shape, lambda i: (0,) * a.ndim)
    return pl.pallas_call(
        _sample_attn_kernel,
        grid=(n // bs,),
        in_specs=[pl.BlockSpec((bs, N_QKV_HEADS, HEAD_DIM), lambda i: (i, 0, 0)),
                  ca_spec, cb[0][1], cb[1][1], cb[2][1],
                  full(slope_a), full(sink), full(slope_b)],
        out_specs=[pl.BlockSpec((bs, A_GQA, A_KV_HEADS, HEAD_DIM), lambda i: (i, 0, 0, 0)),
                   pl.BlockSpec((bs, B_HEADS_PER_GROUP, HEAD_DIM), lambda i: (i, 0, 0))],
        out_shape=[jax.ShapeDtypeStruct((n, A_GQA, A_KV_HEADS, HEAD_DIM), F32),
                   jax.ShapeDtypeStruct((n, B_HEADS_PER_GROUP, HEAD_DIM), F32)],
        compiler_params=_params(1),
        name="sample_attn",
    )(zs3, ca, cb[0][0], cb[1][0], cb[2][0], slope_a, sink, slope_b)


CACHE_SLOTS = 3
CACHE_CHUNK_BYTES = 4 * 1024 * 1024


def _cache_ring(c, nw, o, nseq, rows, buf, sem_body, sem_tail, sem_out):
    n, l = c.shape[0], c.shape[1]
    parts = l // rows
    n_chunks = (n // nseq) * parts
    assert n % nseq == 0 and l % rows == 0 and n_chunks >= CACHE_SLOTS

    def where(k):
        return k % CACHE_SLOTS, pl.ds((k // parts) * nseq, nseq), (k % parts) * rows

    def body(k):
        slot, seqs, r0 = where(k)
        return pltpu.make_async_copy(c.at[seqs, pl.ds(r0 + 1, rows - 1)],
                                     buf.at[slot, :, pl.ds(0, rows - 1)], sem_body.at[slot])

    def tail_old(k):
        slot, seqs, r0 = where(k)
        return pltpu.make_async_copy(c.at[seqs, pl.ds(jnp.minimum(r0 + rows, l - 1), 1)],
                                     buf.at[slot, :, pl.ds(rows - 1, 1)], sem_tail.at[slot])

    def tail_new(k):
        slot, seqs, _ = where(k)
        return pltpu.make_async_copy(nw.at[seqs], buf.at[slot, :, rows - 1], sem_tail.at[slot])

    def write(k):
        slot, seqs, r0 = where(k)
        return pltpu.make_async_copy(buf.at[slot], o.at[seqs, pl.ds(r0, rows)], sem_out.at[slot])

    def read(k, action):
        action(body(k))
        if parts == 1:
            action(tail_new(k))
        else:
            is_last = (k % parts) == parts - 1
            pl.when(is_last)(lambda: action(tail_new(k)))
            pl.when(jnp.logical_not(is_last))(lambda: action(tail_old(k)))

    start = lambda cp: cp.start()
    wait = lambda cp: cp.wait()

    read(0, start)

    def step(k, carry):
        pl.when(k + 1 >= CACHE_SLOTS)(lambda: write(k + 1 - CACHE_SLOTS).wait())
        pl.when(k + 1 < n_chunks)(lambda: read(k + 1, start))
        read(k, wait)
        write(k).start()
        return carry

    lax.fori_loop(0, n_chunks, step, 0)
    for k in range(n_chunks - CACHE_SLOTS + 1, n_chunks):
        write(k).wait()


def _cache_chunk(c):
    n, l = c.shape[0], c.shape[1]
    seq_bytes = math.prod(c.shape[1:]) * c.dtype.itemsize
    if seq_bytes <= CACHE_CHUNK_BYTES:
        return min(n, CACHE_CHUNK_BYTES // seq_bytes), l
    return 1, l // (seq_bytes // CACHE_CHUNK_BYTES)


def _cache_kernel(*refs):
    n = len(refs) // 3
    caches, news, outs = refs[:n], refs[n:2 * n], refs[2 * n:]
    for c, nw, o in zip(caches, news, outs):
        nseq, rows = _cache_chunk(c)
        dma_sems = pltpu.SemaphoreType.DMA((CACHE_SLOTS,))
        pl.run_scoped(
            functools.partial(_cache_ring, c, nw, o, nseq, rows),
            pltpu.VMEM((CACHE_SLOTS, nseq, rows) + tuple(c.shape[2:]), c.dtype), dma_sems, dma_sems, dma_sems)


def _cache_update(caches, news):
    n = len(caches)
    any_spec = pl.BlockSpec(memory_space=pl.ANY)
    return pl.pallas_call(
        _cache_kernel,
        in_specs=[any_spec] * (2 * n),
        out_specs=[any_spec] * n,
        out_shape=[jax.ShapeDtypeStruct(c.shape, c.dtype) for c in caches],
        compiler_params=pltpu.CompilerParams(vmem_limit_bytes=VMEM_LIMIT_BYTES),
        name="cache_update",
    )(*caches, *news)


def _block_kernel(x_ref, oa_ref, ob_ref, ga_ref, gb_ref, wa_ref, wb_ref, wo_ref, n2_ref, h_ref, hn_ref):
    ya = jnp.dot(oa_ref[...], wa_ref[...], preferred_element_type=F32)
    yb = jnp.dot(ob_ref[...], wb_ref[...], preferred_element_type=F32)
    mix = (ga_ref[...].astype(F32) * ya + gb_ref[...].astype(F32) * yb).astype(BF16)
    h = x_ref[...] + jnp.dot(mix, wo_ref[...], preferred_element_type=F32)
    h_ref[...] = h
    hn_ref[...] = (_rms(h) * n2_ref[...]).astype(BF16)


def _block(x2d, oa, ob, zg, wa, wb, wo, n2, tm):
    t, d = x2d.shape
    row = lambda w: pl.BlockSpec((tm, w), lambda i: (i, 0))
    full = lambda a: pl.BlockSpec(a.shape, lambda i: (0, 0))
    return pl.pallas_call(
        _block_kernel,
        grid=(t // tm,),
        in_specs=[row(d), row(oa.shape[1]), row(ob.shape[1]),
                  pl.BlockSpec((tm, d), lambda i: (i, 0)), pl.BlockSpec((tm, d), lambda i: (i, 1)),
                  full(wa), full(wb), full(wo), full(n2)],
        out_specs=[row(d), row(d)],
        out_shape=[jax.ShapeDtypeStruct((t, d), F32), jax.ShapeDtypeStruct((t, d), BF16)],
        compiler_params=_params(1),
        name="block",
    )(x2d, oa, ob, zg, zg, wa, wb, wo, n2)


RANK_NONE = float(PEER_NKEYS)
CAND_ROWS = tuple(PEER_TOPK // (a + 1) for a in range(PEER_TOPK))
CAND_PAD = tuple(-(-r // 8) * 8 for r in CAND_ROWS)


def _extract(work, n_take, break_ties):
    rows = lax.broadcasted_iota(jnp.int32, work.shape, 0).astype(F32)
    rank = jnp.full(work.shape, RANK_NONE, F32)
    vals = []
    for kk in range(n_take):
        mx = jnp.max(work, axis=0, keepdims=True)
        sel = work == mx
        if break_ties:
            first = jnp.min(jnp.where(sel, rows, float(work.shape[0])), axis=0, keepdims=True)
            sel = rows == first
        rank = jnp.where(sel, float(kk), rank)
        work = jnp.where(sel, -jnp.inf, work)
        vals.append(mx)
    return rank, jnp.concatenate(vals, axis=0)


def _extract_top(work, n_take):
    rank, vals = _extract(work, n_take, break_ties=False)
    ranked = jnp.sum(jnp.where(rank < RANK_NONE, 1.0, 0.0), axis=0, keepdims=True)
    return lax.cond(jnp.max(ranked) > n_take,
                    lambda: _extract(work, n_take, break_ties=True), lambda: (rank, vals))


def _route_kernel(hn_ref, wq_ref, sk_ref, rank1_ref, p1_ref, m_ref, p0_ref, s_scr):
    q = jnp.dot(hn_ref[...], wq_ref[...], preferred_element_type=F32).astype(BF16)
    s_scr[...] = lax.dot_general(sk_ref[...], q, NT_DIMS, preferred_element_type=F32)

    def head(hh, carry):
        base = pl.multiple_of(hh * 2 * PEER_NKEYS, 2 * PEER_NKEYS)
        s0 = s_scr[pl.ds(base, PEER_NKEYS), :]
        s1 = s_scr[pl.ds(base + PEER_NKEYS, PEER_NKEYS), :]
        rank0, vals0 = _extract_top(s0, PEER_TOPK)
        rank1, vals1 = _extract_top(s1, PEER_TOPK)
        cand = []
        for a in range(PEER_TOPK):
            blk = vals0[a:a + 1] + vals1[:CAND_PAD[a]]
            rr = lax.broadcasted_iota(jnp.int32, blk.shape, 0)
            cand.append(jnp.where(rr < CAND_ROWS[a], blk, -jnp.inf))
        taken, best = _extract_top(jnp.concatenate(cand, axis=0), PEER_TOPK)
        zsum = jnp.sum(jnp.exp(best - best[0:1]), axis=0, keepdims=True)
        m = jnp.zeros_like(s0)
        off = 0
        for a in range(PEER_TOPK):
            cnt = jnp.sum(jnp.where(taken[off:off + CAND_PAD[a]] < RANK_NONE, 1.0, 0.0), axis=0, keepdims=True)
            m = jnp.where(rank0 == float(a), cnt, m)
            off += CAND_PAD[a]
        rank1_ref[hh] = rank1
        p1_ref[hh] = jnp.exp(s1 - vals1[0:1])
        m_ref[hh] = m
        p0_ref[hh] = jnp.exp(s0 - vals0[0:1]) / zsum
        return carry

    lax.fori_loop(0, PEER_HEADS, head, 0)


def _route(hn, wq, sk, tr):
    t, d = hn.shape
    table = jax.ShapeDtypeStruct((PEER_HEADS, PEER_NKEYS, t), F32)
    tspec = pl.BlockSpec((PEER_HEADS, PEER_NKEYS, tr), lambda i: (0, 0, i))
    full = lambda a: pl.BlockSpec(a.shape, lambda i: (0, 0))
    return pl.pallas_call(
        _route_kernel,
        grid=(t // tr,),
        in_specs=[pl.BlockSpec((tr, d), lambda i: (i, 0)), full(wq), full(sk)],
        out_specs=[tspec] * 4,
        out_shape=[table] * 4,
        scratch_shapes=[pltpu.VMEM((PEER_HEADS * 2 * PEER_NKEYS, tr), F32)],
        compiler_params=_params(1),
        name="route",
    )(hn, wq, sk)


PEER_TE = 512
PEER_KEYS_PER_BLOCK = PEER_TE // PEER_NKEYS
PEER_TABLE_KEYS = 8
PEER_BLOCKS_PER_TABLE = PEER_TABLE_KEYS // PEER_KEYS_PER_BLOCK
assert PEER_BLOCKS_PER_TABLE == 2
SQRT_HALF = math.sqrt(0.5)


PEER_GATE_ROWS = 64


def _peer_gate_phase(act_ref, w_ref, rank1_ref, p1_ref, m_ref, p0_ref, key_off):
    tm = act_ref.shape[1]
    for c in range(tm // LANES):
        ls = slice(c * LANES, (c + 1) * LANES)
        for r0 in range(0, PEER_NKEYS, PEER_GATE_ROWS):
            gates = [jnp.zeros((PEER_GATE_ROWS, LANES), F32) for _ in range(PEER_KEYS_PER_BLOCK)]
            for hh in range(PEER_HEADS):
                r1 = rank1_ref[hh, r0:r0 + PEER_GATE_ROWS, ls]
                p1 = p1_ref[hh, r0:r0 + PEER_GATE_ROWS, ls]
                for j in range(PEER_KEYS_PER_BLOCK):
                    kr = slice(key_off + j, key_off + j + 1)
                    gates[j] = gates[j] + jnp.where(r1 < m_ref[hh, kr, ls], p1, 0.0) * p0_ref[hh, kr, ls]
            for j in range(PEER_KEYS_PER_BLOCK):
                rs = slice(j * PEER_NKEYS + r0, j * PEER_NKEYS + r0 + PEER_GATE_ROWS)
                a = act_ref[rs, ls]
                w_ref[rs, ls] = (gates[j] * (0.5 * a * (1.0 + lax.erf(a * SQRT_HALF)))).astype(BF16)


def _peer_kernel(h_ref, hn_ref, u_ref, v_ref, rank1_ref, p1_ref, m_ref, p0_ref, o_ref,
                 act0, act1, w0, w1, *, n_blocks):
    g = pl.program_id(0)

    @pl.when(g == 0)
    def _():
        for ref in (act0, act1, w0, w1, o_ref):
            ref[...] = jnp.zeros(ref.shape, ref.dtype)

    first = (g < 2) | (lax.rem(g - 2, n_blocks) == 0)

    def body(act_new, act_cur, w_new, w_old, key_off):
        act_new[...] = lax.dot_general(u_ref[...], hn_ref[...], NT_DIMS, preferred_element_type=F32)
        _peer_gate_phase(act_cur, w_new, rank1_ref, p1_ref, m_ref, p0_ref, key_off)
        y = lax.dot_general(w_old[...], v_ref[...], TN_DIMS, preferred_element_type=F32)
        o_ref[...] = jnp.where(first, h_ref[...], o_ref[...]) + y

    even = lax.rem(g, 2) == 0
    pl.when(even)(lambda: body(act0, act1, w1, w0, PEER_KEYS_PER_BLOCK))
    pl.when(jnp.logical_not(even))(lambda: body(act1, act0, w0, w1, 0))


def _peer(h, hn, u, v, tables, tm):
    t, d = h.shape
    nb = u.shape[0] // PEER_TE
    assert nb % PEER_BLOCKS_PER_TABLE == 0
    n = (t // tm) * nb
    blk1 = lambda g: jnp.minimum(g, n - 1)
    blk2 = lambda g: jnp.clip(g - 1, 0, n - 1)
    blk3 = lambda g: jnp.clip(g - 2, 0, n - 1)
    tspec = pl.BlockSpec((PEER_HEADS, PEER_NKEYS, tm), lambda g: (0, 0, blk2(g) // nb))
    kspec = pl.BlockSpec((PEER_HEADS, PEER_TABLE_KEYS, tm),
                         lambda g: (0, (blk2(g) % nb) // PEER_BLOCKS_PER_TABLE, blk2(g) // nb))
    return pl.pallas_call(
        functools.partial(_peer_kernel, n_blocks=nb),
        grid=(n + 2,),
        in_specs=[pl.BlockSpec((tm, d), lambda g: (blk3(g) // nb, 0)),
                  pl.BlockSpec((tm, d), lambda g: (blk1(g) // nb, 0)),
                  pl.BlockSpec((PEER_TE, d), lambda g: (blk1(g) % nb, 0)),
                  pl.BlockSpec((PEER_TE, d), lambda g: (blk3(g) % nb, 0)), tspec, tspec, kspec, kspec],
        out_specs=pl.BlockSpec((tm, d), lambda g: (blk3(g) // nb, 0)),
        out_shape=jax.ShapeDtypeStruct((t, d), F32),
        scratch_shapes=[pltpu.VMEM((PEER_TE, tm), F32), pltpu.VMEM((PEER_TE, tm), F32),
                        pltpu.VMEM((PEER_TE, tm), BF16), pltpu.VMEM((PEER_TE, tm), BF16)],
        compiler_params=_params(1),
        name="peer",
    )(h, hn, u, v, *tables)


def _alibi_slopes():
    return 2.0 ** (-8.0 * jnp.arange(1, N_ALIBI_HEADS + 1, dtype=F32) / N_ALIBI_HEADS)


def _subkey_matrix(subkeys):
    two, nk, dh = subkeys.shape
    eye = jnp.eye(PEER_HEADS * two, dtype=subkeys.dtype).reshape(PEER_HEADS, two, PEER_HEADS, two)
    sk = jnp.einsum("hcgb,cnd->hcngbd", eye, subkeys)
    return sk.reshape(PEER_HEADS * two * nk, PEER_HEADS * two * dh)


def _tail(x2d, oa, ob, zg, w, tm_block, tr, tm_peer):
    h, hn = _block(x2d, oa, ob, zg, w["wa"], w["wb"], w["wo"], w["n2"], tm_block)
    tables = _route(hn, w["wq"], w["sk"], tr)
    return _peer(h, hn, w["u"], w["v"], tables, tm_peer)


def kernel(x_prompt, x_sample, cache_a_kv, cache_b1_kv, cache_b2_kv, cache_b3_kv, norm1_w, w_in, q_norm_a,
           k_norm_a, sink_a, q_norm_b, k_norm_b, w_branch_a, w_branch_b, w_out, norm2_w, peer_wq,
           peer_subkeys, peer_u, peer_v):
    assert norm1_w.shape[0] == 1, "single layer"
    b, s, d = x_prompt.shape
    n_dec = x_sample.shape[0]
    assert x_sample.shape[1] == 1

    slopes = _alibi_slopes()
    slopes_a, slopes_b = slopes[:A_Q_HEADS], slopes[A_Q_HEADS:]
    sink = sink_a[0].astype(F32)
    ones = jnp.ones((HEAD_DIM,), F32)
    col_w = jnp.concatenate(
        [jnp.tile(q_norm_a[0], A_Q_HEADS), jnp.tile(k_norm_a[0], A_KV_HEADS), jnp.tile(ones, A_KV_HEADS),
         jnp.tile(q_norm_b[0], B_HEADS), jnp.tile(k_norm_b[0], B_HEADS), jnp.tile(ones, B_HEADS)]
    ).astype(F32)[None]
    gate_w = jnp.ones((1, GATE_W), F32)
    n1 = norm1_w[0].astype(F32)[None]
    w_qkv = w_in[0, :, :QKV_W].astype(BF16)
    w_gate = w_in[0, :, QKV_W:].astype(BF16)
    w = dict(wa=w_branch_a[0].astype(BF16), wb=w_branch_b[0].astype(BF16), wo=w_out[0].astype(BF16),
             n2=norm2_w[0].astype(F32)[None], wq=peer_wq[0].astype(BF16),
             sk=_subkey_matrix(peer_subkeys[0]).astype(BF16),
             u=peer_u[0].astype(BF16), v=peer_v[0].astype(BF16))

    xp = x_prompt.reshape(b * s, d)
    zq = _proj(xp, n1, w_qkv, col_w, QKV_KINDS, F32, 1024, "proj_qkv")
    zg = _proj(xp, n1, w_gate, gate_w, GATE_KINDS, BF16, 1024, "proj_gate")
    zq3 = zq.reshape(b, s, QKV_W)
    oa = _attn_a(zq3, slopes_a, sink).reshape(b * s, A_Q_HEADS * HEAD_DIM)
    ob = _attn_b(zq3, slopes_b).reshape(b * s, B_HEADS_PER_GROUP * HEAD_DIM)
    y_prompt = _tail(xp, oa, ob, zg, w, 256, 256, 512).reshape(b, s, d)

    def window(k0, v0, nh, length):
        part = lambda h0: zq3[:, s - length:, h0 * HEAD_DIM:(h0 + nh) * HEAD_DIM].reshape(b, length, nh, HEAD_DIM)
        return jnp.stack([part(k0), part(v0)], axis=2)[None]

    kv_prompt = [window(HEAD_KA, HEAD_VA, A_KV_HEADS, min(A_WINDOW, s))]
    for g, (win, _) in enumerate(B_GROUPS):
        o4 = g * B_HEADS_PER_GROUP
        kv_prompt.append(window(HEAD_KB + o4, HEAD_VB + o4, B_HEADS_PER_GROUP, min(win, s)))

    xs = x_sample.reshape(n_dec, d)
    zs = _proj(xs, n1, w_qkv, col_w, QKV_KINDS, F32, n_dec, "proj_qkv_s")
    zgs = _proj(xs, n1, w_gate, gate_w, GATE_KINDS, BF16, n_dec, "proj_gate_s")
    zs3 = zs.reshape(n_dec, N_QKV_HEADS, HEAD_DIM)
    bcast = lambda a: jnp.broadcast_to(a[..., None], a.shape + (HEAD_DIM,))
    slope_a4 = bcast(slopes_a.reshape(A_KV_HEADS, A_GQA).T)[:, None]
    sink4 = bcast(sink.reshape(A_KV_HEADS, A_GQA).T)[:, None]
    slope_b4 = bcast(slopes_b.reshape(N_B_GROUPS, B_HEADS_PER_GROUP))[:, None]
    caches = (cache_a_kv[0], cache_b1_kv[0], cache_b2_kv[0], cache_b3_kv[0])
    oa_s, ob_s = _sample_attn(zs3, *caches, slope_a4, sink4, slope_b4)
    oa_s = oa_s.transpose(0, 2, 1, 3).reshape(n_dec, A_Q_HEADS * HEAD_DIM).astype(BF16)
    ob_s = ob_s.reshape(n_dec, B_HEADS_PER_GROUP * HEAD_DIM).astype(BF16)
    y_sample = _tail(xs, oa_s, ob_s, zgs, w, n_dec, n_dec, n_dec).reshape(n_dec, 1, d)

    news = [jnp.stack([zs3[:, HEAD_KA:HEAD_KA + A_KV_HEADS], zs3[:, HEAD_VA:HEAD_VA + A_KV_HEADS]], axis=1)]
    for g in range(N_B_GROUPS):
        o4 = g * B_HEADS_PER_GROUP
        news.append(jnp.stack([zs3[:, HEAD_KB + o4:HEAD_KB + o4 + B_HEADS_PER_GROUP],
                               zs3[:, HEAD_VB + o4:HEAD_VB + o4 + B_HEADS_PER_GROUP]], axis=1))
    kv_sample = [o[None] for o in _cache_update(caches, news)]

    return (y_prompt, y_sample, *kv_prompt, *kv_sample)
```

```python
import functools
import math

import jax
import jax.numpy as jnp
from jax import lax
from jax.experimental import pallas as pl
from jax.experimental.pallas import tpu as pltpu

F32 = jnp.float32
BF16 = jnp.bfloat16

D_MODEL = 2048
HEAD_DIM = 128
A_Q_HEADS = 8
A_KV_HEADS = 2
A_GQA = A_Q_HEADS // A_KV_HEADS
A_WINDOW = 128
B_GROUPS = ((128, 1), (512, 4), (2048, 16))
B_HEADS_PER_GROUP = 4
N_B_GROUPS = len(B_GROUPS)
B_HEADS = N_B_GROUPS * B_HEADS_PER_GROUP
BAND = 128
N_ALIBI_HEADS = A_Q_HEADS + B_HEADS
ATTN_SCALE = HEAD_DIM ** -0.5
PEER_HEADS = 8
PEER_NKEYS = 128
PEER_EXPERTS = PEER_NKEYS * PEER_NKEYS
PEER_DKEY = 128
PEER_TOPK = 16
NORM_EPS = 1e-6
NEG_INF = -1e30

QKV_W = (A_Q_HEADS + 2 * A_KV_HEADS + 3 * B_HEADS) * HEAD_DIM
GATE_W = 2 * D_MODEL
HEAD_QA, HEAD_KA, HEAD_VA = 0, A_Q_HEADS, A_Q_HEADS + A_KV_HEADS
HEAD_QB = A_Q_HEADS + 2 * A_KV_HEADS
HEAD_KB = HEAD_QB + B_HEADS
HEAD_VB = HEAD_KB + B_HEADS
N_QKV_HEADS = QKV_W // HEAD_DIM

VMEM_LIMIT_BYTES = 56 * 1024 * 1024
LANES = 128

NT_DIMS = (((1,), (1,)), ((), ()))
TN_DIMS = (((0,), (0,)), ((), ()))


def _params(n_grid_axes):
    return pltpu.CompilerParams(
        dimension_semantics=("arbitrary",) * n_grid_axes,
        vmem_limit_bytes=VMEM_LIMIT_BYTES)


def _rms(x):
    return x * lax.rsqrt(jnp.mean(x * x, axis=-1, keepdims=True) + NORM_EPS)


PROJ_TN = 512
PROJ_HEADS_PER_TILE = PROJ_TN // HEAD_DIM


def _proj_kernel(x_ref, n1_ref, w_ref, cw_ref, o_ref, xn_ref, *, kinds):
    j = pl.program_id(1)

    @pl.when(j == 0)
    def _():
        xn_ref[...] = (_rms(x_ref[...]) * n1_ref[...]).astype(BF16)

    z = jnp.dot(xn_ref[...], w_ref[...], preferred_element_type=F32)

    def epilogue(head_is_normed):
        for hh, normed in enumerate(head_is_normed):
            cs = slice(hh * HEAD_DIM, (hh + 1) * HEAD_DIM)
            zh = z[:, cs]
            if normed:
                zh = _rms(zh) * cw_ref[:, cs]
            o_ref[:, cs] = zh.astype(o_ref.dtype)

    for kind in sorted(set(kinds)):
        cond = functools.reduce(jnp.logical_or, [j == jj for jj, k in enumerate(kinds) if k == kind])
        if kind == "gate":
            @pl.when(cond)
            def _():
                o_ref[...] = jax.nn.sigmoid(z).astype(o_ref.dtype)
        else:
            @pl.when(cond)
            def _(kind=kind):
                epilogue(kind)


def _proj(x2d, n1, w, cw, kinds, out_dtype, tm, name):
    t, d = x2d.shape
    n = w.shape[1]
    assert t % tm == 0 and n == PROJ_TN * len(kinds)
    return pl.pallas_call(
        functools.partial(_proj_kernel, kinds=kinds),
        grid=(t // tm, len(kinds)),
        in_specs=[
            pl.BlockSpec((tm, d), lambda i, j: (i, 0)),
            pl.BlockSpec((1, d), lambda i, j: (0, 0)),
            pl.BlockSpec((d, PROJ_TN), lambda i, j: (0, j)),
            pl.BlockSpec((1, PROJ_TN), lambda i, j: (0, j)),
        ],
        out_specs=pl.BlockSpec((tm, PROJ_TN), lambda i, j: (i, j)),
        out_shape=jax.ShapeDtypeStruct((t, n), out_dtype),
        scratch_shapes=[pltpu.VMEM((tm, d), BF16)],
        compiler_params=_params(2),
        name=name,
    )(x2d, n1, w, cw)


_N4, _I4 = (True,) * 4, (False,) * 4
QKV_KINDS = (_N4, _N4, (True, True, False, False)) + (_N4,) * 6 + (_I4,) * 3
GATE_KINDS = ("gate",) * (GATE_W // PROJ_TN)


def _band_softmax(q, k, v, slope_dist, mask, sink):
    s = lax.dot_general(q, k, NT_DIMS, preferred_element_type=F32) * ATTN_SCALE
    logits = jnp.where(mask, s - slope_dist, NEG_INF)
    m = jnp.max(logits, axis=-1, keepdims=True)
    if sink is not None:
        m = jnp.maximum(m, sink)
    p = jnp.exp(logits - m)
    denom = jnp.sum(p, axis=-1, keepdims=True)
    if sink is not None:
        denom = denom + jnp.exp(sink - m)
    o = jnp.dot(p.astype(BF16), v, preferred_element_type=F32) / denom
    return o, m + jnp.log(denom)


def _band_geometry(has_prev):
    nk = 2 * BAND if has_prev else BAND
    qi = lax.broadcasted_iota(jnp.int32, (BAND, nk), 0)
    sj = lax.broadcasted_iota(jnp.int32, (BAND, nk), 1)
    dist = qi - sj + (BAND if has_prev else 0)
    return dist, sj


def _attn_a_kernel(slopes_ref, sink_ref, q_ref, k_ref, v_ref, o_ref):
    kvh = pl.program_id(1)
    n_blocks = q_ref.shape[1] // BAND
    dist, sj = _band_geometry(True)
    in_window = (dist >= 0) & (dist <= A_WINDOW)
    distf = dist.astype(F32)

    def block(blk, carry):
        cur = pl.multiple_of(blk * BAND, BAND)
        prev = pl.multiple_of(jnp.maximum(blk - 1, 0) * BAND, BAND)
        k = jnp.concatenate([k_ref[0, pl.ds(prev, BAND), :], k_ref[0, pl.ds(cur, BAND), :]], axis=0).astype(BF16)
        v = jnp.concatenate([v_ref[0, pl.ds(prev, BAND), :], v_ref[0, pl.ds(cur, BAND), :]], axis=0).astype(BF16)
        mask = in_window & (sj >= jnp.where(blk > 0, 0, BAND))
        for g in range(A_GQA):
            cs = slice(g * HEAD_DIM, (g + 1) * HEAD_DIM)
            head = kvh * A_GQA + g
            q = q_ref[0, pl.ds(cur, BAND), cs].astype(BF16)
            o, _ = _band_softmax(q, k, v, slopes_ref[head] * distf, mask, sink_ref[head])
            o_ref[0, pl.ds(cur, BAND), cs] = o.astype(o_ref.dtype)
        return carry

    lax.fori_loop(0, n_blocks, block, 0)


def _attn_a(zq3, slopes_a, sink):
    b, s, _ = zq3.shape
    gw = A_GQA * HEAD_DIM
    smem = pl.BlockSpec(memory_space=pltpu.SMEM)
    return pl.pallas_call(
        _attn_a_kernel,
        grid=(b, A_KV_HEADS),
        in_specs=[
            smem, smem,
            pl.BlockSpec((1, s, gw), lambda bi, h: (bi, 0, h)),
            pl.BlockSpec((1, s, HEAD_DIM), lambda bi, h: (bi, 0, HEAD_KA + h)),
            pl.BlockSpec((1, s, HEAD_DIM), lambda bi, h: (bi, 0, HEAD_VA + h)),
        ],
        out_specs=pl.BlockSpec((1, s, gw), lambda bi, h: (bi, 0, h)),
        out_shape=jax.ShapeDtypeStruct((b, s, A_Q_HEADS * HEAD_DIM), BF16),
        compiler_params=_params(2),
        name="attn_a",
    )(slopes_a, sink, zq3, zq3, zq3)


def _attn_b_kernel(slopes_ref, *refs):
    q_refs, k_refs, v_refs = refs[0:3], refs[3:6], refs[6:9]
    o_ref, og_ref, lse_ref = refs[9:12]
    h = pl.program_id(1)
    s_len = o_ref.shape[1]

    for g, (win, dil) in enumerate(B_GROUPS):
        n_blocks = s_len // (BAND * dil)
        slope = slopes_ref[g * B_HEADS_PER_GROUP + h]
        for has_prev in (False, True):
            dist, _ = _band_geometry(has_prev)
            mask = (dist >= 0) & (dist <= win // dil)
            slope_dist = slope * (dil * dist).astype(F32)
            for blk in range(n_blocks):
                if (blk > 0) != has_prev:
                    continue
                for r in range(dil):
                    def rows(ref, b0):
                        start = BAND * dil * b0 + r
                        if dil == 1:
                            return ref[0, pl.ds(start, BAND), :]
                        return ref[0, pl.ds(start, BAND, stride=dil), :]

                    q = rows(q_refs[g], blk).astype(BF16)
                    k = rows(k_refs[g], blk)
                    v = rows(v_refs[g], blk)
                    if has_prev:
                        k = jnp.concatenate([rows(k_refs[g], blk - 1), k], axis=0)
                        v = jnp.concatenate([rows(v_refs[g], blk - 1), v], axis=0)
                    o, lse = _band_softmax(q, k.astype(BF16), v.astype(BF16), slope_dist, mask, None)
                    start = BAND * dil * blk + r
                    idx = pl.ds(start, BAND) if dil == 1 else pl.ds(start, BAND, stride=dil)
                    og_ref[g, idx, :] = o
                    lse_ref[g, idx, :] = jnp.broadcast_to(lse, (BAND, HEAD_DIM))

    def combine(c, carry):
        rs = pl.ds(pl.multiple_of(c * BAND, BAND), BAND)
        lses = [lse_ref[g, rs, :] for g in range(N_B_GROUPS)]
        mx = functools.reduce(jnp.maximum, lses)
        ws = [jnp.exp(l - mx) for l in lses]
        acc = functools.reduce(lambda a, b_: a + b_, [w * og_ref[g, rs, :] for g, w in enumerate(ws)])
        o_ref[0, rs, :] = (acc / functools.reduce(lambda a, b_: a + b_, ws)).astype(o_ref.dtype)
        return carry

    lax.fori_loop(0, s_len // BAND, combine, 0)


def _attn_b(zq3, slopes_b):
    b, s, _ = zq3.shape
    smem = pl.BlockSpec(memory_space=pltpu.SMEM)

    def head_spec(base):
        return [pl.BlockSpec((1, s, HEAD_DIM), lambda bi, h, c=base + g * B_HEADS_PER_GROUP: (bi, 0, c + h))
                for g in range(N_B_GROUPS)]

    return pl.pallas_call(
        _attn_b_kernel,
        grid=(b, B_HEADS_PER_GROUP),
        in_specs=[smem] + head_spec(HEAD_QB) + head_spec(HEAD_KB) + head_spec(HEAD_VB),
        out_specs=pl.BlockSpec((1, s, HEAD_DIM), lambda bi, h: (bi, 0, h)),
        out_shape=jax.ShapeDtypeStruct((b, s, B_HEADS_PER_GROUP * HEAD_DIM), BF16),
        scratch_shapes=[pltpu.VMEM((N_B_GROUPS, s, HEAD_DIM), F32),
                        pltpu.VMEM((N_B_GROUPS, s, HEAD_DIM), F32)],
        compiler_params=_params(2),
        name="attn_b",
    )(slopes_b, *([zq3] * 9))


SAMPLE_BS = 8


def _window_read(q, k_new, v_new, c_ref, slope, dil, sink):
    n = c_ref.shape[1]
    k = c_ref[:, :, 0]
    v = c_ref[:, :, 1]
    s = jnp.sum(k * q[:, None], axis=-1, keepdims=True) * ATTN_SCALE
    steps = (n - lax.broadcasted_iota(jnp.int32, s.shape, 1)).astype(F32)
    logits = s - slope[:, None] * (steps * float(dil))
    s_new = jnp.sum(k_new * q, axis=-1, keepdims=True) * ATTN_SCALE
    m = jnp.maximum(jnp.max(logits, axis=1), s_new)
    if sink is not None:
        m = jnp.maximum(m, sink)
    p = jnp.exp(logits - m[:, None])
    p_new = jnp.exp(s_new - m)
    denom = jnp.sum(p, axis=1) + p_new
    if sink is not None:
        denom = denom + jnp.exp(sink - m)
    o = (jnp.sum(p * v, axis=1) + p_new * v_new) / denom
    return o, m + jnp.log(denom)


def _sample_attn_kernel(z_ref, ca_ref, cb1_ref, cb2_ref, cb3_ref, slope_a_ref, sink_ref, slope_b_ref,
                        oa_ref, ob_ref):
    z = z_ref[...]
    k_new = z[:, HEAD_KA:HEAD_KA + A_KV_HEADS]
    v_new = z[:, HEAD_VA:HEAD_VA + A_KV_HEADS]
    for g in range(A_GQA):
        q = jnp.concatenate([z[:, kv * A_GQA + g:kv * A_GQA + g + 1] for kv in range(A_KV_HEADS)], axis=1)
        o, _ = _window_read(q, k_new, v_new, ca_ref, slope_a_ref[g], 1, sink_ref[g])
        oa_ref[:, g] = o
    outs, lses = [], []
    for g, (c_ref, (_, dil)) in enumerate(zip((cb1_ref, cb2_ref, cb3_ref), B_GROUPS)):
        hs = lambda base: slice(base + g * B_HEADS_PER_GROUP, base + (g + 1) * B_HEADS_PER_GROUP)
        o, lse = _window_read(z[:, hs(HEAD_QB)], z[:, hs(HEAD_KB)], z[:, hs(HEAD_VB)], c_ref,
                              slope_b_ref[g], dil, None)
        outs.append(o)
        lses.append(lse)
    mx = functools.reduce(jnp.maximum, lses)
    ws = [jnp.exp(l - mx) for l in lses]
    acc = functools.reduce(lambda a, b_: a + b_, [w * o for w, o in zip(ws, outs)])
    ob_ref[...] = acc / functools.reduce(lambda a, b_: a + b_, ws)


def _sample_attn(zs3, cache_a, cache_b1, cache_b2, cache_b3, slope_a, sink, slope_b):
    n = zs3.shape[0]
    bs = SAMPLE_BS

    def strided_rows(c, dil):
        nb, l, two, hh, hd = c.shape
        assert l == A_WINDOW * dil
        c6 = c.reshape(nb, l // dil, dil, two, hh, hd)
        spec = pl.BlockSpec((bs, l // dil, None, two, hh, hd), lambda i: (i, 0, 0, 0, 0, 0))
        return c6, spec

    ca, ca_spec = strided_rows(cache_a, 1)
    cb = [strided_rows(c, dil) for c, (_, dil) in zip((cache_b1, cache_b2, cache_b3), B_GROUPS)]
    full = lambda a: pl.BlockSpec(a.shape, lambda i: (0,) * a.ndim)
    return pl.pallas_call(
        _sample_attn_kernel,
        grid=(n // bs,),
        in_specs=[pl.BlockSpec((bs, N_QKV_HEADS, HEAD_DIM), lambda i: (i, 0, 0)),
                  ca_spec, cb[0][1], cb[1][1], cb[2][1],
                  full(slope_a), full(sink), full(slope_b)],
        out_specs=[pl.BlockSpec((bs, A_GQA, A_KV_HEADS, HEAD_DIM), lambda i: (i, 0, 0, 0)),
                   pl.BlockSpec((bs, B_HEADS_PER_GROUP, HEAD_DIM), lambda i: (i, 0, 0))],
        out_shape=[jax.ShapeDtypeStruct((n, A_GQA, A_KV_HEADS, HEAD_DIM), F32),
                   jax.ShapeDtypeStruct((n, B_HEADS_PER_GROUP, HEAD_DIM), F32)],
        compiler_params=_params(1),
        name="sample_attn",
    )(zs3, ca, cb[0][0], cb[1][0], cb[2][0], slope_a, sink, slope_b)


CACHE_SLOTS = 3
CACHE_CHUNK_BYTES = 4 * 1024 * 1024


class _ShiftRing:
    def __init__(self, c, nw, o, buf, sem_body, sem_tail, sem_out):
        self.c, self.nw, self.o, self.buf = c, nw, o, buf
        self.sem_body, self.sem_tail, self.sem_out = sem_body, sem_tail, sem_out
        self.n_slots, self.nseq, self.rows = buf.shape[:3]
        n, self.l = c.shape[0], c.shape[1]
        assert n % self.nseq == 0 and self.l % self.rows == 0
        self.parts = self.l // self.rows
        self.n_chunks = (n // self.nseq) * self.parts

    def _where(self, k):
        return k % self.n_slots, pl.ds((k // self.parts) * self.nseq, self.nseq), (k % self.parts) * self.rows

    def _body(self, k):
        slot, seqs, r0 = self._where(k)
        return pltpu.make_async_copy(self.c.at[seqs, pl.ds(r0 + 1, self.rows - 1)],
                                     self.buf.at[slot, :, pl.ds(0, self.rows - 1)], self.sem_body.at[slot])

    def _tail_old(self, k):
        slot, seqs, r0 = self._where(k)
        return pltpu.make_async_copy(self.c.at[seqs, pl.ds(jnp.minimum(r0 + self.rows, self.l - 1), 1)],
                                     self.buf.at[slot, :, pl.ds(self.rows - 1, 1)], self.sem_tail.at[slot])

    def _tail_new(self, k):
        slot, seqs, _ = self._where(k)
        return pltpu.make_async_copy(self.nw.at[seqs], self.buf.at[slot, :, self.rows - 1], self.sem_tail.at[slot])

    def write(self, k):
        slot, seqs, r0 = self._where(k)
        return pltpu.make_async_copy(self.buf.at[slot], self.o.at[seqs, pl.ds(r0, self.rows)], self.sem_out.at[slot])

    def _read(self, k, action):
        action(self._body(k))
        if self.parts == 1:
            action(self._tail_new(k))
        else:
            is_last = (k % self.parts) == self.parts - 1
            pl.when(is_last)(lambda: action(self._tail_new(k)))
            pl.when(jnp.logical_not(is_last))(lambda: action(self._tail_old(k)))

    def start_read(self, k):
        self._read(k, lambda cp: cp.start())

    def wait_read(self, k):
        self._read(k, lambda cp: cp.wait())

    def step(self, k):
        n = self.n_chunks
        written = k + 1 - self.n_slots
        pl.when(k == 0)(lambda: self.start_read(0))
        pl.when((written >= 0) & (written < n))(lambda: self.write(jnp.clip(written, 0, n - 1)).wait())
        pl.when(k + 1 < n)(lambda: self.start_read(jnp.minimum(k + 1, n - 1)))

        @pl.when(k < n)
        def _():
            self.wait_read(jnp.minimum(k, n - 1))
            self.write(jnp.minimum(k, n - 1)).start()

    def drain(self, done_steps):
        for k in range(max(done_steps - self.n_slots + 1, 0), self.n_chunks):
            self.write(k).wait()


def _cache_ring(c, nw, o, buf, sem_body, sem_tail, sem_out):
    ring = _ShiftRing(c, nw, o, buf, sem_body, sem_tail, sem_out)

    def step(k, carry):
        ring.step(k)
        return carry

    lax.fori_loop(0, ring.n_chunks, step, 0)
    ring.drain(ring.n_chunks)


def _cache_chunk(c):
    n, l = c.shape[0], c.shape[1]
    seq_bytes = math.prod(c.shape[1:]) * c.dtype.itemsize
    if seq_bytes <= CACHE_CHUNK_BYTES:
        return min(n, CACHE_CHUNK_BYTES // seq_bytes), l
    return 1, l // (seq_bytes // CACHE_CHUNK_BYTES)


def _cache_kernel(*refs):
    n = len(refs) // 3
    caches, news, outs = refs[:n], refs[n:2 * n], refs[2 * n:]
    for c, nw, o in zip(caches, news, outs):
        nseq, rows = _cache_chunk(c)
        dma_sems = pltpu.SemaphoreType.DMA((CACHE_SLOTS,))
        pl.run_scoped(
            functools.partial(_cache_ring, c, nw, o),
            pltpu.VMEM((CACHE_SLOTS, nseq, rows) + tuple(c.shape[2:]), c.dtype), dma_sems, dma_sems, dma_sems)


def _cache_update(caches, news):
    n = len(caches)
    any_spec = pl.BlockSpec(memory_space=pl.ANY)
    return pl.pallas_call(
        _cache_kernel,
        in_specs=[any_spec] * (2 * n),
        out_specs=[any_spec] * n,
        out_shape=[jax.ShapeDtypeStruct(c.shape, c.dtype) for c in caches],
        compiler_params=pltpu.CompilerParams(vmem_limit_bytes=VMEM_LIMIT_BYTES),
        name="cache_update",
    )(*caches, *news)


def _block_kernel(x_ref, oa_ref, ob_ref, ga_ref, gb_ref, wa_ref, wb_ref, wo_ref, n2_ref, h_ref, hn_ref):
    ya = jnp.dot(oa_ref[...], wa_ref[...], preferred_element_type=F32)
    yb = jnp.dot(ob_ref[...], wb_ref[...], preferred_element_type=F32)
    mix = (ga_ref[...].astype(F32) * ya + gb_ref[...].astype(F32) * yb).astype(BF16)
    h = x_ref[...] + jnp.dot(mix, wo_ref[...], preferred_element_type=F32)
    h_ref[...] = h
    hn_ref[...] = (_rms(h) * n2_ref[...]).astype(BF16)


def _block(x2d, oa, ob, zg, wa, wb, wo, n2, tm):
    t, d = x2d.shape
    row = lambda w: pl.BlockSpec((tm, w), lambda i: (i, 0))
    full = lambda a: pl.BlockSpec(a.shape, lambda i: (0, 0))
    return pl.pallas_call(
        _block_kernel,
        grid=(t // tm,),
        in_specs=[row(d), row(oa.shape[1]), row(ob.shape[1]),
                  pl.BlockSpec((tm, d), lambda i: (i, 0)), pl.BlockSpec((tm, d), lambda i: (i, 1)),
                  full(wa), full(wb), full(wo), full(n2)],
        out_specs=[row(d), row(d)],
        out_shape=[jax.ShapeDtypeStruct((t, d), F32), jax.ShapeDtypeStruct((t, d), BF16)],
        compiler_params=_params(1),
        name="block",
    )(x2d, oa, ob, zg, zg, wa, wb, wo, n2)


RANK_NONE = float(PEER_NKEYS)
CAND_ROWS = tuple(PEER_TOPK // (a + 1) for a in range(PEER_TOPK))
CAND_PAD = tuple(-(-r // 8) * 8 for r in CAND_ROWS)


def _extract(work, n_take, break_ties):
    rows = lax.broadcasted_iota(jnp.int32, work.shape, 0).astype(F32)
    rank = jnp.full(work.shape, RANK_NONE, F32)
    vals = []
    for kk in range(n_take):
        mx = jnp.max(work, axis=0, keepdims=True)
        sel = work == mx
        if break_ties:
            first = jnp.min(jnp.where(sel, rows, float(work.shape[0])), axis=0, keepdims=True)
            sel = rows == first
        rank = jnp.where(sel, float(kk), rank)
        work = jnp.where(sel, -jnp.inf, work)
        vals.append(mx)
    return rank, jnp.concatenate(vals, axis=0)


def _extract_top(work, n_take):
    rank, vals = _extract(work, n_take, break_ties=False)
    ranked = jnp.sum(jnp.where(rank < RANK_NONE, 1.0, 0.0), axis=0, keepdims=True)
    return lax.cond(jnp.max(ranked) > n_take,
                    lambda: _extract(work, n_take, break_ties=True), lambda: (rank, vals))


def _route_kernel(hn_ref, wq_ref, sk_ref, rank1_ref, p1_ref, m_ref, p0_ref, s_scr):
    q = jnp.dot(hn_ref[...], wq_ref[...], preferred_element_type=F32).astype(BF16)
    s_scr[...] = lax.dot_general(sk_ref[...], q, NT_DIMS, preferred_element_type=F32)

    def head(hh, carry):
        base = pl.multiple_of(hh * 2 * PEER_NKEYS, 2 * PEER_NKEYS)
        s0 = s_scr[pl.ds(base, PEER_NKEYS), :]
        s1 = s_scr[pl.ds(base + PEER_NKEYS, PEER_NKEYS), :]
        rank0, vals0 = _extract_top(s0, PEER_TOPK)
        rank1, vals1 = _extract_top(s1, PEER_TOPK)
        cand = []
        for a in range(PEER_TOPK):
            blk = vals0[a:a + 1] + vals1[:CAND_PAD[a]]
            rr = lax.broadcasted_iota(jnp.int32, blk.shape, 0)
            cand.append(jnp.where(rr < CAND_ROWS[a], blk, -jnp.inf))
        taken, best = _extract_top(jnp.concatenate(cand, axis=0), PEER_TOPK)
        zsum = jnp.sum(jnp.exp(best - best[0:1]), axis=0, keepdims=True)
        m = jnp.zeros_like(s0)
        off = 0
        for a in range(PEER_TOPK):
            cnt = jnp.sum(jnp.where(taken[off:off + CAND_PAD[a]] < RANK_NONE, 1.0, 0.0), axis=0, keepdims=True)
            m = jnp.where(rank0 == float(a), cnt, m)
            off += CAND_PAD[a]
        rank1_ref[hh] = rank1
        p1_ref[hh] = jnp.exp(s1 - vals1[0:1])
        m_ref[hh] = m
        p0_ref[hh] = jnp.exp(s0 - vals0[0:1]) / zsum
        return carry

    lax.fori_loop(0, PEER_HEADS, head, 0)


def _route(hn, wq, sk, tr):
    t, d = hn.shape
    table = jax.ShapeDtypeStruct((PEER_HEADS, PEER_NKEYS, t), F32)
    tspec = pl.BlockSpec((PEER_HEADS, PEER_NKEYS, tr), lambda i: (0, 0, i))
    full = lambda a: pl.BlockSpec(a.shape, lambda i: (0, 0))
    return pl.pallas_call(
        _route_kernel,
        grid=(t // tr,),
        in_specs=[pl.BlockSpec((tr, d), lambda i: (i, 0)), full(wq), full(sk)],
        out_specs=[tspec] * 4,
        out_shape=[table] * 4,
        scratch_shapes=[pltpu.VMEM((PEER_HEADS * 2 * PEER_NKEYS, tr), F32)],
        compiler_params=_params(1),
        name="route",
    )(hn, wq, sk)


PEER_TE = 512
PEER_KEYS_PER_BLOCK = PEER_TE // PEER_NKEYS
PEER_TABLE_KEYS = 8
PEER_BLOCKS_PER_TABLE = PEER_TABLE_KEYS // PEER_KEYS_PER_BLOCK
assert PEER_BLOCKS_PER_TABLE == 2
SQRT_HALF = math.sqrt(0.5)


PEER_GATE_ROWS = 64


def _peer_gate_pieces(act_ref, w_ref, rank1_ref, p1_ref, m_ref, p0_ref, key_off):
    tm = act_ref.shape[1]

    def piece(ls, r0):
        gates = [jnp.zeros((PEER_GATE_ROWS, LANES), F32) for _ in range(PEER_KEYS_PER_BLOCK)]
        for hh in range(PEER_HEADS):
            r1 = rank1_ref[hh, r0:r0 + PEER_GATE_ROWS, ls]
            p1 = p1_ref[hh, r0:r0 + PEER_GATE_ROWS, ls]
            for j in range(PEER_KEYS_PER_BLOCK):
                kr = slice(key_off + j, key_off + j + 1)
                gates[j] = gates[j] + jnp.where(r1 < m_ref[hh, kr, ls], p1, 0.0) * p0_ref[hh, kr, ls]
        for j in range(PEER_KEYS_PER_BLOCK):
            rs = slice(j * PEER_NKEYS + r0, j * PEER_NKEYS + r0 + PEER_GATE_ROWS)
            a = act_ref[rs, ls]
            w_ref[rs, ls] = (gates[j] * (0.5 * a * (1.0 + lax.erf(a * SQRT_HALF)))).astype(BF16)

    return [functools.partial(piece, slice(c * LANES, (c + 1) * LANES), r0)
            for c in range(tm // LANES) for r0 in range(0, PEER_NKEYS, PEER_GATE_ROWS)]


PEER_DOT1_ROWS = 256
PEER_DOT2_ROWS = 512


def _interleave(*stages):
    tagged = [((i + 0.5) / len(st), si, piece) for si, st in enumerate(stages) for i, piece in enumerate(st)]
    return [piece for _, _, piece in sorted(tagged, key=lambda x: x[:2])]


def _peer_kernel(*refs, n_blocks, n_steps, with_shift):
    h_ref, hn_ref, u_ref, vt_ref, rank1_ref, p1_ref, m_ref, p0_ref = refs[:8]
    g = pl.program_id(0)
    if with_shift:
        c_ref, nw_ref, o_ref, oc_ref, act0, act1, w0, w1, acc, buf, sem_body, sem_tail, sem_out = refs[8:]
        ring = _ShiftRing(c_ref, nw_ref, oc_ref, buf, sem_body, sem_tail, sem_out)
        assert ring.n_chunks + ring.n_slots - 1 <= n_steps, "not enough grid steps to finish the shift"
        ring.step(g)
    else:
        o_ref, act0, act1, w0, w1, acc = refs[8:]

    @pl.when(g == 0)
    def _():
        for ref in (act0, act1, w0, w1, acc):
            ref[...] = jnp.zeros(ref.shape, ref.dtype)

    first = (g < 2) | (lax.rem(g - 2, n_blocks) == 0)
    last = (g >= 2) & (lax.rem(g - 2, n_blocks) == n_blocks - 1)

    def body(act_new, act_cur, w_new, w_old, key_off):
        def dot1(rows):
            act_new[rows, :] = lax.dot_general(u_ref[rows, :], hn_ref[...], NT_DIMS, preferred_element_type=F32)

        def dot2(rows):
            y = jnp.dot(vt_ref[rows, :], w_old[...], preferred_element_type=F32)
            acc[rows, :] = jnp.where(first, 0.0, acc[rows, :]) + y

        chunks = lambda n, size: [slice(r, r + size) for r in range(0, n, size)]
        for piece in _interleave(
                [functools.partial(dot1, rows) for rows in chunks(u_ref.shape[0], PEER_DOT1_ROWS)],
                _peer_gate_pieces(act_cur, w_new, rank1_ref, p1_ref, m_ref, p0_ref, key_off),
                [functools.partial(dot2, rows) for rows in chunks(vt_ref.shape[0], PEER_DOT2_ROWS)]):
            piece()

        @pl.when(last)
        def _():
            o_ref[...] = h_ref[...] + acc[...].T

    even = lax.rem(g, 2) == 0
    pl.when(even)(lambda: body(act0, act1, w1, w0, PEER_KEYS_PER_BLOCK))
    pl.when(jnp.logical_not(even))(lambda: body(act1, act0, w0, w1, 0))


PEER_SHIFT_SLOTS = 2


def _peer(h, hn, u, vt, tables, tm, shift=None):
    t, d = h.shape
    nb = u.shape[0] // PEER_TE
    assert nb % PEER_BLOCKS_PER_TABLE == 0
    n = (t // tm) * nb
    n_steps = n + 2
    blk1 = lambda g: jnp.minimum(g, n - 1)
    blk2 = lambda g: jnp.clip(g - 1, 0, n - 1)
    blk3 = lambda g: jnp.clip(g - 2, 0, n - 1)
    tspec = pl.BlockSpec((PEER_HEADS, PEER_NKEYS, tm), lambda g: (0, 0, blk2(g) // nb))
    kspec = pl.BlockSpec((PEER_HEADS, PEER_TABLE_KEYS, tm),
                         lambda g: (0, (blk2(g) % nb) // PEER_BLOCKS_PER_TABLE, blk2(g) // nb))
    in_specs = [pl.BlockSpec((tm, d), lambda g: (blk3(g) // nb, 0)),
                pl.BlockSpec((tm, d), lambda g: (blk1(g) // nb, 0)),
                pl.BlockSpec((PEER_TE, d), lambda g: (blk1(g) % nb, 0)),
                pl.BlockSpec((d, PEER_TE), lambda g: (0, blk3(g) % nb)), tspec, tspec, kspec, kspec]
    out_specs = [pl.BlockSpec((tm, d), lambda g: (blk3(g) // nb, 0))]
    out_shape = [jax.ShapeDtypeStruct((t, d), F32)]
    scratch = [pltpu.VMEM((PEER_TE, tm), F32), pltpu.VMEM((PEER_TE, tm), F32),
               pltpu.VMEM((PEER_TE, tm), BF16), pltpu.VMEM((PEER_TE, tm), BF16), pltpu.VMEM((d, tm), F32)]
    args = [h, hn, u, vt, *tables]
    if shift is not None:
        c, nw = shift
        parts = (n_steps - PEER_SHIFT_SLOTS + 1) // c.shape[0]
        assert parts >= 1, "more sequences than grid steps"
        while c.shape[1] % parts:
            parts -= 1
        any_spec = pl.BlockSpec(memory_space=pl.ANY)
        in_specs += [any_spec, any_spec]
        out_specs.append(any_spec)
        out_shape.append(jax.ShapeDtypeStruct(c.shape, c.dtype))
        dma_sems = pltpu.SemaphoreType.DMA((PEER_SHIFT_SLOTS,))
        scratch += [pltpu.VMEM((PEER_SHIFT_SLOTS, 1, c.shape[1] // parts) + tuple(c.shape[2:]), c.dtype),
                    dma_sems, dma_sems, dma_sems]
        args += [c, nw]
    out = pl.pallas_call(
        functools.partial(_peer_kernel, n_blocks=nb, n_steps=n_steps, with_shift=shift is not None),
        grid=(n_steps,),
        in_specs=in_specs,
        out_specs=out_specs,
        out_shape=out_shape,
        scratch_shapes=scratch,
        compiler_params=_params(1),
        name="peer",
    )(*args)
    return out if shift is not None else out[0]


def _alibi_slopes():
    return 2.0 ** (-8.0 * jnp.arange(1, N_ALIBI_HEADS + 1, dtype=F32) / N_ALIBI_HEADS)


def _subkey_matrix(subkeys):
    two, nk, dh = subkeys.shape
    eye = jnp.eye(PEER_HEADS * two, dtype=subkeys.dtype).reshape(PEER_HEADS, two, PEER_HEADS, two)
    sk = jnp.einsum("hcgb,cnd->hcngbd", eye, subkeys)
    return sk.reshape(PEER_HEADS * two * nk, PEER_HEADS * two * dh)


def _tail(x2d, oa, ob, zg, w, tm_block, tr, tm_peer, shift=None):
    h, hn = _block(x2d, oa, ob, zg, w["wa"], w["wb"], w["wo"], w["n2"], tm_block)
    tables = _route(hn, w["wq"], w["sk"], tr)
    return _peer(h, hn, w["u"], w["vt"], tables, tm_peer, shift)


def kernel(x_prompt, x_sample, cache_a_kv, cache_b1_kv, cache_b2_kv, cache_b3_kv, norm1_w, w_in, q_norm_a,
           k_norm_a, sink_a, q_norm_b, k_norm_b, w_branch_a, w_branch_b, w_out, norm2_w, peer_wq,
           peer_subkeys, peer_u, peer_v):
    assert norm1_w.shape[0] == 1, "single layer"
    b, s, d = x_prompt.shape
    n_dec = x_sample.shape[0]
    assert x_sample.shape[1] == 1

    slopes = _alibi_slopes()
    slopes_a, slopes_b = slopes[:A_Q_HEADS], slopes[A_Q_HEADS:]
    sink = sink_a[0].astype(F32)
    ones = jnp.ones((HEAD_DIM,), F32)
    col_w = jnp.concatenate(
        [jnp.tile(q_norm_a[0], A_Q_HEADS), jnp.tile(k_norm_a[0], A_KV_HEADS), jnp.tile(ones, A_KV_HEADS),
         jnp.tile(q_norm_b[0], B_HEADS), jnp.tile(k_norm_b[0], B_HEADS), jnp.tile(ones, B_HEADS)]
    ).astype(F32)[None]
    gate_w = jnp.ones((1, GATE_W), F32)
    n1 = norm1_w[0].astype(F32)[None]
    w_qkv = w_in[0, :, :QKV_W].astype(BF16)
    w_gate = w_in[0, :, QKV_W:].astype(BF16)
    w = dict(wa=w_branch_a[0].astype(BF16), wb=w_branch_b[0].astype(BF16), wo=w_out[0].astype(BF16),
             n2=norm2_w[0].astype(F32)[None], wq=peer_wq[0].astype(BF16),
             sk=_subkey_matrix(peer_subkeys[0]).astype(BF16),
             u=peer_u[0].astype(BF16), vt=peer_v[0].astype(BF16).T)

    xs = x_sample.reshape(n_dec, d)
    zs = _proj(xs, n1, w_qkv, col_w, QKV_KINDS, F32, n_dec, "proj_qkv_s")
    zgs = _proj(xs, n1, w_gate, gate_w, GATE_KINDS, BF16, n_dec, "proj_gate_s")
    zs3 = zs.reshape(n_dec, N_QKV_HEADS, HEAD_DIM)
    caches = (cache_a_kv[0], cache_b1_kv[0], cache_b2_kv[0], cache_b3_kv[0])
    news = [jnp.stack([zs3[:, HEAD_KA:HEAD_KA + A_KV_HEADS], zs3[:, HEAD_VA:HEAD_VA + A_KV_HEADS]], axis=1)]
    for g in range(N_B_GROUPS):
        o4 = g * B_HEADS_PER_GROUP
        news.append(jnp.stack([zs3[:, HEAD_KB + o4:HEAD_KB + o4 + B_HEADS_PER_GROUP],
                               zs3[:, HEAD_VB + o4:HEAD_VB + o4 + B_HEADS_PER_GROUP]], axis=1))

    xp = x_prompt.reshape(b * s, d)
    zq = _proj(xp, n1, w_qkv, col_w, QKV_KINDS, F32, 1024, "proj_qkv")
    zg = _proj(xp, n1, w_gate, gate_w, GATE_KINDS, BF16, 1024, "proj_gate")
    zq3 = zq.reshape(b, s, QKV_W)
    oa = _attn_a(zq3, slopes_a, sink).reshape(b * s, A_Q_HEADS * HEAD_DIM)
    ob = _attn_b(zq3, slopes_b).reshape(b * s, B_HEADS_PER_GROUP * HEAD_DIM)
    y_prompt, new_b3 = _tail(xp, oa, ob, zg, w, 256, 256, 512, shift=(caches[3], news[3]))
    y_prompt = y_prompt.reshape(b, s, d)

    def window(k0, v0, nh, length):
        part = lambda h0: zq3[:, s - length:, h0 * HEAD_DIM:(h0 + nh) * HEAD_DIM].reshape(b, length, nh, HEAD_DIM)
        return jnp.stack([part(k0), part(v0)], axis=2)[None]

    kv_prompt = [window(HEAD_KA, HEAD_VA, A_KV_HEADS, min(A_WINDOW, s))]
    for g, (win, _) in enumerate(B_GROUPS):
        o4 = g * B_HEADS_PER_GROUP
        kv_prompt.append(window(HEAD_KB + o4, HEAD_VB + o4, B_HEADS_PER_GROUP, min(win, s)))

    bcast = lambda a: jnp.broadcast_to(a[..., None], a.shape + (HEAD_DIM,))
    slope_a4 = bcast(slopes_a.reshape(A_KV_HEADS, A_GQA).T)[:, None]
    sink4 = bcast(sink.reshape(A_KV_HEADS, A_GQA).T)[:, None]
    slope_b4 = bcast(slopes_b.reshape(N_B_GROUPS, B_HEADS_PER_GROUP))[:, None]
    oa_s, ob_s = _sample_attn(zs3, *caches, slope_a4, sink4, slope_b4)
    oa_s = oa_s.transpose(0, 2, 1, 3).reshape(n_dec, A_Q_HEADS * HEAD_DIM).astype(BF16)
    ob_s = ob_s.reshape(n_dec, B_HEADS_PER_GROUP * HEAD_DIM).astype(BF16)
    y_sample = _tail(xs, oa_s, ob_s, zgs, w, n_dec, n_dec, n_dec).reshape(n_dec, 1, d)

    kv_sample = [o[None] for o in (*_cache_update(caches[:3], news[:3]), new_b3)]

    return (y_prompt, y_sample, *kv_prompt, *kv_sample)
```

```python
import functools
import math

import jax
import jax.numpy as jnp
from jax import lax
from jax.experimental import pallas as pl
from jax.experimental.pallas import tpu as pltpu

F32 = jnp.float32
BF16 = jnp.bfloat16

D_MODEL = 2048
HEAD_DIM = 128
A_Q_HEADS = 8
A_KV_HEADS = 2
A_GQA = A_Q_HEADS // A_KV_HEADS
A_WINDOW = 128
B_GROUPS = ((128, 1), (512, 4), (2048, 16))
B_HEADS_PER_GROUP = 4
N_B_GROUPS = len(B_GROUPS)
B_HEADS = N_B_GROUPS * B_HEADS_PER_GROUP
BAND = 128
N_ALIBI_HEADS = A_Q_HEADS + B_HEADS
ATTN_SCALE = HEAD_DIM ** -0.5
PEER_HEADS = 8
PEER_NKEYS = 128
PEER_EXPERTS = PEER_NKEYS * PEER_NKEYS
PEER_DKEY = 128
PEER_TOPK = 16
NORM_EPS = 1e-6
NEG_INF = -1e30

QKV_W = (A_Q_HEADS + 2 * A_KV_HEADS + 3 * B_HEADS) * HEAD_DIM
GATE_W = 2 * D_MODEL
HEAD_QA, HEAD_KA, HEAD_VA = 0, A_Q_HEADS, A_Q_HEADS + A_KV_HEADS
HEAD_QB = A_Q_HEADS + 2 * A_KV_HEADS
HEAD_KB = HEAD_QB + B_HEADS
HEAD_VB = HEAD_KB + B_HEADS
N_QKV_HEADS = QKV_W // HEAD_DIM

VMEM_LIMIT_BYTES = 56 * 1024 * 1024
LANES = 128

NT_DIMS = (((1,), (1,)), ((), ()))
TN_DIMS = (((0,), (0,)), ((), ()))


def _params(n_grid_axes):
    return pltpu.CompilerParams(
        dimension_semantics=("arbitrary",) * n_grid_axes,
        vmem_limit_bytes=VMEM_LIMIT_BYTES)


def _rms(x):
    return x * lax.rsqrt(jnp.mean(x * x, axis=-1, keepdims=True) + NORM_EPS)


PROJ_TN = 512
PROJ_HEADS_PER_TILE = PROJ_TN // HEAD_DIM


def _proj_kernel(x_ref, n1_ref, w_ref, cw_ref, o_ref, xn_ref, *, kinds):
    j = pl.program_id(1)

    @pl.when(j == 0)
    def _():
        xn_ref[...] = (_rms(x_ref[...]) * n1_ref[...]).astype(BF16)

    z = jnp.dot(xn_ref[...], w_ref[...], preferred_element_type=F32)

    def epilogue(head_is_normed):
        for hh, normed in enumerate(head_is_normed):
            cs = slice(hh * HEAD_DIM, (hh + 1) * HEAD_DIM)
            zh = z[:, cs]
            if normed:
                zh = _rms(zh) * cw_ref[:, cs]
            o_ref[:, cs] = zh.astype(o_ref.dtype)

    for kind in sorted(set(kinds)):
        cond = functools.reduce(jnp.logical_or, [j == jj for jj, k in enumerate(kinds) if k == kind])
        if kind == "gate":
            @pl.when(cond)
            def _():
                o_ref[...] = jax.nn.sigmoid(z).astype(o_ref.dtype)
        else:
            @pl.when(cond)
            def _(kind=kind):
                epilogue(kind)


def _proj(x2d, n1, w, cw, kinds, out_dtype, tm, name):
    t, d = x2d.shape
    n = w.shape[1]
    assert t % tm == 0 and n == PROJ_TN * len(kinds)
    return pl.pallas_call(
        functools.partial(_proj_kernel, kinds=kinds),
        grid=(t // tm, len(kinds)),
        in_specs=[
            pl.BlockSpec((tm, d), lambda i, j: (i, 0)),
            pl.BlockSpec((1, d), lambda i, j: (0, 0)),
            pl.BlockSpec((d, PROJ_TN), lambda i, j: (0, j)),
            pl.BlockSpec((1, PROJ_TN), lambda i, j: (0, j)),
        ],
        out_specs=pl.BlockSpec((tm, PROJ_TN), lambda i, j: (i, j)),
        out_shape=jax.ShapeDtypeStruct((t, n), out_dtype),
        scratch_shapes=[pltpu.VMEM((tm, d), BF16)],
        compiler_params=_params(2),
        name=name,
    )(x2d, n1, w, cw)


_N4, _I4 = (True,) * 4, (False,) * 4
QKV_KINDS = (_N4, _N4, (True, True, False, False)) + (_N4,) * 6 + (_I4,) * 3
GATE_KINDS = ("gate",) * (GATE_W // PROJ_TN)


def _band_softmax(q, k, v, slope_dist, mask, sink):
    s = lax.dot_general(q, k, NT_DIMS, preferred_element_type=F32) * ATTN_SCALE
    logits = jnp.where(mask, s - slope_dist, NEG_INF)
    m = jnp.max(logits, axis=-1, keepdims=True)
    if sink is not None:
        m = jnp.maximum(m, sink)
    p = jnp.exp(logits - m)
    denom = jnp.sum(p, axis=-1, keepdims=True)
    if sink is not None:
        denom = denom + jnp.exp(sink - m)
    o = jnp.dot(p.astype(BF16), v, preferred_element_type=F32) / denom
    return o, m + jnp.log(denom)


def _band_geometry(has_prev):
    nk = 2 * BAND if has_prev else BAND
    qi = lax.broadcasted_iota(jnp.int32, (BAND, nk), 0)
    sj = lax.broadcasted_iota(jnp.int32, (BAND, nk), 1)
    dist = qi - sj + (BAND if has_prev else 0)
    return dist, sj


def _attn_a_kernel(slopes_ref, sink_ref, q_ref, k_ref, v_ref, o_ref):
    kvh = pl.program_id(1)
    n_blocks = q_ref.shape[1] // BAND
    dist, sj = _band_geometry(True)
    in_window = (dist >= 0) & (dist <= A_WINDOW)
    distf = dist.astype(F32)

    def block(blk, carry):
        cur = pl.multiple_of(blk * BAND, BAND)
        prev = pl.multiple_of(jnp.maximum(blk - 1, 0) * BAND, BAND)
        k = jnp.concatenate([k_ref[0, pl.ds(prev, BAND), :], k_ref[0, pl.ds(cur, BAND), :]], axis=0).astype(BF16)
        v = jnp.concatenate([v_ref[0, pl.ds(prev, BAND), :], v_ref[0, pl.ds(cur, BAND), :]], axis=0).astype(BF16)
        mask = in_window & (sj >= jnp.where(blk > 0, 0, BAND))
        for g in range(A_GQA):
            cs = slice(g * HEAD_DIM, (g + 1) * HEAD_DIM)
            head = kvh * A_GQA + g
            q = q_ref[0, pl.ds(cur, BAND), cs].astype(BF16)
            o, _ = _band_softmax(q, k, v, slopes_ref[head] * distf, mask, sink_ref[head])
            o_ref[0, pl.ds(cur, BAND), cs] = o.astype(o_ref.dtype)
        return carry

    lax.fori_loop(0, n_blocks, block, 0)


def _attn_a(zq3, slopes_a, sink):
    b, s, _ = zq3.shape
    gw = A_GQA * HEAD_DIM
    smem = pl.BlockSpec(memory_space=pltpu.SMEM)
    return pl.pallas_call(
        _attn_a_kernel,
        grid=(b, A_KV_HEADS),
        in_specs=[
            smem, smem,
            pl.BlockSpec((1, s, gw), lambda bi, h: (bi, 0, h)),
            pl.BlockSpec((1, s, HEAD_DIM), lambda bi, h: (bi, 0, HEAD_KA + h)),
            pl.BlockSpec((1, s, HEAD_DIM), lambda bi, h: (bi, 0, HEAD_VA + h)),
        ],
        out_specs=pl.BlockSpec((1, s, gw), lambda bi, h: (bi, 0, h)),
        out_shape=jax.ShapeDtypeStruct((b, s, A_Q_HEADS * HEAD_DIM), BF16),
        compiler_params=_params(2),
        name="attn_a",
    )(slopes_a, sink, zq3, zq3, zq3)


def _attn_b_kernel(slopes_ref, *refs):
    q_refs, k_refs, v_refs = refs[0:3], refs[3:6], refs[6:9]
    o_ref, og_ref, lse_ref = refs[9:12]
    h = pl.program_id(1)
    s_len = o_ref.shape[1]

    for g, (win, dil) in enumerate(B_GROUPS):
        n_blocks = s_len // (BAND * dil)
        slope = slopes_ref[g * B_HEADS_PER_GROUP + h]
        for has_prev in (False, True):
            dist, _ = _band_geometry(has_prev)
            mask = (dist >= 0) & (dist <= win // dil)
            slope_dist = slope * (dil * dist).astype(F32)
            for blk in range(n_blocks):
                if (blk > 0) != has_prev:
                    continue
                for r in range(dil):
                    def rows(ref, b0):
                        start = BAND * dil * b0 + r
                        if dil == 1:
                            return ref[0, pl.ds(start, BAND), :]
                        return ref[0, pl.ds(start, BAND, stride=dil), :]

                    q = rows(q_refs[g], blk).astype(BF16)
                    k = rows(k_refs[g], blk)
                    v = rows(v_refs[g], blk)
                    if has_prev:
                        k = jnp.concatenate([rows(k_refs[g], blk - 1), k], axis=0)
                        v = jnp.concatenate([rows(v_refs[g], blk - 1), v], axis=0)
                    o, lse = _band_softmax(q, k.astype(BF16), v.astype(BF16), slope_dist, mask, None)
                    start = BAND * dil * blk + r
                    idx = pl.ds(start, BAND) if dil == 1 else pl.ds(start, BAND, stride=dil)
                    og_ref[g, idx, :] = o
                    lse_ref[g, idx, :] = jnp.broadcast_to(lse, (BAND, HEAD_DIM))

    def combine(c, carry):
        rs = pl.ds(pl.multiple_of(c * BAND, BAND), BAND)
        lses = [lse_ref[g, rs, :] for g in range(N_B_GROUPS)]
        mx = functools.reduce(jnp.maximum, lses)
        ws = [jnp.exp(l - mx) for l in lses]
        acc = functools.reduce(lambda a, b_: a + b_, [w * og_ref[g, rs, :] for g, w in enumerate(ws)])
        o_ref[0, rs, :] = (acc / functools.reduce(lambda a, b_: a + b_, ws)).astype(o_ref.dtype)
        return carry

    lax.fori_loop(0, s_len // BAND, combine, 0)


def _attn_b(zq3, slopes_b):
    b, s, _ = zq3.shape
    smem = pl.BlockSpec(memory_space=pltpu.SMEM)

    def head_spec(base):
        return [pl.BlockSpec((1, s, HEAD_DIM), lambda bi, h, c=base + g * B_HEADS_PER_GROUP: (bi, 0, c + h))
                for g in range(N_B_GROUPS)]

    return pl.pallas_call(
        _attn_b_kernel,
        grid=(b, B_HEADS_PER_GROUP),
        in_specs=[smem] + head_spec(HEAD_QB) + head_spec(HEAD_KB) + head_spec(HEAD_VB),
        out_specs=pl.BlockSpec((1, s, HEAD_DIM), lambda bi, h: (bi, 0, h)),
        out_shape=jax.ShapeDtypeStruct((b, s, B_HEADS_PER_GROUP * HEAD_DIM), BF16),
        scratch_shapes=[pltpu.VMEM((N_B_GROUPS, s, HEAD_DIM), F32),
                        pltpu.VMEM((N_B_GROUPS, s, HEAD_DIM), F32)],
        compiler_params=_params(2),
        name="attn_b",
    )(slopes_b, *([zq3] * 9))


SAMPLE_BS = 8


def _window_read(q, k_new, v_new, c_ref, slope, dil, sink):
    n = c_ref.shape[1]
    k = c_ref[:, :, 0]
    v = c_ref[:, :, 1]
    s = jnp.sum(k * q[:, None], axis=-1, keepdims=True) * ATTN_SCALE
    steps = (n - lax.broadcasted_iota(jnp.int32, s.shape, 1)).astype(F32)
    logits = s - slope[:, None] * (steps * float(dil))
    s_new = jnp.sum(k_new * q, axis=-1, keepdims=True) * ATTN_SCALE
    m = jnp.maximum(jnp.max(logits, axis=1), s_new)
    if sink is not None:
        m = jnp.maximum(m, sink)
    p = jnp.exp(logits - m[:, None])
    p_new = jnp.exp(s_new - m)
    denom = jnp.sum(p, axis=1) + p_new
    if sink is not None:
        denom = denom + jnp.exp(sink - m)
    o = (jnp.sum(p * v, axis=1) + p_new * v_new) / denom
    return o, m + jnp.log(denom)


def _sample_attn_kernel(z_ref, ca_ref, cb1_ref, cb2_ref, cb3_ref, slope_a_ref, sink_ref, slope_b_ref,
                        oa_ref, ob_ref):
    z = z_ref[...]
    k_new = z[:, HEAD_KA:HEAD_KA + A_KV_HEADS]
    v_new = z[:, HEAD_VA:HEAD_VA + A_KV_HEADS]
    for g in range(A_GQA):
        q = jnp.concatenate([z[:, kv * A_GQA + g:kv * A_GQA + g + 1] for kv in range(A_KV_HEADS)], axis=1)
        o, _ = _window_read(q, k_new, v_new, ca_ref, slope_a_ref[g], 1, sink_ref[g])
        oa_ref[:, g] = o
    outs, lses = [], []
    for g, (c_ref, (_, dil)) in enumerate(zip((cb1_ref, cb2_ref, cb3_ref), B_GROUPS)):
        hs = lambda base: slice(base + g * B_HEADS_PER_GROUP, base + (g + 1) * B_HEADS_PER_GROUP)
        o, lse = _window_read(z[:, hs(HEAD_QB)], z[:, hs(HEAD_KB)], z[:, hs(HEAD_VB)], c_ref,
                              slope_b_ref[g], dil, None)
        outs.append(o)
        lses.append(lse)
    mx = functools.reduce(jnp.maximum, lses)
    ws = [jnp.exp(l - mx) for l in lses]
    acc = functools.reduce(lambda a, b_: a + b_, [w * o for w, o in zip(ws, outs)])
    ob_ref[...] = acc / functools.reduce(lambda a, b_: a + b_, ws)


def _sample_attn(zs3, cache_a, cache_b1, cache_b2, cache_b3, slope_a, sink, slope_b):
    n = zs3.shape[0]
    bs = SAMPLE_BS

    def strided_rows(c, dil):
        nb, l, two, hh, hd = c.shape
        assert l == A_WINDOW * dil
        c6 = c.reshape(nb, l // dil, dil, two, hh, hd)
        spec = pl.BlockSpec((bs, l // dil, None, two, hh, hd), lambda i: (i, 0, 0, 0, 0, 0))
        return c6, spec

    ca, ca_spec = strided_rows(cache_a, 1)
    cb = [strided_rows(c, dil) for c, (_, dil) in zip((cache_b1, cache_b2, cache_b3), B_GROUPS)]
    full = lambda a: pl.BlockSpec(a.shape, lambda i: (0,) * a.ndim)
    return pl.pallas_call(
        _sample_attn_kernel,
        grid=(n // bs,),
        in_specs=[pl.BlockSpec((bs, N_QKV_HEADS, HEAD_DIM), lambda i: (i, 0, 0)),
                  ca_spec, cb[0][1], cb[1][1], cb[2][1],
                  full(slope_a), full(sink), full(slope_b)],
        out_specs=[pl.BlockSpec((bs, A_GQA, A_KV_HEADS, HEAD_DIM), lambda i: (i, 0, 0, 0)),
                   pl.BlockSpec((bs, B_HEADS_PER_GROUP, HEAD_DIM), lambda i: (i, 0, 0))],
        out_shape=[jax.ShapeDtypeStruct((n, A_GQA, A_KV_HEADS, HEAD_DIM), F32),
                   jax.ShapeDtypeStruct((n, B_HEADS_PER_GROUP, HEAD_DIM), F32)],
        compiler_params=_params(1),
        name="sample_attn",
    )(zs3, ca, cb[0][0], cb[1][0], cb[2][0], slope_a, sink, slope_b)


CACHE_SLOTS = 3
CACHE_CHUNK_BYTES = 4 * 1024 * 1024


class _ShiftRing:
    def __init__(self, c, nw, o, buf, sem_body, sem_tail, sem_out):
        self.c, self.nw, self.o, self.buf = c, nw, o, buf
        self.sem_body, self.sem_tail, self.sem_out = sem_body, sem_tail, sem_out
        self.n_slots, self.nseq, self.rows = buf.shape[:3]
        n, self.l = c.shape[0], c.shape[1]
        assert n % self.nseq == 0 and self.l % self.rows == 0
        self.parts = self.l // self.rows
        self.n_chunks = (n // self.nseq) * self.parts

    def _where(self, k):
        return k % self.n_slots, pl.ds((k // self.parts) * self.nseq, self.nseq), (k % self.parts) * self.rows

    def _body(self, k):
        slot, seqs, r0 = self._where(k)
        return pltpu.make_async_copy(self.c.at[seqs, pl.ds(r0 + 1, self.rows - 1)],
                                     self.buf.at[slot, :, pl.ds(0, self.rows - 1)], self.sem_body.at[slot])

    def _tail_old(self, k):
        slot, seqs, r0 = self._where(k)
        return pltpu.make_async_copy(self.c.at[seqs, pl.ds(jnp.minimum(r0 + self.rows, self.l - 1), 1)],
                                     self.buf.at[slot, :, pl.ds(self.rows - 1, 1)], self.sem_tail.at[slot])

    def _tail_new(self, k):
        slot, seqs, _ = self._where(k)
        return pltpu.make_async_copy(self.nw.at[seqs], self.buf.at[slot, :, self.rows - 1], self.sem_tail.at[slot])

    def write(self, k):
        slot, seqs, r0 = self._where(k)
        return pltpu.make_async_copy(self.buf.at[slot], self.o.at[seqs, pl.ds(r0, self.rows)], self.sem_out.at[slot])

    def _read(self, k, action):
        action(self._body(k))
        if self.parts == 1:
            action(self._tail_new(k))
        else:
            is_last = (k % self.parts) == self.parts - 1
            pl.when(is_last)(lambda: action(self._tail_new(k)))
            pl.when(jnp.logical_not(is_last))(lambda: action(self._tail_old(k)))

    def start_read(self, k):
        self._read(k, lambda cp: cp.start())

    def wait_read(self, k):
        self._read(k, lambda cp: cp.wait())

    def step(self, k):
        n = self.n_chunks
        written = k + 1 - self.n_slots
        pl.when(k == 0)(lambda: self.start_read(0))
        pl.when((written >= 0) & (written < n))(lambda: self.write(jnp.clip(written, 0, n - 1)).wait())
        pl.when(k + 1 < n)(lambda: self.start_read(jnp.minimum(k + 1, n - 1)))

        @pl.when(k < n)
        def _():
            self.wait_read(jnp.minimum(k, n - 1))
            self.write(jnp.minimum(k, n - 1)).start()

    def drain(self, done_steps):
        for k in range(max(done_steps - self.n_slots + 1, 0), self.n_chunks):
            self.write(k).wait()


def _cache_ring(c, nw, o, buf, sem_body, sem_tail, sem_out):
    ring = _ShiftRing(c, nw, o, buf, sem_body, sem_tail, sem_out)

    def step(k, carry):
        ring.step(k)
        return carry

    lax.fori_loop(0, ring.n_chunks, step, 0)
    ring.drain(ring.n_chunks)


def _cache_chunk(c):
    n, l = c.shape[0], c.shape[1]
    seq_bytes = math.prod(c.shape[1:]) * c.dtype.itemsize
    if seq_bytes <= CACHE_CHUNK_BYTES:
        return min(n, CACHE_CHUNK_BYTES // seq_bytes), l
    return 1, l // (seq_bytes // CACHE_CHUNK_BYTES)


def _cache_kernel(*refs):
    n = len(refs) // 3
    caches, news, outs = refs[:n], refs[n:2 * n], refs[2 * n:]
    for c, nw, o in zip(caches, news, outs):
        nseq, rows = _cache_chunk(c)
        dma_sems = pltpu.SemaphoreType.DMA((CACHE_SLOTS,))
        pl.run_scoped(
            functools.partial(_cache_ring, c, nw, o),
            pltpu.VMEM((CACHE_SLOTS, nseq, rows) + tuple(c.shape[2:]), c.dtype), dma_sems, dma_sems, dma_sems)


def _cache_update(caches, news):
    n = len(caches)
    any_spec = pl.BlockSpec(memory_space=pl.ANY)
    return pl.pallas_call(
        _cache_kernel,
        in_specs=[any_spec] * (2 * n),
        out_specs=[any_spec] * n,
        out_shape=[jax.ShapeDtypeStruct(c.shape, c.dtype) for c in caches],
        compiler_params=pltpu.CompilerParams(vmem_limit_bytes=VMEM_LIMIT_BYTES),
        name="cache_update",
    )(*caches, *news)


def _block_kernel(x_ref, oa_ref, ob_ref, ga_ref, gb_ref, wa_ref, wb_ref, wo_ref, n2_ref, h_ref, hn_ref):
    ya = jnp.dot(oa_ref[...], wa_ref[...], preferred_element_type=F32)
    yb = jnp.dot(ob_ref[...], wb_ref[...], preferred_element_type=F32)
    mix = (ga_ref[...].astype(F32) * ya + gb_ref[...].astype(F32) * yb).astype(BF16)
    h = x_ref[...] + jnp.dot(mix, wo_ref[...], preferred_element_type=F32)
    h_ref[...] = h
    hn_ref[...] = (_rms(h) * n2_ref[...]).astype(BF16)


def _block(x2d, oa, ob, zg, wa, wb, wo, n2, tm):
    t, d = x2d.shape
    row = lambda w: pl.BlockSpec((tm, w), lambda i: (i, 0))
    full = lambda a: pl.BlockSpec(a.shape, lambda i: (0, 0))
    return pl.pallas_call(
        _block_kernel,
        grid=(t // tm,),
        in_specs=[row(d), row(oa.shape[1]), row(ob.shape[1]),
                  pl.BlockSpec((tm, d), lambda i: (i, 0)), pl.BlockSpec((tm, d), lambda i: (i, 1)),
                  full(wa), full(wb), full(wo), full(n2)],
        out_specs=[row(d), row(d)],
        out_shape=[jax.ShapeDtypeStruct((t, d), F32), jax.ShapeDtypeStruct((t, d), BF16)],
        compiler_params=_params(1),
        name="block",
    )(x2d, oa, ob, zg, zg, wa, wb, wo, n2)


RANK_NONE = float(PEER_NKEYS)
CAND_ROWS = tuple(PEER_TOPK // (a + 1) for a in range(PEER_TOPK))
CAND_PAD = tuple(-(-r // 8) * 8 for r in CAND_ROWS)


def _extract(work, n_take, break_ties):
    rows = lax.broadcasted_iota(jnp.int32, work.shape, 0).astype(F32)
    rank = jnp.full(work.shape, RANK_NONE, F32)
    vals = []
    for kk in range(n_take):
        mx = jnp.max(work, axis=0, keepdims=True)
        sel = work == mx
        if break_ties:
            first = jnp.min(jnp.where(sel, rows, float(work.shape[0])), axis=0, keepdims=True)
            sel = rows == first
        rank = jnp.where(sel, float(kk), rank)
        work = jnp.where(sel, -jnp.inf, work)
        vals.append(mx)
    return rank, jnp.concatenate(vals, axis=0)


def _extract_top(work, n_take):
    rank, vals = _extract(work, n_take, break_ties=False)
    ranked = jnp.sum(jnp.where(rank < RANK_NONE, 1.0, 0.0), axis=0, keepdims=True)
    return lax.cond(jnp.max(ranked) > n_take,
                    lambda: _extract(work, n_take, break_ties=True), lambda: (rank, vals))


ROUTE_SHIFT_SLOTS = 2
ROUTE_SHIFT_PER_STEP = 2


def _route_kernel(*refs, n_steps, with_shift):
    hn_ref, wq_ref, sk_ref = refs[:3]
    if with_shift:
        c_ref, nw_ref, rank1_ref, p1_ref, m_ref, p0_ref, oc_ref, s_scr, buf, sem_body, sem_tail, sem_out = refs[3:]
        ring = _ShiftRing(c_ref, nw_ref, oc_ref, buf, sem_body, sem_tail, sem_out)
        assert ring.n_chunks == n_steps * ROUTE_SHIFT_PER_STEP and PEER_HEADS % ROUTE_SHIFT_PER_STEP == 0
        first_chunk = pl.program_id(0) * ROUTE_SHIFT_PER_STEP
    else:
        rank1_ref, p1_ref, m_ref, p0_ref, s_scr = refs[3:]
        ring = None
    q = jnp.dot(hn_ref[...], wq_ref[...], preferred_element_type=F32).astype(BF16)
    s_scr[...] = lax.dot_general(sk_ref[...], q, NT_DIMS, preferred_element_type=F32)

    def head(hh, carry):
        if ring is not None:
            every = PEER_HEADS // ROUTE_SHIFT_PER_STEP
            pl.when(lax.rem(hh, every) == 0)(lambda: ring.step(first_chunk + hh // every))
        base = pl.multiple_of(hh * 2 * PEER_NKEYS, 2 * PEER_NKEYS)
        s0 = s_scr[pl.ds(base, PEER_NKEYS), :]
        s1 = s_scr[pl.ds(base + PEER_NKEYS, PEER_NKEYS), :]
        rank0, vals0 = _extract_top(s0, PEER_TOPK)
        rank1, vals1 = _extract_top(s1, PEER_TOPK)
        cand = []
        for a in range(PEER_TOPK):
            blk = vals0[a:a + 1] + vals1[:CAND_PAD[a]]
            rr = lax.broadcasted_iota(jnp.int32, blk.shape, 0)
            cand.append(jnp.where(rr < CAND_ROWS[a], blk, -jnp.inf))
        taken, best = _extract_top(jnp.concatenate(cand, axis=0), PEER_TOPK)
        zsum = jnp.sum(jnp.exp(best - best[0:1]), axis=0, keepdims=True)
        m = jnp.zeros_like(s0)
        off = 0
        for a in range(PEER_TOPK):
            cnt = jnp.sum(jnp.where(taken[off:off + CAND_PAD[a]] < RANK_NONE, 1.0, 0.0), axis=0, keepdims=True)
            m = jnp.where(rank0 == float(a), cnt, m)
            off += CAND_PAD[a]
        rank1_ref[hh] = rank1
        p1_ref[hh] = jnp.exp(s1 - vals1[0:1])
        m_ref[hh] = m
        p0_ref[hh] = jnp.exp(s0 - vals0[0:1]) / zsum
        return carry

    lax.fori_loop(0, PEER_HEADS, head, 0)
    if ring is not None:
        pl.when(pl.program_id(0) == n_steps - 1)(lambda: ring.drain(ring.n_chunks))


def _route(hn, wq, sk, tr, shift=None):
    t, d = hn.shape
    n_steps = t // tr
    table = jax.ShapeDtypeStruct((PEER_HEADS, PEER_NKEYS, t), F32)
    tspec = pl.BlockSpec((PEER_HEADS, PEER_NKEYS, tr), lambda i: (0, 0, i))
    full = lambda a: pl.BlockSpec(a.shape, lambda i: (0, 0))
    in_specs = [pl.BlockSpec((tr, d), lambda i: (i, 0)), full(wq), full(sk)]
    out_specs, out_shape = [tspec] * 4, [table] * 4
    scratch = [pltpu.VMEM((PEER_HEADS * 2 * PEER_NKEYS, tr), F32)]
    args = [hn, wq, sk]
    if shift is not None:
        c, nw = shift
        nseq = c.shape[0] // (n_steps * ROUTE_SHIFT_PER_STEP)
        assert nseq * n_steps * ROUTE_SHIFT_PER_STEP == c.shape[0]
        any_spec = pl.BlockSpec(memory_space=pl.ANY)
        in_specs += [any_spec, any_spec]
        out_specs = out_specs + [any_spec]
        out_shape = out_shape + [jax.ShapeDtypeStruct(c.shape, c.dtype)]
        dma_sems = pltpu.SemaphoreType.DMA((ROUTE_SHIFT_SLOTS,))
        scratch += [pltpu.VMEM((ROUTE_SHIFT_SLOTS, nseq) + tuple(c.shape[1:]), c.dtype), dma_sems, dma_sems, dma_sems]
        args += [c, nw]
    out = pl.pallas_call(
        functools.partial(_route_kernel, n_steps=n_steps, with_shift=shift is not None),
        grid=(n_steps,),
        in_specs=in_specs,
        out_specs=out_specs,
        out_shape=out_shape,
        scratch_shapes=scratch,
        compiler_params=_params(1),
        name="route",
    )(*args)
    return (out[:4], out[4]) if shift is not None else out


PEER_TE = 512
PEER_KEYS_PER_BLOCK = PEER_TE // PEER_NKEYS
PEER_TABLE_KEYS = 8
PEER_BLOCKS_PER_TABLE = PEER_TABLE_KEYS // PEER_KEYS_PER_BLOCK
assert PEER_BLOCKS_PER_TABLE == 2
SQRT_HALF = math.sqrt(0.5)


PEER_GATE_ROWS = 32


def _peer_gate_pieces(act_ref, w_ref, rank1_ref, p1_ref, m_ref, p0_ref, key_off):
    tm = act_ref.shape[1]

    def piece(ls, r0):
        gates = [jnp.zeros((PEER_GATE_ROWS, LANES), F32) for _ in range(PEER_KEYS_PER_BLOCK)]
        for hh in range(PEER_HEADS):
            r1 = rank1_ref[hh, r0:r0 + PEER_GATE_ROWS, ls]
            p1 = p1_ref[hh, r0:r0 + PEER_GATE_ROWS, ls]
            for j in range(PEER_KEYS_PER_BLOCK):
                kr = slice(key_off + j, key_off + j + 1)
                gates[j] = gates[j] + jnp.where(r1 < m_ref[hh, kr, ls], p1, 0.0) * p0_ref[hh, kr, ls]
        for j in range(PEER_KEYS_PER_BLOCK):
            rs = slice(j * PEER_NKEYS + r0, j * PEER_NKEYS + r0 + PEER_GATE_ROWS)
            a = act_ref[rs, ls]
            w_ref[rs, ls] = (gates[j] * (0.5 * a * (1.0 + lax.erf(a * SQRT_HALF)))).astype(BF16)

    return [functools.partial(piece, slice(c * LANES, (c + 1) * LANES), r0)
            for c in range(tm // LANES) for r0 in range(0, PEER_NKEYS, PEER_GATE_ROWS)]


PEER_DOT1_ROWS = 256
PEER_DOT2_ROWS = 512


def _interleave(*stages):
    tagged = [((i + 0.5) / len(st), si, piece) for si, st in enumerate(stages) for i, piece in enumerate(st)]
    return [piece for _, _, piece in sorted(tagged, key=lambda x: x[:2])]


def _peer_kernel(*refs, n_blocks, n_steps, with_shift):
    h_ref, hn_ref, u_ref, vt_ref, rank1_ref, p1_ref, m_ref, p0_ref = refs[:8]
    g = pl.program_id(0)
    if with_shift:
        c_ref, nw_ref, o_ref, oc_ref, act0, act1, w0, w1, acc, buf, sem_body, sem_tail, sem_out = refs[8:]
        ring = _ShiftRing(c_ref, nw_ref, oc_ref, buf, sem_body, sem_tail, sem_out)
        assert ring.n_chunks + ring.n_slots - 1 <= n_steps, "not enough grid steps to finish the shift"
        ring.step(g)
    else:
        o_ref, act0, act1, w0, w1, acc = refs[8:]

    @pl.when(g == 0)
    def _():
        for ref in (act0, act1, w0, w1, acc):
            ref[...] = jnp.zeros(ref.shape, ref.dtype)

    last = (g >= 2) & (lax.rem(g - 2, n_blocks) == n_blocks - 1)

    def body(act_new, act_cur, w_new, w_old, key_off):
        def dot1(rows):
            act_new[rows, :] = lax.dot_general(u_ref[rows, :], hn_ref[...], NT_DIMS, preferred_element_type=F32)

        def dot2(rows):
            acc[rows, :] += jnp.dot(vt_ref[rows, :], w_old[...], preferred_element_type=F32)

        chunks = lambda n, size: [slice(r, r + size) for r in range(0, n, size)]
        for piece in _interleave(
                [functools.partial(dot1, rows) for rows in chunks(u_ref.shape[0], PEER_DOT1_ROWS)],
                _peer_gate_pieces(act_cur, w_new, rank1_ref, p1_ref, m_ref, p0_ref, key_off),
                [functools.partial(dot2, rows) for rows in chunks(vt_ref.shape[0], PEER_DOT2_ROWS)]):
            piece()

        @pl.when(last)
        def _():
            o_ref[...] = h_ref[...] + acc[...].T
            acc[...] = jnp.zeros(acc.shape, acc.dtype)

    even = lax.rem(g, 2) == 0
    pl.when(even)(lambda: body(act0, act1, w1, w0, PEER_KEYS_PER_BLOCK))
    pl.when(jnp.logical_not(even))(lambda: body(act1, act0, w0, w1, 0))


PEER_SHIFT_SLOTS = 2


def _peer(h, hn, u, vt, tables, tm, shift=None):
    t, d = h.shape
    nb = u.shape[0] // PEER_TE
    assert nb % PEER_BLOCKS_PER_TABLE == 0
    n = (t // tm) * nb
    n_steps = n + 2
    blk1 = lambda g: jnp.minimum(g, n - 1)
    blk2 = lambda g: jnp.clip(g - 1, 0, n - 1)
    blk3 = lambda g: jnp.clip(g - 2, 0, n - 1)
    tspec = pl.BlockSpec((PEER_HEADS, PEER_NKEYS, tm), lambda g: (0, 0, blk2(g) // nb))
    kspec = pl.BlockSpec((PEER_HEADS, PEER_TABLE_KEYS, tm),
                         lambda g: (0, (blk2(g) % nb) // PEER_BLOCKS_PER_TABLE, blk2(g) // nb))
    in_specs = [pl.BlockSpec((tm, d), lambda g: (blk3(g) // nb, 0)),
                pl.BlockSpec((tm, d), lambda g: (blk1(g) // nb, 0)),
                pl.BlockSpec((PEER_TE, d), lambda g: (blk1(g) % nb, 0)),
                pl.BlockSpec((None, d, PEER_TE), lambda g: (blk3(g) % nb, 0, 0)), tspec, tspec, kspec, kspec]
    out_specs = [pl.BlockSpec((tm, d), lambda g: (blk3(g) // nb, 0))]
    out_shape = [jax.ShapeDtypeStruct((t, d), F32)]
    scratch = [pltpu.VMEM((PEER_TE, tm), F32), pltpu.VMEM((PEER_TE, tm), F32),
               pltpu.VMEM((PEER_TE, tm), BF16), pltpu.VMEM((PEER_TE, tm), BF16), pltpu.VMEM((d, tm), F32)]
    args = [h, hn, u, vt, *tables]
    if shift is not None:
        c, nw = shift
        parts = (n_steps - PEER_SHIFT_SLOTS + 1) // c.shape[0]
        assert parts >= 1, "more sequences than grid steps"
        while c.shape[1] % parts:
            parts -= 1
        any_spec = pl.BlockSpec(memory_space=pl.ANY)
        in_specs += [any_spec, any_spec]
        out_specs.append(any_spec)
        out_shape.append(jax.ShapeDtypeStruct(c.shape, c.dtype))
        dma_sems = pltpu.SemaphoreType.DMA((PEER_SHIFT_SLOTS,))
        scratch += [pltpu.VMEM((PEER_SHIFT_SLOTS, 1, c.shape[1] // parts) + tuple(c.shape[2:]), c.dtype),
                    dma_sems, dma_sems, dma_sems]
        args += [c, nw]
    out = pl.pallas_call(
        functools.partial(_peer_kernel, n_blocks=nb, n_steps=n_steps, with_shift=shift is not None),
        grid=(n_steps,),
        in_specs=in_specs,
        out_specs=out_specs,
        out_shape=out_shape,
        scratch_shapes=scratch,
        compiler_params=_params(1),
        name="peer",
    )(*args)
    return out if shift is not None else out[0]


def _alibi_slopes():
    return 2.0 ** (-8.0 * jnp.arange(1, N_ALIBI_HEADS + 1, dtype=F32) / N_ALIBI_HEADS)


def _subkey_matrix(subkeys):
    two, nk, dh = subkeys.shape
    eye = jnp.eye(PEER_HEADS * two, dtype=subkeys.dtype).reshape(PEER_HEADS, two, PEER_HEADS, two)
    sk = jnp.einsum("hcgb,cnd->hcngbd", eye, subkeys)
    return sk.reshape(PEER_HEADS * two * nk, PEER_HEADS * two * dh)


def _tail(x2d, oa, ob, zg, w, tm_block, tr, tm_peer, route_shift=None, peer_shift=None):
    h, hn = _block(x2d, oa, ob, zg, w["wa"], w["wb"], w["wo"], w["n2"], tm_block)
    routed = _route(hn, w["wq"], w["sk"], tr, route_shift)
    tables, shifted = routed if route_shift is not None else (routed, None)
    y = _peer(h, hn, w["u"], w["vt"], tables, tm_peer, peer_shift)
    if peer_shift is not None:
        return (y[0], shifted, y[1]) if route_shift is not None else y
    return (y, shifted) if route_shift is not None else y


def kernel(x_prompt, x_sample, cache_a_kv, cache_b1_kv, cache_b2_kv, cache_b3_kv, norm1_w, w_in, q_norm_a,
           k_norm_a, sink_a, q_norm_b, k_norm_b, w_branch_a, w_branch_b, w_out, norm2_w, peer_wq,
           peer_subkeys, peer_u, peer_v):
    assert norm1_w.shape[0] == 1, "single layer"
    b, s, d = x_prompt.shape
    n_dec = x_sample.shape[0]
    assert x_sample.shape[1] == 1

    slopes = _alibi_slopes()
    slopes_a, slopes_b = slopes[:A_Q_HEADS], slopes[A_Q_HEADS:]
    sink = sink_a[0].astype(F32)
    ones = jnp.ones((HEAD_DIM,), F32)
    col_w = jnp.concatenate(
        [jnp.tile(q_norm_a[0], A_Q_HEADS), jnp.tile(k_norm_a[0], A_KV_HEADS), jnp.tile(ones, A_KV_HEADS),
         jnp.tile(q_norm_b[0], B_HEADS), jnp.tile(k_norm_b[0], B_HEADS), jnp.tile(ones, B_HEADS)]
    ).astype(F32)[None]
    gate_w = jnp.ones((1, GATE_W), F32)
    n1 = norm1_w[0].astype(F32)[None]
    w_qkv = w_in[0, :, :QKV_W].astype(BF16)
    w_gate = w_in[0, :, QKV_W:].astype(BF16)
    w = dict(wa=w_branch_a[0].astype(BF16), wb=w_branch_b[0].astype(BF16), wo=w_out[0].astype(BF16),
             n2=norm2_w[0].astype(F32)[None], wq=peer_wq[0].astype(BF16),
             sk=_subkey_matrix(peer_subkeys[0]).astype(BF16),
             u=peer_u[0].astype(BF16),
             vt=peer_v[0].astype(BF16).reshape(PEER_EXPERTS // PEER_TE, PEER_TE, d).transpose(0, 2, 1))

    xs = x_sample.reshape(n_dec, d)
    zs = _proj(xs, n1, w_qkv, col_w, QKV_KINDS, F32, n_dec, "proj_qkv_s")
    zgs = _proj(xs, n1, w_gate, gate_w, GATE_KINDS, BF16, n_dec, "proj_gate_s")
    zs3 = zs.reshape(n_dec, N_QKV_HEADS, HEAD_DIM)
    caches = (cache_a_kv[0], cache_b1_kv[0], cache_b2_kv[0], cache_b3_kv[0])
    news = [jnp.stack([zs3[:, HEAD_KA:HEAD_KA + A_KV_HEADS], zs3[:, HEAD_VA:HEAD_VA + A_KV_HEADS]], axis=1)]
    for g in range(N_B_GROUPS):
        o4 = g * B_HEADS_PER_GROUP
        news.append(jnp.stack([zs3[:, HEAD_KB + o4:HEAD_KB + o4 + B_HEADS_PER_GROUP],
                               zs3[:, HEAD_VB + o4:HEAD_VB + o4 + B_HEADS_PER_GROUP]], axis=1))

    xp = x_prompt.reshape(b * s, d)
    zq = _proj(xp, n1, w_qkv, col_w, QKV_KINDS, F32, 1024, "proj_qkv")
    zg = _proj(xp, n1, w_gate, gate_w, GATE_KINDS, BF16, 1024, "proj_gate")
    zq3 = zq.reshape(b, s, QKV_W)
    oa = _attn_a(zq3, slopes_a, sink).reshape(b * s, A_Q_HEADS * HEAD_DIM)
    ob = _attn_b(zq3, slopes_b).reshape(b * s, B_HEADS_PER_GROUP * HEAD_DIM)
    y_prompt, new_b2, new_b3 = _tail(xp, oa, ob, zg, w, 256, 256, 512, route_shift=(caches[2], news[2]),
                                     peer_shift=(caches[3], news[3]))
    y_prompt = y_prompt.reshape(b, s, d)

    def window(k0, v0, nh, length):
        part = lambda h0: zq3[:, s - length:, h0 * HEAD_DIM:(h0 + nh) * HEAD_DIM].reshape(b, length, nh, HEAD_DIM)
        return jnp.stack([part(k0), part(v0)], axis=2)[None]

    kv_prompt = [window(HEAD_KA, HEAD_VA, A_KV_HEADS, min(A_WINDOW, s))]
    for g, (win, _) in enumerate(B_GROUPS):
        o4 = g * B_HEADS_PER_GROUP
        kv_prompt.append(window(HEAD_KB + o4, HEAD_VB + o4, B_HEADS_PER_GROUP, min(win, s)))

    bcast = lambda a: jnp.broadcast_to(a[..., None], a.shape + (HEAD_DIM,))
    slope_a4 = bcast(slopes_a.reshape(A_KV_HEADS, A_GQA).T)[:, None]
    sink4 = bcast(sink.reshape(A_KV_HEADS, A_GQA).T)[:, None]
    slope_b4 = bcast(slopes_b.reshape(N_B_GROUPS, B_HEADS_PER_GROUP))[:, None]
    oa_s, ob_s = _sample_attn(zs3, *caches, slope_a4, sink4, slope_b4)
    oa_s = oa_s.transpose(0, 2, 1, 3).reshape(n_dec, A_Q_HEADS * HEAD_DIM).astype(BF16)
    ob_s = ob_s.reshape(n_dec, B_HEADS_PER_GROUP * HEAD_DIM).astype(BF16)
    y_sample = _tail(xs, oa_s, ob_s, zgs, w, n_dec, n_dec, n_dec).reshape(n_dec, 1, d)

    kv_sample = [o[None] for o in (*_cache_update(caches[:2], news[:2]), new_b2, new_b3)]

    return (y_prompt, y_sample, *kv_prompt, *kv_sample)
```

```python
import functools
import math

import jax
import jax.numpy as jnp
from jax import lax
from jax.experimental import pallas as pl
from jax.experimental.pallas import tpu as pltpu

F32 = jnp.float32
BF16 = jnp.bfloat16

D_MODEL = 2048
HEAD_DIM = 128
A_Q_HEADS = 8
A_KV_HEADS = 2
A_GQA = A_Q_HEADS // A_KV_HEADS
A_WINDOW = 128
B_GROUPS = ((128, 1), (512, 4), (2048, 16))
B_HEADS_PER_GROUP = 4
N_B_GROUPS = len(B_GROUPS)
B_HEADS = N_B_GROUPS * B_HEADS_PER_GROUP
BAND = 128
N_ALIBI_HEADS = A_Q_HEADS + B_HEADS
ATTN_SCALE = HEAD_DIM ** -0.5
PEER_HEADS = 8
PEER_NKEYS = 128
PEER_EXPERTS = PEER_NKEYS * PEER_NKEYS
PEER_DKEY = 128
PEER_TOPK = 16
NORM_EPS = 1e-6
NEG_INF = -1e30

QKV_W = (A_Q_HEADS + 2 * A_KV_HEADS + 3 * B_HEADS) * HEAD_DIM
GATE_W = 2 * D_MODEL
HEAD_QA, HEAD_KA, HEAD_VA = 0, A_Q_HEADS, A_Q_HEADS + A_KV_HEADS
HEAD_QB = A_Q_HEADS + 2 * A_KV_HEADS
HEAD_KB = HEAD_QB + B_HEADS
HEAD_VB = HEAD_KB + B_HEADS
N_QKV_HEADS = QKV_W // HEAD_DIM

VMEM_LIMIT_BYTES = 56 * 1024 * 1024
LANES = 128

NT_DIMS = (((1,), (1,)), ((), ()))
TN_DIMS = (((0,), (0,)), ((), ()))


def _params(n_grid_axes):
    return pltpu.CompilerParams(
        dimension_semantics=("arbitrary",) * n_grid_axes,
        vmem_limit_bytes=VMEM_LIMIT_BYTES)


def _rms(x):
    return x * lax.rsqrt(jnp.mean(x * x, axis=-1, keepdims=True) + NORM_EPS)


PROJ_TN = 512
PROJ_HEADS_PER_TILE = PROJ_TN // HEAD_DIM


def _proj_kernel(x_ref, n1_ref, w_ref, cw_ref, o_ref, xn_ref, *, kinds):
    j = pl.program_id(1)

    @pl.when(j == 0)
    def _():
        xn_ref[...] = (_rms(x_ref[...]) * n1_ref[...]).astype(BF16)

    z = jnp.dot(xn_ref[...], w_ref[...], preferred_element_type=F32)

    def epilogue(head_is_normed):
        for hh, normed in enumerate(head_is_normed):
            cs = slice(hh * HEAD_DIM, (hh + 1) * HEAD_DIM)
            zh = z[:, cs]
            if normed:
                zh = _rms(zh) * cw_ref[:, cs]
            o_ref[:, cs] = zh.astype(o_ref.dtype)

    for kind in sorted(set(kinds)):
        cond = functools.reduce(jnp.logical_or, [j == jj for jj, k in enumerate(kinds) if k == kind])
        if kind == "gate":
            @pl.when(cond)
            def _():
                o_ref[...] = jax.nn.sigmoid(z).astype(o_ref.dtype)
        else:
            @pl.when(cond)
            def _(kind=kind):
                epilogue(kind)


def _proj(x2d, n1, w, cw, kinds, out_dtype, tm, name):
    t, d = x2d.shape
    n = w.shape[1]
    assert t % tm == 0 and n == PROJ_TN * len(kinds)
    return pl.pallas_call(
        functools.partial(_proj_kernel, kinds=kinds),
        grid=(t // tm, len(kinds)),
        in_specs=[
            pl.BlockSpec((tm, d), lambda i, j: (i, 0)),
            pl.BlockSpec((1, d), lambda i, j: (0, 0)),
            pl.BlockSpec((d, PROJ_TN), lambda i, j: (0, j)),
            pl.BlockSpec((1, PROJ_TN), lambda i, j: (0, j)),
        ],
        out_specs=pl.BlockSpec((tm, PROJ_TN), lambda i, j: (i, j)),
        out_shape=jax.ShapeDtypeStruct((t, n), out_dtype),
        scratch_shapes=[pltpu.VMEM((tm, d), BF16)],
        compiler_params=_params(2),
        name=name,
    )(x2d, n1, w, cw)


_N4, _I4 = (True,) * 4, (False,) * 4
QKV_KINDS = (_N4, _N4, (True, True, False, False)) + (_N4,) * 6 + (_I4,) * 3
GATE_KINDS = ("gate",) * (GATE_W // PROJ_TN)


def _band_softmax(q, k, v, slope_dist, mask, sink):
    s = lax.dot_general(q, k, NT_DIMS, preferred_element_type=F32) * ATTN_SCALE
    logits = jnp.where(mask, s - slope_dist, NEG_INF)
    m = jnp.max(logits, axis=-1, keepdims=True)
    if sink is not None:
        m = jnp.maximum(m, sink)
    p = jnp.exp(logits - m)
    denom = jnp.sum(p, axis=-1, keepdims=True)
    if sink is not None:
        denom = denom + jnp.exp(sink - m)
    o = jnp.dot(p.astype(BF16), v, preferred_element_type=F32) / denom
    return o, m + jnp.log(denom)


def _band_geometry(has_prev):
    nk = 2 * BAND if has_prev else BAND
    qi = lax.broadcasted_iota(jnp.int32, (BAND, nk), 0)
    sj = lax.broadcasted_iota(jnp.int32, (BAND, nk), 1)
    dist = qi - sj + (BAND if has_prev else 0)
    return dist, sj


def _attn_a_kernel(slopes_ref, sink_ref, q_ref, k_ref, v_ref, o_ref):
    kvh = pl.program_id(1)
    n_blocks = q_ref.shape[1] // BAND
    dist, sj = _band_geometry(True)
    in_window = (dist >= 0) & (dist <= A_WINDOW)
    distf = dist.astype(F32)

    def block(blk, carry):
        cur = pl.multiple_of(blk * BAND, BAND)
        prev = pl.multiple_of(jnp.maximum(blk - 1, 0) * BAND, BAND)
        k = jnp.concatenate([k_ref[0, pl.ds(prev, BAND), :], k_ref[0, pl.ds(cur, BAND), :]], axis=0).astype(BF16)
        v = jnp.concatenate([v_ref[0, pl.ds(prev, BAND), :], v_ref[0, pl.ds(cur, BAND), :]], axis=0).astype(BF16)
        mask = in_window & (sj >= jnp.where(blk > 0, 0, BAND))
        for g in range(A_GQA):
            cs = slice(g * HEAD_DIM, (g + 1) * HEAD_DIM)
            head = kvh * A_GQA + g
            q = q_ref[0, pl.ds(cur, BAND), cs].astype(BF16)
            o, _ = _band_softmax(q, k, v, slopes_ref[head] * distf, mask, sink_ref[head])
            o_ref[0, pl.ds(cur, BAND), cs] = o.astype(o_ref.dtype)
        return carry

    lax.fori_loop(0, n_blocks, block, 0)


def _attn_a(zq3, slopes_a, sink):
    b, s, _ = zq3.shape
    gw = A_GQA * HEAD_DIM
    smem = pl.BlockSpec(memory_space=pltpu.SMEM)
    return pl.pallas_call(
        _attn_a_kernel,
        grid=(b, A_KV_HEADS),
        in_specs=[
            smem, smem,
            pl.BlockSpec((1, s, gw), lambda bi, h: (bi, 0, h)),
            pl.BlockSpec((1, s, HEAD_DIM), lambda bi, h: (bi, 0, HEAD_KA + h)),
            pl.BlockSpec((1, s, HEAD_DIM), lambda bi, h: (bi, 0, HEAD_VA + h)),
        ],
        out_specs=pl.BlockSpec((1, s, gw), lambda bi, h: (bi, 0, h)),
        out_shape=jax.ShapeDtypeStruct((b, s, A_Q_HEADS * HEAD_DIM), BF16),
        compiler_params=_params(2),
        name="attn_a",
    )(slopes_a, sink, zq3, zq3, zq3)


def _attn_b_kernel(slopes_ref, *refs):
    q_refs, k_refs, v_refs = refs[0:3], refs[3:6], refs[6:9]
    o_ref, og_ref, lse_ref = refs[9:12]
    h = pl.program_id(1)
    s_len = o_ref.shape[1]

    for g, (win, dil) in enumerate(B_GROUPS):
        n_blocks = s_len // (BAND * dil)
        slope = slopes_ref[g * B_HEADS_PER_GROUP + h]
        for has_prev in (False, True):
            dist, _ = _band_geometry(has_prev)
            mask = (dist >= 0) & (dist <= win // dil)
            slope_dist = slope * (dil * dist).astype(F32)
            for blk in range(n_blocks):
                if (blk > 0) != has_prev:
                    continue
                for r in range(dil):
                    def rows(ref, b0):
                        start = BAND * dil * b0 + r
                        if dil == 1:
                            return ref[0, pl.ds(start, BAND), :]
                        return ref[0, pl.ds(start, BAND, stride=dil), :]

                    q = rows(q_refs[g], blk).astype(BF16)
                    k = rows(k_refs[g], blk)
                    v = rows(v_refs[g], blk)
                    if has_prev:
                        k = jnp.concatenate([rows(k_refs[g], blk - 1), k], axis=0)
                        v = jnp.concatenate([rows(v_refs[g], blk - 1), v], axis=0)
                    o, lse = _band_softmax(q, k.astype(BF16), v.astype(BF16), slope_dist, mask, None)
                    start = BAND * dil * blk + r
                    idx = pl.ds(start, BAND) if dil == 1 else pl.ds(start, BAND, stride=dil)
                    og_ref[g, idx, :] = o
                    lse_ref[g, idx, :] = jnp.broadcast_to(lse, (BAND, HEAD_DIM))

    def combine(c, carry):
        rs = pl.ds(pl.multiple_of(c * BAND, BAND), BAND)
        lses = [lse_ref[g, rs, :] for g in range(N_B_GROUPS)]
        mx = functools.reduce(jnp.maximum, lses)
        ws = [jnp.exp(l - mx) for l in lses]
        acc = functools.reduce(lambda a, b_: a + b_, [w * og_ref[g, rs, :] for g, w in enumerate(ws)])
        o_ref[0, rs, :] = (acc / functools.reduce(lambda a, b_: a + b_, ws)).astype(o_ref.dtype)
        return carry

    lax.fori_loop(0, s_len // BAND, combine, 0)


def _attn_b(zq3, slopes_b):
    b, s, _ = zq3.shape
    smem = pl.BlockSpec(memory_space=pltpu.SMEM)

    def head_spec(base):
        return [pl.BlockSpec((1, s, HEAD_DIM), lambda bi, h, c=base + g * B_HEADS_PER_GROUP: (bi, 0, c + h))
                for g in range(N_B_GROUPS)]

    return pl.pallas_call(
        _attn_b_kernel,
        grid=(b, B_HEADS_PER_GROUP),
        in_specs=[smem] + head_spec(HEAD_QB) + head_spec(HEAD_KB) + head_spec(HEAD_VB),
        out_specs=pl.BlockSpec((1, s, HEAD_DIM), lambda bi, h: (bi, 0, h)),
        out_shape=jax.ShapeDtypeStruct((b, s, B_HEADS_PER_GROUP * HEAD_DIM), BF16),
        scratch_shapes=[pltpu.VMEM((N_B_GROUPS, s, HEAD_DIM), F32),
                        pltpu.VMEM((N_B_GROUPS, s, HEAD_DIM), F32)],
        compiler_params=_params(2),
        name="attn_b",
    )(slopes_b, *([zq3] * 9))


SAMPLE_BS = 4
SUBLANES = 8
MASKED = 1e30


def _window_read(q8, kv8, c, bias, bias_new, sink8, hk):
    per_tile = SUBLANES // (2 * hk)
    row = lax.broadcasted_iota(jnp.int32, (1, 1, SUBLANES, HEAD_DIM), 2)
    is_key = lax.rem(row, 2 * hk) < hk
    first_key = row < hk
    first_value = (row >= hk) & (row < 2 * hk)
    roll = lambda x, shift: pltpu.roll(x, shift, 2)

    def over_positions(x, op):
        for i in range(1, per_tile):
            x = op(x, roll(x, i * 2 * hk))
        return x

    q8 = q8 * ATTN_SCALE
    logits = jnp.sum(c * q8, axis=-1, keepdims=True) - bias
    logit_new = jnp.sum(kv8 * q8, axis=-1, keepdims=True) - bias_new
    mx = over_positions(jnp.max(logits, axis=1, keepdims=True), jnp.maximum)
    mx = jnp.maximum(mx, logit_new)
    if sink8 is not None:
        mx = jnp.maximum(mx, sink8)
    mx = jnp.where(is_key, mx, 0.0)
    p = jnp.exp(logits - mx)
    p_new = jnp.where(first_key, jnp.exp(logit_new - mx), 0.0)
    den = over_positions(jnp.sum(p, axis=1, keepdims=True), jnp.add) + p_new
    if sink8 is not None:
        den = den + jnp.where(first_key, jnp.exp(sink8 - mx), 0.0)
    acc = over_positions(jnp.sum(roll(p, hk) * c, axis=1, keepdims=True), jnp.add) + roll(p_new, hk) * kv8
    o8 = acc / jnp.where(first_value, roll(den, hk), 1.0)
    lse8 = roll(mx + jnp.log(jnp.where(first_key, den, 1.0)), hk)
    return o8, lse8


def _sample_attn_kernel(qa_ref, kva_ref, qb_ref, kvb_ref, ca_ref, cb1_ref, cb2_ref, cb3_ref,
                        bias_a_ref, new_a_ref, sink_ref, bias_b_ref, new_b_ref, oa_ref, ob_ref):
    kva = kva_ref[...][:, None]
    ca = ca_ref[...]
    for g in range(A_GQA):
        o8, _ = _window_read(qa_ref[:, g][:, None], kva, ca, bias_a_ref[g], new_a_ref[...], sink_ref[g], A_KV_HEADS)
        oa_ref[:, g] = o8[:, 0]
    outs, lses = [], []
    for g, c_ref in enumerate((cb1_ref, cb2_ref, cb3_ref)):
        o8, lse8 = _window_read(qb_ref[:, g][:, None], kvb_ref[:, g][:, None], c_ref[...], bias_b_ref[g],
                                new_b_ref[...], None, B_HEADS_PER_GROUP)
        outs.append(o8[:, 0])
        lses.append(lse8[:, 0])
    mx = functools.reduce(jnp.maximum, lses)
    ws = [jnp.exp(l - mx) for l in lses]
    acc = functools.reduce(lambda a, b_: a + b_, [w * o for w, o in zip(ws, outs)])
    ob_ref[...] = acc / functools.reduce(lambda a, b_: a + b_, ws)


def _tile_rows(x, hk):
    per_tile = SUBLANES // (2 * hk)
    pos = jnp.concatenate([x, jnp.zeros_like(x)], axis=-2)
    return jnp.concatenate([pos] * per_tile, axis=-2)


def _key_row_bias(dist, slopes, hk):
    key = dist[:, :, None] * slopes[None, None, :]
    rows = jnp.concatenate([key, jnp.full_like(key, MASKED)], axis=-1).reshape(dist.shape[0], SUBLANES)
    return jnp.broadcast_to(rows[:, :, None], rows.shape + (HEAD_DIM,))


def _sample_attn(zs3, cache_a, cache_b1, cache_b2, cache_b3, slopes_a, sink, slopes_b):
    n = zs3.shape[0]
    bs = SAMPLE_BS
    f32 = lambda x: x.astype(F32)

    def tiles(c, dil):
        nb, l, two, hh, hd = c.shape
        assert l == A_WINDOW * dil
        per_tile = SUBLANES // (two * hh)
        view = c.reshape(nb, l // (dil * per_tile), dil * SUBLANES, hd)
        return view, pl.BlockSpec((bs, view.shape[1], SUBLANES, hd), lambda i: (i, 0, 0, 0))

    ca, ca_spec = tiles(cache_a, 1)
    cb = [tiles(c, dil) for c, (_, dil) in zip((cache_b1, cache_b2, cache_b3), B_GROUPS)]

    heads_a = lambda h0: zs3[:, h0:h0 + A_KV_HEADS]
    qa = zs3[:, HEAD_QA:HEAD_QA + A_Q_HEADS].reshape(n, A_KV_HEADS, A_GQA, HEAD_DIM).transpose(0, 2, 1, 3)
    qa8 = _tile_rows(qa, A_KV_HEADS)
    kva8 = jnp.concatenate([heads_a(HEAD_KA), heads_a(HEAD_VA)] * (SUBLANES // (2 * A_KV_HEADS)), axis=1)
    grp = lambda h0: zs3[:, h0:h0 + B_HEADS].reshape(n, N_B_GROUPS, B_HEADS_PER_GROUP, HEAD_DIM)
    qb8 = _tile_rows(grp(HEAD_QB), B_HEADS_PER_GROUP)
    kvb8 = jnp.concatenate([grp(HEAD_KB), grp(HEAD_VB)], axis=2)

    per_a = SUBLANES // (2 * A_KV_HEADS)
    dist_a = (A_WINDOW - jnp.arange(A_WINDOW, dtype=F32)).reshape(A_WINDOW // per_a, per_a)
    slopes_ag = f32(slopes_a).reshape(A_KV_HEADS, A_GQA).T
    bias_a = jnp.stack([_key_row_bias(dist_a, slopes_ag[g], A_KV_HEADS) for g in range(A_GQA)])
    new_a = _key_row_bias(jnp.zeros((1, per_a), F32), slopes_ag[0], A_KV_HEADS)[0]
    sink8 = _tile_rows(f32(sink).reshape(A_KV_HEADS, A_GQA).T[:, :, None], A_KV_HEADS)
    key_rows = new_a[None, :, :1] == 0.0
    sink8 = jnp.broadcast_to(jnp.where(key_rows, sink8, -MASKED), (A_GQA, SUBLANES, HEAD_DIM))
    slopes_bg = f32(slopes_b).reshape(N_B_GROUPS, B_HEADS_PER_GROUP)
    dist_b = lambda dil: ((A_WINDOW - jnp.arange(A_WINDOW, dtype=F32)) * dil)[:, None]
    bias_b = jnp.stack([_key_row_bias(dist_b(dil), slopes_bg[g], B_HEADS_PER_GROUP)
                        for g, (_, dil) in enumerate(B_GROUPS)])
    new_b = _key_row_bias(jnp.zeros((1, 1), F32), slopes_bg[0], B_HEADS_PER_GROUP)[0]

    full = lambda a: pl.BlockSpec(a.shape, lambda i: (0,) * a.ndim)
    per_seq = lambda a: pl.BlockSpec((bs,) + a.shape[1:], lambda i: (i,) + (0,) * (a.ndim - 1))
    oa8, ob8 = pl.pallas_call(
        _sample_attn_kernel,
        grid=(n // bs,),
        in_specs=[per_seq(qa8), per_seq(kva8), per_seq(qb8), per_seq(kvb8),
                  ca_spec, cb[0][1], cb[1][1], cb[2][1],
                  full(bias_a), full(new_a), full(sink8), full(bias_b), full(new_b)],
        out_specs=[pl.BlockSpec((bs, A_GQA, SUBLANES, HEAD_DIM), lambda i: (i, 0, 0, 0)),
                   pl.BlockSpec((bs, SUBLANES, HEAD_DIM), lambda i: (i, 0, 0))],
        out_shape=[jax.ShapeDtypeStruct((n, A_GQA, SUBLANES, HEAD_DIM), F32),
                   jax.ShapeDtypeStruct((n, SUBLANES, HEAD_DIM), F32)],
        compiler_params=_params(1),
        name="sample_attn",
    )(qa8, kva8, qb8, kvb8, ca, cb[0][0], cb[1][0], cb[2][0], bias_a, new_a, sink8, bias_b, new_b)
    oa = oa8[:, :, A_KV_HEADS:2 * A_KV_HEADS].transpose(0, 2, 1, 3).reshape(n, A_Q_HEADS * HEAD_DIM)
    ob = ob8[:, B_HEADS_PER_GROUP:].reshape(n, B_HEADS_PER_GROUP * HEAD_DIM)
    return oa, ob


CACHE_SLOTS = 3
CACHE_CHUNK_BYTES = 4 * 1024 * 1024


class _ShiftRing:
    def __init__(self, c, nw, o, buf, sem_body, sem_tail, sem_out):
        self.c, self.nw, self.o, self.buf = c, nw, o, buf
        self.sem_body, self.sem_tail, self.sem_out = sem_body, sem_tail, sem_out
        self.n_slots, self.nseq, self.rows = buf.shape[:3]
        n, self.l = c.shape[0], c.shape[1]
        assert n % self.nseq == 0 and self.l % self.rows == 0
        self.parts = self.l // self.rows
        self.n_chunks = (n // self.nseq) * self.parts

    def _where(self, k):
        return k % self.n_slots, pl.ds((k // self.parts) * self.nseq, self.nseq), (k % self.parts) * self.rows

    def _body(self, k):
        slot, seqs, r0 = self._where(k)
        return pltpu.make_async_copy(self.c.at[seqs, pl.ds(r0 + 1, self.rows - 1)],
                                     self.buf.at[slot, :, pl.ds(0, self.rows - 1)], self.sem_body.at[slot])

    def _tail_old(self, k):
        slot, seqs, r0 = self._where(k)
        return pltpu.make_async_copy(self.c.at[seqs, pl.ds(jnp.minimum(r0 + self.rows, self.l - 1), 1)],
                                     self.buf.at[slot, :, pl.ds(self.rows - 1, 1)], self.sem_tail.at[slot])

    def _tail_new(self, k):
        slot, seqs, _ = self._where(k)
        return pltpu.make_async_copy(self.nw.at[seqs], self.buf.at[slot, :, self.rows - 1], self.sem_tail.at[slot])

    def write(self, k):
        slot, seqs, r0 = self._where(k)
        return pltpu.make_async_copy(self.buf.at[slot], self.o.at[seqs, pl.ds(r0, self.rows)], self.sem_out.at[slot])

    def _read(self, k, action):
        action(self._body(k))
        if self.parts == 1:
            action(self._tail_new(k))
        else:
            is_last = (k % self.parts) == self.parts - 1
            pl.when(is_last)(lambda: action(self._tail_new(k)))
            pl.when(jnp.logical_not(is_last))(lambda: action(self._tail_old(k)))

    def start_read(self, k):
        self._read(k, lambda cp: cp.start())

    def wait_read(self, k):
        self._read(k, lambda cp: cp.wait())

    def step(self, k):
        n = self.n_chunks
        written = k + 1 - self.n_slots
        pl.when(k == 0)(lambda: self.start_read(0))
        pl.when((written >= 0) & (written < n))(lambda: self.write(jnp.clip(written, 0, n - 1)).wait())
        pl.when(k + 1 < n)(lambda: self.start_read(jnp.minimum(k + 1, n - 1)))

        @pl.when(k < n)
        def _():
            self.wait_read(jnp.minimum(k, n - 1))
            self.write(jnp.minimum(k, n - 1)).start()

    def drain(self, done_steps):
        for k in range(max(done_steps - self.n_slots + 1, 0), self.n_chunks):
            self.write(k).wait()


def _cache_ring(c, nw, o, buf, sem_body, sem_tail, sem_out):
    ring = _ShiftRing(c, nw, o, buf, sem_body, sem_tail, sem_out)

    def step(k, carry):
        ring.step(k)
        return carry

    lax.fori_loop(0, ring.n_chunks, step, 0)
    ring.drain(ring.n_chunks)


def _cache_chunk(c):
    n, l = c.shape[0], c.shape[1]
    seq_bytes = math.prod(c.shape[1:]) * c.dtype.itemsize
    if seq_bytes <= CACHE_CHUNK_BYTES:
        return min(n, CACHE_CHUNK_BYTES // seq_bytes), l
    return 1, l // (seq_bytes // CACHE_CHUNK_BYTES)


def _cache_kernel(*refs):
    n = len(refs) // 3
    caches, news, outs = refs[:n], refs[n:2 * n], refs[2 * n:]
    for c, nw, o in zip(caches, news, outs):
        nseq, rows = _cache_chunk(c)
        dma_sems = pltpu.SemaphoreType.DMA((CACHE_SLOTS,))
        pl.run_scoped(
            functools.partial(_cache_ring, c, nw, o),
            pltpu.VMEM((CACHE_SLOTS, nseq, rows) + tuple(c.shape[2:]), c.dtype), dma_sems, dma_sems, dma_sems)


def _cache_update(caches, news):
    n = len(caches)
    any_spec = pl.BlockSpec(memory_space=pl.ANY)
    return pl.pallas_call(
        _cache_kernel,
        in_specs=[any_spec] * (2 * n),
        out_specs=[any_spec] * n,
        out_shape=[jax.ShapeDtypeStruct(c.shape, c.dtype) for c in caches],
        compiler_params=pltpu.CompilerParams(vmem_limit_bytes=VMEM_LIMIT_BYTES),
        name="cache_update",
    )(*caches, *news)


def _block_kernel(x_ref, oa_ref, ob_ref, ga_ref, gb_ref, wa_ref, wb_ref, wo_ref, n2_ref, h_ref, hn_ref):
    ya = jnp.dot(oa_ref[...], wa_ref[...], preferred_element_type=F32)
    yb = jnp.dot(ob_ref[...], wb_ref[...], preferred_element_type=F32)
    mix = (ga_ref[...].astype(F32) * ya + gb_ref[...].astype(F32) * yb).astype(BF16)
    h = x_ref[...] + jnp.dot(mix, wo_ref[...], preferred_element_type=F32)
    h_ref[...] = h
    hn_ref[...] = (_rms(h) * n2_ref[...]).astype(BF16)


def _block(x2d, oa, ob, zg, wa, wb, wo, n2, tm):
    t, d = x2d.shape
    row = lambda w: pl.BlockSpec((tm, w), lambda i: (i, 0))
    full = lambda a: pl.BlockSpec(a.shape, lambda i: (0, 0))
    return pl.pallas_call(
        _block_kernel,
        grid=(t // tm,),
        in_specs=[row(d), row(oa.shape[1]), row(ob.shape[1]),
                  pl.BlockSpec((tm, d), lambda i: (i, 0)), pl.BlockSpec((tm, d), lambda i: (i, 1)),
                  full(wa), full(wb), full(wo), full(n2)],
        out_specs=[row(d), row(d)],
        out_shape=[jax.ShapeDtypeStruct((t, d), F32), jax.ShapeDtypeStruct((t, d), BF16)],
        compiler_params=_params(1),
        name="block",
    )(x2d, oa, ob, zg, zg, wa, wb, wo, n2)


RANK_NONE = float(PEER_NKEYS)
CAND_ROWS = tuple(PEER_TOPK // (a + 1) for a in range(PEER_TOPK))
CAND_PAD = tuple(-(-r // 8) * 8 for r in CAND_ROWS)


def _extract(work, n_take, break_ties):
    rows = lax.broadcasted_iota(jnp.int32, work.shape, 0).astype(F32)
    rank = jnp.full(work.shape, RANK_NONE, F32)
    vals = []
    for kk in range(n_take):
        mx = jnp.max(work, axis=0, keepdims=True)
        sel = work == mx
        if break_ties:
            first = jnp.min(jnp.where(sel, rows, float(work.shape[0])), axis=0, keepdims=True)
            sel = rows == first
        rank = jnp.where(sel, float(kk), rank)
        work = jnp.where(sel, -jnp.inf, work)
        vals.append(mx)
    return rank, jnp.concatenate(vals, axis=0)


def _extract_top(work, n_take):
    rank, vals = _extract(work, n_take, break_ties=False)
    ranked = jnp.sum(jnp.where(rank < RANK_NONE, 1.0, 0.0), axis=0, keepdims=True)
    return lax.cond(jnp.max(ranked) > n_take,
                    lambda: _extract(work, n_take, break_ties=True), lambda: (rank, vals))


ROUTE_SHIFT_SLOTS = 2
ROUTE_SHIFT_PER_STEP = 2


def _route_kernel(*refs, n_steps, with_shift):
    hn_ref, wq_ref, sk_ref = refs[:3]
    if with_shift:
        c_ref, nw_ref, rank1_ref, p1_ref, m_ref, p0_ref, oc_ref, s_scr, buf, sem_body, sem_tail, sem_out = refs[3:]
        ring = _ShiftRing(c_ref, nw_ref, oc_ref, buf, sem_body, sem_tail, sem_out)
        assert ring.n_chunks == n_steps * ROUTE_SHIFT_PER_STEP and PEER_HEADS % ROUTE_SHIFT_PER_STEP == 0
        first_chunk = pl.program_id(0) * ROUTE_SHIFT_PER_STEP
    else:
        rank1_ref, p1_ref, m_ref, p0_ref, s_scr = refs[3:]
        ring = None
    q = jnp.dot(hn_ref[...], wq_ref[...], preferred_element_type=F32).astype(BF16)
    s_scr[...] = lax.dot_general(sk_ref[...], q, NT_DIMS, preferred_element_type=F32)

    def head(hh, carry):
        if ring is not None:
            every = PEER_HEADS // ROUTE_SHIFT_PER_STEP
            pl.when(lax.rem(hh, every) == 0)(lambda: ring.step(first_chunk + hh // every))
        base = pl.multiple_of(hh * 2 * PEER_NKEYS, 2 * PEER_NKEYS)
        s0 = s_scr[pl.ds(base, PEER_NKEYS), :]
        s1 = s_scr[pl.ds(base + PEER_NKEYS, PEER_NKEYS), :]
        rank0, vals0 = _extract_top(s0, PEER_TOPK)
        rank1, vals1 = _extract_top(s1, PEER_TOPK)
        cand = []
        for a in range(PEER_TOPK):
            blk = vals0[a:a + 1] + vals1[:CAND_PAD[a]]
            rr = lax.broadcasted_iota(jnp.int32, blk.shape, 0)
            cand.append(jnp.where(rr < CAND_ROWS[a], blk, -jnp.inf))
        taken, best = _extract_top(jnp.concatenate(cand, axis=0), PEER_TOPK)
        zsum = jnp.sum(jnp.exp(best - best[0:1]), axis=0, keepdims=True)
        m = jnp.zeros_like(s0)
        off = 0
        for a in range(PEER_TOPK):
            cnt = jnp.sum(jnp.where(taken[off:off + CAND_PAD[a]] < RANK_NONE, 1.0, 0.0), axis=0, keepdims=True)
            m = jnp.where(rank0 == float(a), cnt, m)
            off += CAND_PAD[a]
        rank1_ref[hh] = rank1
        p1_ref[hh] = jnp.exp(s1 - vals1[0:1])
        m_ref[hh] = m
        p0_ref[hh] = jnp.exp(s0 - vals0[0:1]) / zsum
        return carry

    lax.fori_loop(0, PEER_HEADS, head, 0)
    if ring is not None:
        pl.when(pl.program_id(0) == n_steps - 1)(lambda: ring.drain(ring.n_chunks))


def _route(hn, wq, sk, tr, shift=None):
    t, d = hn.shape
    n_steps = t // tr
    table = jax.ShapeDtypeStruct((PEER_HEADS, PEER_NKEYS, t), F32)
    tspec = pl.BlockSpec((PEER_HEADS, PEER_NKEYS, tr), lambda i: (0, 0, i))
    full = lambda a: pl.BlockSpec(a.shape, lambda i: (0, 0))
    in_specs = [pl.BlockSpec((tr, d), lambda i: (i, 0)), full(wq), full(sk)]
    out_specs, out_shape = [tspec] * 4, [table] * 4
    scratch = [pltpu.VMEM((PEER_HEADS * 2 * PEER_NKEYS, tr), F32)]
    args = [hn, wq, sk]
    if shift is not None:
        c, nw = shift
        nseq = c.shape[0] // (n_steps * ROUTE_SHIFT_PER_STEP)
        assert nseq * n_steps * ROUTE_SHIFT_PER_STEP == c.shape[0]
        any_spec = pl.BlockSpec(memory_space=pl.ANY)
        in_specs += [any_spec, any_spec]
        out_specs = out_specs + [any_spec]
        out_shape = out_shape + [jax.ShapeDtypeStruct(c.shape, c.dtype)]
        dma_sems = pltpu.SemaphoreType.DMA((ROUTE_SHIFT_SLOTS,))
        scratch += [pltpu.VMEM((ROUTE_SHIFT_SLOTS, nseq) + tuple(c.shape[1:]), c.dtype), dma_sems, dma_sems, dma_sems]
        args += [c, nw]
    out = pl.pallas_call(
        functools.partial(_route_kernel, n_steps=n_steps, with_shift=shift is not None),
        grid=(n_steps,),
        in_specs=in_specs,
        out_specs=out_specs,
        out_shape=out_shape,
        scratch_shapes=scratch,
        compiler_params=_params(1),
        name="route",
    )(*args)
    return (out[:4], out[4]) if shift is not None else out


PEER_TE = 512
PEER_KEYS_PER_BLOCK = PEER_TE // PEER_NKEYS
PEER_TABLE_KEYS = 8
PEER_BLOCKS_PER_TABLE = PEER_TABLE_KEYS // PEER_KEYS_PER_BLOCK
assert PEER_BLOCKS_PER_TABLE == 2
SQRT_HALF = math.sqrt(0.5)


PEER_GATE_ROWS = 32


def _peer_gate_pieces(act_ref, w_ref, rank1_ref, p1_ref, m_ref, p0_ref, key_off):
    tm = act_ref.shape[1]

    def piece(ls, r0):
        gates = [jnp.zeros((PEER_GATE_ROWS, LANES), F32) for _ in range(PEER_KEYS_PER_BLOCK)]
        for hh in range(PEER_HEADS):
            r1 = rank1_ref[hh, r0:r0 + PEER_GATE_ROWS, ls]
            p1 = p1_ref[hh, r0:r0 + PEER_GATE_ROWS, ls]
            for j in range(PEER_KEYS_PER_BLOCK):
                kr = slice(key_off + j, key_off + j + 1)
                gates[j] = gates[j] + jnp.where(r1 < m_ref[hh, kr, ls], p1, 0.0) * p0_ref[hh, kr, ls]
        for j in range(PEER_KEYS_PER_BLOCK):
            rs = slice(j * PEER_NKEYS + r0, j * PEER_NKEYS + r0 + PEER_GATE_ROWS)
            a = act_ref[rs, ls]
            w_ref[rs, ls] = (gates[j] * (0.5 * a * (1.0 + lax.erf(a * SQRT_HALF)))).astype(BF16)

    return [functools.partial(piece, slice(c * LANES, (c + 1) * LANES), r0)
            for c in range(tm // LANES) for r0 in range(0, PEER_NKEYS, PEER_GATE_ROWS)]


PEER_DOT1_ROWS = 256
PEER_DOT2_ROWS = 512


def _interleave(*stages):
    tagged = [((i + 0.5) / len(st), si, piece) for si, st in enumerate(stages) for i, piece in enumerate(st)]
    return [piece for _, _, piece in sorted(tagged, key=lambda x: x[:2])]


def _peer_kernel(*refs, n_blocks, n_steps, with_shift):
    h_ref, hn_ref, u_ref, vt_ref, rank1_ref, p1_ref, m_ref, p0_ref = refs[:8]
    g = pl.program_id(0)
    if with_shift:
        c_ref, nw_ref, o_ref, oc_ref, act0, act1, w0, w1, acc, buf, sem_body, sem_tail, sem_out = refs[8:]
        ring = _ShiftRing(c_ref, nw_ref, oc_ref, buf, sem_body, sem_tail, sem_out)
        assert ring.n_chunks + ring.n_slots - 1 <= n_steps, "not enough grid steps to finish the shift"
        ring.step(g)
    else:
        o_ref, act0, act1, w0, w1, acc = refs[8:]

    @pl.when(g == 0)
    def _():
        for ref in (act0, act1, w0, w1, acc):
            ref[...] = jnp.zeros(ref.shape, ref.dtype)

    last = (g >= 2) & (lax.rem(g - 2, n_blocks) == n_blocks - 1)

    def body(act_new, act_cur, w_new, w_old, key_off):
        def dot1(rows):
            act_new[rows, :] = lax.dot_general(u_ref[rows, :], hn_ref[...], NT_DIMS, preferred_element_type=F32)

        def dot2(rows):
            acc[rows, :] += jnp.dot(vt_ref[rows, :], w_old[...], preferred_element_type=F32)

        chunks = lambda n, size: [slice(r, r + size) for r in range(0, n, size)]
        for piece in _interleave(
                [functools.partial(dot1, rows) for rows in chunks(u_ref.shape[0], PEER_DOT1_ROWS)],
                _peer_gate_pieces(act_cur, w_new, rank1_ref, p1_ref, m_ref, p0_ref, key_off),
                [functools.partial(dot2, rows) for rows in chunks(vt_ref.shape[0], PEER_DOT2_ROWS)]):
            piece()

        @pl.when(last)
        def _():
            o_ref[...] = h_ref[...] + acc[...].T
            acc[...] = jnp.zeros(acc.shape, acc.dtype)

    even = lax.rem(g, 2) == 0
    pl.when(even)(lambda: body(act0, act1, w1, w0, PEER_KEYS_PER_BLOCK))
    pl.when(jnp.logical_not(even))(lambda: body(act1, act0, w0, w1, 0))


PEER_SHIFT_SLOTS = 2


def _peer(h, hn, u, vt, tables, tm, shift=None):
    t, d = h.shape
    nb = u.shape[0] // PEER_TE
    assert nb % PEER_BLOCKS_PER_TABLE == 0
    n = (t // tm) * nb
    n_steps = n + 2
    blk1 = lambda g: jnp.minimum(g, n - 1)
    blk2 = lambda g: jnp.clip(g - 1, 0, n - 1)
    blk3 = lambda g: jnp.clip(g - 2, 0, n - 1)
    tspec = pl.BlockSpec((PEER_HEADS, PEER_NKEYS, tm), lambda g: (0, 0, blk2(g) // nb))
    kspec = pl.BlockSpec((PEER_HEADS, PEER_TABLE_KEYS, tm),
                         lambda g: (0, (blk2(g) % nb) // PEER_BLOCKS_PER_TABLE, blk2(g) // nb))
    in_specs = [pl.BlockSpec((tm, d), lambda g: (blk3(g) // nb, 0)),
                pl.BlockSpec((tm, d), lambda g: (blk1(g) // nb, 0)),
                pl.BlockSpec((PEER_TE, d), lambda g: (blk1(g) % nb, 0)),
                pl.BlockSpec((None, d, PEER_TE), lambda g: (blk3(g) % nb, 0, 0)), tspec, tspec, kspec, kspec]
    out_specs = [pl.BlockSpec((tm, d), lambda g: (blk3(g) // nb, 0))]
    out_shape = [jax.ShapeDtypeStruct((t, d), F32)]
    scratch = [pltpu.VMEM((PEER_TE, tm), F32), pltpu.VMEM((PEER_TE, tm), F32),
               pltpu.VMEM((PEER_TE, tm), BF16), pltpu.VMEM((PEER_TE, tm), BF16), pltpu.VMEM((d, tm), F32)]
    args = [h, hn, u, vt, *tables]
    if shift is not None:
        c, nw = shift
        parts = (n_steps - PEER_SHIFT_SLOTS + 1) // c.shape[0]
        assert parts >= 1, "more sequences than grid steps"
        while c.shape[1] % parts:
            parts -= 1
        any_spec = pl.BlockSpec(memory_space=pl.ANY)
        in_specs += [any_spec, any_spec]
        out_specs.append(any_spec)
        out_shape.append(jax.ShapeDtypeStruct(c.shape, c.dtype))
        dma_sems = pltpu.SemaphoreType.DMA((PEER_SHIFT_SLOTS,))
        scratch += [pltpu.VMEM((PEER_SHIFT_SLOTS, 1, c.shape[1] // parts) + tuple(c.shape[2:]), c.dtype),
                    dma_sems, dma_sems, dma_sems]
        args += [c, nw]
    out = pl.pallas_call(
        functools.partial(_peer_kernel, n_blocks=nb, n_steps=n_steps, with_shift=shift is not None),
        grid=(n_steps,),
        in_specs=in_specs,
        out_specs=out_specs,
        out_shape=out_shape,
        scratch_shapes=scratch,
        compiler_params=_params(1),
        name="peer",
    )(*args)
    return out if shift is not None else out[0]


def _alibi_slopes():
    return 2.0 ** (-8.0 * jnp.arange(1, N_ALIBI_HEADS + 1, dtype=F32) / N_ALIBI_HEADS)


def _subkey_matrix(subkeys):
    two, nk, dh = subkeys.shape
    eye = jnp.eye(PEER_HEADS * two, dtype=subkeys.dtype).reshape(PEER_HEADS, two, PEER_HEADS, two)
    sk = jnp.einsum("hcgb,cnd->hcngbd", eye, subkeys)
    return sk.reshape(PEER_HEADS * two * nk, PEER_HEADS * two * dh)


def _tail(x2d, oa, ob, zg, w, tm_block, tr, tm_peer, route_shift=None, peer_shift=None):
    h, hn = _block(x2d, oa, ob, zg, w["wa"], w["wb"], w["wo"], w["n2"], tm_block)
    routed = _route(hn, w["wq"], w["sk"], tr, route_shift)
    tables, shifted = routed if route_shift is not None else (routed, None)
    y = _peer(h, hn, w["u"], w["vt"], tables, tm_peer, peer_shift)
    if peer_shift is not None:
        return (y[0], shifted, y[1]) if route_shift is not None else y
    return (y, shifted) if route_shift is not None else y


def kernel(x_prompt, x_sample, cache_a_kv, cache_b1_kv, cache_b2_kv, cache_b3_kv, norm1_w, w_in, q_norm_a,
           k_norm_a, sink_a, q_norm_b, k_norm_b, w_branch_a, w_branch_b, w_out, norm2_w, peer_wq,
           peer_subkeys, peer_u, peer_v):
    assert norm1_w.shape[0] == 1, "single layer"
    b, s, d = x_prompt.shape
    n_dec = x_sample.shape[0]
    assert x_sample.shape[1] == 1

    slopes = _alibi_slopes()
    slopes_a, slopes_b = slopes[:A_Q_HEADS], slopes[A_Q_HEADS:]
    sink = sink_a[0].astype(F32)
    ones = jnp.ones((HEAD_DIM,), F32)
    col_w = jnp.concatenate(
        [jnp.tile(q_norm_a[0], A_Q_HEADS), jnp.tile(k_norm_a[0], A_KV_HEADS), jnp.tile(ones, A_KV_HEADS),
         jnp.tile(q_norm_b[0], B_HEADS), jnp.tile(k_norm_b[0], B_HEADS), jnp.tile(ones, B_HEADS)]
    ).astype(F32)[None]
    gate_w = jnp.ones((1, GATE_W), F32)
    n1 = norm1_w[0].astype(F32)[None]
    w_qkv = w_in[0, :, :QKV_W].astype(BF16)
    w_gate = w_in[0, :, QKV_W:].astype(BF16)
    w = dict(wa=w_branch_a[0].astype(BF16), wb=w_branch_b[0].astype(BF16), wo=w_out[0].astype(BF16),
             n2=norm2_w[0].astype(F32)[None], wq=peer_wq[0].astype(BF16),
             sk=_subkey_matrix(peer_subkeys[0]).astype(BF16),
             u=peer_u[0].astype(BF16),
             vt=peer_v[0].astype(BF16).reshape(PEER_EXPERTS // PEER_TE, PEER_TE, d).transpose(0, 2, 1))

    xs = x_sample.reshape(n_dec, d)
    zs = _proj(xs, n1, w_qkv, col_w, QKV_KINDS, F32, n_dec, "proj_qkv_s")
    zgs = _proj(xs, n1, w_gate, gate_w, GATE_KINDS, BF16, n_dec, "proj_gate_s")
    zs3 = zs.reshape(n_dec, N_QKV_HEADS, HEAD_DIM)
    caches = (cache_a_kv[0], cache_b1_kv[0], cache_b2_kv[0], cache_b3_kv[0])
    news = [jnp.stack([zs3[:, HEAD_KA:HEAD_KA + A_KV_HEADS], zs3[:, HEAD_VA:HEAD_VA + A_KV_HEADS]], axis=1)]
    for g in range(N_B_GROUPS):
        o4 = g * B_HEADS_PER_GROUP
        news.append(jnp.stack([zs3[:, HEAD_KB + o4:HEAD_KB + o4 + B_HEADS_PER_GROUP],
                               zs3[:, HEAD_VB + o4:HEAD_VB + o4 + B_HEADS_PER_GROUP]], axis=1))

    xp = x_prompt.reshape(b * s, d)
    zq = _proj(xp, n1, w_qkv, col_w, QKV_KINDS, F32, 1024, "proj_qkv")
    zg = _proj(xp, n1, w_gate, gate_w, GATE_KINDS, BF16, 1024, "proj_gate")
    zq3 = zq.reshape(b, s, QKV_W)
    oa = _attn_a(zq3, slopes_a, sink).reshape(b * s, A_Q_HEADS * HEAD_DIM)
    ob = _attn_b(zq3, slopes_b).reshape(b * s, B_HEADS_PER_GROUP * HEAD_DIM)
    y_prompt, new_b2, new_b3 = _tail(xp, oa, ob, zg, w, 256, 256, 512, route_shift=(caches[2], news[2]),
                                     peer_shift=(caches[3], news[3]))
    y_prompt = y_prompt.reshape(b, s, d)

    def window(k0, v0, nh, length):
        part = lambda h0: zq3[:, s - length:, h0 * HEAD_DIM:(h0 + nh) * HEAD_DIM].reshape(b, length, nh, HEAD_DIM)
        return jnp.stack([part(k0), part(v0)], axis=2)[None]

    kv_prompt = [window(HEAD_KA, HEAD_VA, A_KV_HEADS, min(A_WINDOW, s))]
    for g, (win, _) in enumerate(B_GROUPS):
        o4 = g * B_HEADS_PER_GROUP
        kv_prompt.append(window(HEAD_KB + o4, HEAD_VB + o4, B_HEADS_PER_GROUP, min(win, s)))

    oa_s, ob_s = _sample_attn(zs3, *caches, slopes_a, sink, slopes_b)
    y_sample = _tail(xs, oa_s.astype(BF16), ob_s.astype(BF16), zgs, w, n_dec, n_dec, n_dec).reshape(n_dec, 1, d)

    kv_sample = [o[None] for o in (*_cache_update(caches[:2], news[:2]), new_b2, new_b3)]

    return (y_prompt, y_sample, *kv_prompt, *kv_sample)
```

```python
import functools
import math

import jax
import jax.numpy as jnp
from jax import lax
from jax.experimental import pallas as pl
from jax.experimental.pallas import tpu as pltpu

F32 = jnp.float32
BF16 = jnp.bfloat16

D_MODEL = 2048
HEAD_DIM = 128
A_Q_HEADS = 8
A_KV_HEADS = 2
A_GQA = A_Q_HEADS // A_KV_HEADS
A_WINDOW = 128
B_GROUPS = ((128, 1), (512, 4), (2048, 16))
B_HEADS_PER_GROUP = 4
N_B_GROUPS = len(B_GROUPS)
B_HEADS = N_B_GROUPS * B_HEADS_PER_GROUP
BAND = 128
N_ALIBI_HEADS = A_Q_HEADS + B_HEADS
ATTN_SCALE = HEAD_DIM ** -0.5
PEER_HEADS = 8
PEER_NKEYS = 128
PEER_EXPERTS = PEER_NKEYS * PEER_NKEYS
PEER_DKEY = 128
PEER_TOPK = 16
NORM_EPS = 1e-6
NEG_INF = -1e30

QKV_W = (A_Q_HEADS + 2 * A_KV_HEADS + 3 * B_HEADS) * HEAD_DIM
GATE_W = 2 * D_MODEL
HEAD_QA, HEAD_KA, HEAD_VA = 0, A_Q_HEADS, A_Q_HEADS + A_KV_HEADS
HEAD_QB = A_Q_HEADS + 2 * A_KV_HEADS
HEAD_KB = HEAD_QB + B_HEADS
HEAD_VB = HEAD_KB + B_HEADS
N_QKV_HEADS = QKV_W // HEAD_DIM

VMEM_LIMIT_BYTES = 56 * 1024 * 1024
LANES = 128

NT_DIMS = (((1,), (1,)), ((), ()))
TN_DIMS = (((0,), (0,)), ((), ()))


def _params(n_grid_axes):
    return pltpu.CompilerParams(
        dimension_semantics=("arbitrary",) * n_grid_axes,
        vmem_limit_bytes=VMEM_LIMIT_BYTES)


def _rms(x):
    return x * lax.rsqrt(jnp.mean(x * x, axis=-1, keepdims=True) + NORM_EPS)


PROJ_TN = 512
PROJ_HEADS_PER_TILE = PROJ_TN // HEAD_DIM


def _proj_kernel(x_ref, n1_ref, w_ref, cw_ref, o_ref, xn_ref, *, kinds):
    j = pl.program_id(1)

    @pl.when(j == 0)
    def _():
        xn_ref[...] = (_rms(x_ref[...]) * n1_ref[...]).astype(BF16)

    z = jnp.dot(xn_ref[...], w_ref[...], preferred_element_type=F32)

    def epilogue(head_is_normed):
        for hh, normed in enumerate(head_is_normed):
            cs = slice(hh * HEAD_DIM, (hh + 1) * HEAD_DIM)
            zh = z[:, cs]
            if normed:
                zh = _rms(zh) * cw_ref[:, cs]
            o_ref[:, cs] = zh.astype(o_ref.dtype)

    for kind in sorted(set(kinds)):
        cond = functools.reduce(jnp.logical_or, [j == jj for jj, k in enumerate(kinds) if k == kind])
        if kind == "gate":
            @pl.when(cond)
            def _():
                o_ref[...] = jax.nn.sigmoid(z).astype(o_ref.dtype)
        else:
            @pl.when(cond)
            def _(kind=kind):
                epilogue(kind)


def _proj(x2d, n1, w, cw, kinds, out_dtype, tm, name):
    t, d = x2d.shape
    n = w.shape[1]
    assert t % tm == 0 and n == PROJ_TN * len(kinds)
    return pl.pallas_call(
        functools.partial(_proj_kernel, kinds=kinds),
        grid=(t // tm, len(kinds)),
        in_specs=[
            pl.BlockSpec((tm, d), lambda i, j: (i, 0)),
            pl.BlockSpec((1, d), lambda i, j: (0, 0)),
            pl.BlockSpec((d, PROJ_TN), lambda i, j: (0, j)),
            pl.BlockSpec((1, PROJ_TN), lambda i, j: (0, j)),
        ],
        out_specs=pl.BlockSpec((tm, PROJ_TN), lambda i, j: (i, j)),
        out_shape=jax.ShapeDtypeStruct((t, n), out_dtype),
        scratch_shapes=[pltpu.VMEM((tm, d), BF16)],
        compiler_params=_params(2),
        name=name,
    )(x2d, n1, w, cw)


_N4, _I4 = (True,) * 4, (False,) * 4
QKV_KINDS = (_N4, _N4, (True, True, False, False)) + (_N4,) * 6 + (_I4,) * 3
GATE_KINDS = ("gate",) * (GATE_W // PROJ_TN)


def _band_softmax(q, k, v, slope_dist, mask, sink):
    s = lax.dot_general(q, k, NT_DIMS, preferred_element_type=F32) * ATTN_SCALE
    logits = jnp.where(mask, s - slope_dist, NEG_INF)
    m = jnp.max(logits, axis=-1, keepdims=True)
    if sink is not None:
        m = jnp.maximum(m, sink)
    p = jnp.exp(logits - m)
    denom = jnp.sum(p, axis=-1, keepdims=True)
    if sink is not None:
        denom = denom + jnp.exp(sink - m)
    o = jnp.dot(p.astype(BF16), v, preferred_element_type=F32) / denom
    return o, m + jnp.log(denom)


def _band_geometry(has_prev):
    nk = 2 * BAND if has_prev else BAND
    qi = lax.broadcasted_iota(jnp.int32, (BAND, nk), 0)
    sj = lax.broadcasted_iota(jnp.int32, (BAND, nk), 1)
    dist = qi - sj + (BAND if has_prev else 0)
    return dist, sj


def _attn_a_kernel(slopes_ref, sink_ref, q_ref, k_ref, v_ref, o_ref):
    kvh = pl.program_id(1)
    n_blocks = q_ref.shape[1] // BAND
    dist, sj = _band_geometry(True)
    in_window = (dist >= 0) & (dist <= A_WINDOW)
    distf = dist.astype(F32)

    def block(blk, carry):
        cur = pl.multiple_of(blk * BAND, BAND)
        prev = pl.multiple_of(jnp.maximum(blk - 1, 0) * BAND, BAND)
        k = jnp.concatenate([k_ref[0, pl.ds(prev, BAND), :], k_ref[0, pl.ds(cur, BAND), :]], axis=0).astype(BF16)
        v = jnp.concatenate([v_ref[0, pl.ds(prev, BAND), :], v_ref[0, pl.ds(cur, BAND), :]], axis=0).astype(BF16)
        mask = in_window & (sj >= jnp.where(blk > 0, 0, BAND))
        for g in range(A_GQA):
            cs = slice(g * HEAD_DIM, (g + 1) * HEAD_DIM)
            head = kvh * A_GQA + g
            q = q_ref[0, pl.ds(cur, BAND), cs].astype(BF16)
            o, _ = _band_softmax(q, k, v, slopes_ref[head] * distf, mask, sink_ref[head])
            o_ref[0, pl.ds(cur, BAND), cs] = o.astype(o_ref.dtype)
        return carry

    lax.fori_loop(0, n_blocks, block, 0)


def _attn_a(zq3, slopes_a, sink):
    b, s, _ = zq3.shape
    gw = A_GQA * HEAD_DIM
    smem = pl.BlockSpec(memory_space=pltpu.SMEM)
    return pl.pallas_call(
        _attn_a_kernel,
        grid=(b, A_KV_HEADS),
        in_specs=[
            smem, smem,
            pl.BlockSpec((1, s, gw), lambda bi, h: (bi, 0, h)),
            pl.BlockSpec((1, s, HEAD_DIM), lambda bi, h: (bi, 0, HEAD_KA + h)),
            pl.BlockSpec((1, s, HEAD_DIM), lambda bi, h: (bi, 0, HEAD_VA + h)),
        ],
        out_specs=pl.BlockSpec((1, s, gw), lambda bi, h: (bi, 0, h)),
        out_shape=jax.ShapeDtypeStruct((b, s, A_Q_HEADS * HEAD_DIM), BF16),
        compiler_params=_params(2),
        name="attn_a",
    )(slopes_a, sink, zq3, zq3, zq3)


def _attn_b_kernel(slopes_ref, *refs):
    q_refs, k_refs, v_refs = refs[0:3], refs[3:6], refs[6:9]
    o_ref, og_ref, lse_ref = refs[9:12]
    h = pl.program_id(1)
    s_len = o_ref.shape[1]

    for g, (win, dil) in enumerate(B_GROUPS):
        n_blocks = s_len // (BAND * dil)
        slope = slopes_ref[g * B_HEADS_PER_GROUP + h]
        for has_prev in (False, True):
            dist, _ = _band_geometry(has_prev)
            mask = (dist >= 0) & (dist <= win // dil)
            slope_dist = slope * (dil * dist).astype(F32)
            for blk in range(n_blocks):
                if (blk > 0) != has_prev:
                    continue
                for r in range(dil):
                    def rows(ref, b0):
                        start = BAND * dil * b0 + r
                        if dil == 1:
                            return ref[0, pl.ds(start, BAND), :]
                        return ref[0, pl.ds(start, BAND, stride=dil), :]

                    q = rows(q_refs[g], blk).astype(BF16)
                    k = rows(k_refs[g], blk)
                    v = rows(v_refs[g], blk)
                    if has_prev:
                        k = jnp.concatenate([rows(k_refs[g], blk - 1), k], axis=0)
                        v = jnp.concatenate([rows(v_refs[g], blk - 1), v], axis=0)
                    o, lse = _band_softmax(q, k.astype(BF16), v.astype(BF16), slope_dist, mask, None)
                    start = BAND * dil * blk + r
                    idx = pl.ds(start, BAND) if dil == 1 else pl.ds(start, BAND, stride=dil)
                    og_ref[g, idx, :] = o
                    lse_ref[g, idx, :] = jnp.broadcast_to(lse, (BAND, HEAD_DIM))

    def combine(c, carry):
        rs = pl.ds(pl.multiple_of(c * BAND, BAND), BAND)
        lses = [lse_ref[g, rs, :] for g in range(N_B_GROUPS)]
        mx = functools.reduce(jnp.maximum, lses)
        ws = [jnp.exp(l - mx) for l in lses]
        acc = functools.reduce(lambda a, b_: a + b_, [w * og_ref[g, rs, :] for g, w in enumerate(ws)])
        o_ref[0, rs, :] = (acc / functools.reduce(lambda a, b_: a + b_, ws)).astype(o_ref.dtype)
        return carry

    lax.fori_loop(0, s_len // BAND, combine, 0)


def _attn_b(zq3, slopes_b):
    b, s, _ = zq3.shape
    smem = pl.BlockSpec(memory_space=pltpu.SMEM)

    def head_spec(base):
        return [pl.BlockSpec((1, s, HEAD_DIM), lambda bi, h, c=base + g * B_HEADS_PER_GROUP: (bi, 0, c + h))
                for g in range(N_B_GROUPS)]

    return pl.pallas_call(
        _attn_b_kernel,
        grid=(b, B_HEADS_PER_GROUP),
        in_specs=[smem] + head_spec(HEAD_QB) + head_spec(HEAD_KB) + head_spec(HEAD_VB),
        out_specs=pl.BlockSpec((1, s, HEAD_DIM), lambda bi, h: (bi, 0, h)),
        out_shape=jax.ShapeDtypeStruct((b, s, B_HEADS_PER_GROUP * HEAD_DIM), BF16),
        scratch_shapes=[pltpu.VMEM((N_B_GROUPS, s, HEAD_DIM), F32),
                        pltpu.VMEM((N_B_GROUPS, s, HEAD_DIM), F32)],
        compiler_params=_params(2),
        name="attn_b",
    )(slopes_b, *([zq3] * 9))


SAMPLE_BS = 4
SUBLANES = 8
MASKED = 1e30


def _window_read(q8, kv8, c, bias, bias_new, sink8, hk):
    per_tile = SUBLANES // (2 * hk)
    row = lax.broadcasted_iota(jnp.int32, (1, 1, SUBLANES, HEAD_DIM), 2)
    is_key = lax.rem(row, 2 * hk) < hk
    first_key = row < hk
    first_value = (row >= hk) & (row < 2 * hk)
    roll = lambda x, shift: pltpu.roll(x, shift, 2)

    def over_positions(x, op):
        for i in range(1, per_tile):
            x = op(x, roll(x, i * 2 * hk))
        return x

    q8 = q8 * ATTN_SCALE
    logits = jnp.sum(c * q8, axis=-1, keepdims=True) - bias
    logit_new = jnp.sum(kv8 * q8, axis=-1, keepdims=True) - bias_new
    mx = over_positions(jnp.max(logits, axis=1, keepdims=True), jnp.maximum)
    mx = jnp.maximum(mx, logit_new)
    if sink8 is not None:
        mx = jnp.maximum(mx, sink8)
    mx = jnp.where(is_key, mx, 0.0)
    p = jnp.exp(logits - mx)
    p_new = jnp.where(first_key, jnp.exp(logit_new - mx), 0.0)
    den = over_positions(jnp.sum(p, axis=1, keepdims=True), jnp.add) + p_new
    if sink8 is not None:
        den = den + jnp.where(first_key, jnp.exp(sink8 - mx), 0.0)
    acc = over_positions(jnp.sum(roll(p, hk) * c, axis=1, keepdims=True), jnp.add) + roll(p_new, hk) * kv8
    o8 = acc / jnp.where(first_value, roll(den, hk), 1.0)
    lse8 = roll(mx + jnp.log(jnp.where(first_key, den, 1.0)), hk)
    return o8, lse8


def _sample_attn_kernel(qa_ref, kva_ref, qb_ref, kvb_ref, ca_ref, cb1_ref, cb2_ref, cb3_ref,
                        bias_a_ref, new_a_ref, sink_ref, bias_b_ref, new_b_ref, oa_ref, ob_ref):
    kva = kva_ref[...][:, None]
    ca = ca_ref[...]
    for g in range(A_GQA):
        o8, _ = _window_read(qa_ref[:, g][:, None], kva, ca, bias_a_ref[g], new_a_ref[...], sink_ref[g], A_KV_HEADS)
        oa_ref[:, g] = o8[:, 0]
    outs, lses = [], []
    for g, c_ref in enumerate((cb1_ref, cb2_ref, cb3_ref)):
        o8, lse8 = _window_read(qb_ref[:, g][:, None], kvb_ref[:, g][:, None], c_ref[...], bias_b_ref[g],
                                new_b_ref[...], None, B_HEADS_PER_GROUP)
        outs.append(o8[:, 0])
        lses.append(lse8[:, 0])
    mx = functools.reduce(jnp.maximum, lses)
    ws = [jnp.exp(l - mx) for l in lses]
    acc = functools.reduce(lambda a, b_: a + b_, [w * o for w, o in zip(ws, outs)])
    ob_ref[...] = acc / functools.reduce(lambda a, b_: a + b_, ws)


def _tile_rows(x, hk):
    per_tile = SUBLANES // (2 * hk)
    pos = jnp.concatenate([x, jnp.zeros_like(x)], axis=-2)
    return jnp.concatenate([pos] * per_tile, axis=-2)


def _key_row_bias(dist, slopes, hk):
    key = dist[:, :, None] * slopes[None, None, :]
    rows = jnp.concatenate([key, jnp.full_like(key, MASKED)], axis=-1).reshape(dist.shape[0], SUBLANES)
    return jnp.broadcast_to(rows[:, :, None], rows.shape + (HEAD_DIM,))


def _sample_attn(zs3, cache_a, cache_b1, cache_b2, cache_b3, slopes_a, sink, slopes_b):
    n = zs3.shape[0]
    bs = SAMPLE_BS
    f32 = lambda x: x.astype(F32)

    def tiles(c, dil):
        nb, l, two, hh, hd = c.shape
        assert l == A_WINDOW * dil
        per_tile = SUBLANES // (two * hh)
        view = c.reshape(nb, l // (dil * per_tile), dil * SUBLANES, hd)
        return view, pl.BlockSpec((bs, view.shape[1], SUBLANES, hd), lambda i: (i, 0, 0, 0))

    ca, ca_spec = tiles(cache_a, 1)
    cb = [tiles(c, dil) for c, (_, dil) in zip((cache_b1, cache_b2, cache_b3), B_GROUPS)]

    heads_a = lambda h0: zs3[:, h0:h0 + A_KV_HEADS]
    qa = zs3[:, HEAD_QA:HEAD_QA + A_Q_HEADS].reshape(n, A_KV_HEADS, A_GQA, HEAD_DIM).transpose(0, 2, 1, 3)
    qa8 = _tile_rows(qa, A_KV_HEADS)
    kva8 = jnp.concatenate([heads_a(HEAD_KA), heads_a(HEAD_VA)] * (SUBLANES // (2 * A_KV_HEADS)), axis=1)
    grp = lambda h0: zs3[:, h0:h0 + B_HEADS].reshape(n, N_B_GROUPS, B_HEADS_PER_GROUP, HEAD_DIM)
    qb8 = _tile_rows(grp(HEAD_QB), B_HEADS_PER_GROUP)
    kvb8 = jnp.concatenate([grp(HEAD_KB), grp(HEAD_VB)], axis=2)

    per_a = SUBLANES // (2 * A_KV_HEADS)
    dist_a = (A_WINDOW - jnp.arange(A_WINDOW, dtype=F32)).reshape(A_WINDOW // per_a, per_a)
    slopes_ag = f32(slopes_a).reshape(A_KV_HEADS, A_GQA).T
    bias_a = jnp.stack([_key_row_bias(dist_a, slopes_ag[g], A_KV_HEADS) for g in range(A_GQA)])
    new_a = _key_row_bias(jnp.zeros((1, per_a), F32), slopes_ag[0], A_KV_HEADS)[0]
    sink8 = _tile_rows(f32(sink).reshape(A_KV_HEADS, A_GQA).T[:, :, None], A_KV_HEADS)
    key_rows = new_a[None, :, :1] == 0.0
    sink8 = jnp.broadcast_to(jnp.where(key_rows, sink8, -MASKED), (A_GQA, SUBLANES, HEAD_DIM))
    slopes_bg = f32(slopes_b).reshape(N_B_GROUPS, B_HEADS_PER_GROUP)
    dist_b = lambda dil: ((A_WINDOW - jnp.arange(A_WINDOW, dtype=F32)) * dil)[:, None]
    bias_b = jnp.stack([_key_row_bias(dist_b(dil), slopes_bg[g], B_HEADS_PER_GROUP)
                        for g, (_, dil) in enumerate(B_GROUPS)])
    new_b = _key_row_bias(jnp.zeros((1, 1), F32), slopes_bg[0], B_HEADS_PER_GROUP)[0]

    full = lambda a: pl.BlockSpec(a.shape, lambda i: (0,) * a.ndim)
    per_seq = lambda a: pl.BlockSpec((bs,) + a.shape[1:], lambda i: (i,) + (0,) * (a.ndim - 1))
    oa8, ob8 = pl.pallas_call(
        _sample_attn_kernel,
        grid=(n // bs,),
        in_specs=[per_seq(qa8), per_seq(kva8), per_seq(qb8), per_seq(kvb8),
                  ca_spec, cb[0][1], cb[1][1], cb[2][1],
                  full(bias_a), full(new_a), full(sink8), full(bias_b), full(new_b)],
        out_specs=[pl.BlockSpec((bs, A_GQA, SUBLANES, HEAD_DIM), lambda i: (i, 0, 0, 0)),
                   pl.BlockSpec((bs, SUBLANES, HEAD_DIM), lambda i: (i, 0, 0))],
        out_shape=[jax.ShapeDtypeStruct((n, A_GQA, SUBLANES, HEAD_DIM), F32),
                   jax.ShapeDtypeStruct((n, SUBLANES, HEAD_DIM), F32)],
        compiler_params=_params(1),
        name="sample_attn",
    )(qa8, kva8, qb8, kvb8, ca, cb[0][0], cb[1][0], cb[2][0], bias_a, new_a, sink8, bias_b, new_b)
    oa = oa8[:, :, A_KV_HEADS:2 * A_KV_HEADS].transpose(0, 2, 1, 3).reshape(n, A_Q_HEADS * HEAD_DIM)
    ob = ob8[:, B_HEADS_PER_GROUP:].reshape(n, B_HEADS_PER_GROUP * HEAD_DIM)
    return oa, ob


CACHE_SLOTS = 3
CACHE_CHUNK_BYTES = 4 * 1024 * 1024


class _ShiftRing:
    def __init__(self, c, nw, o, buf, sem_body, sem_tail, sem_out):
        self.c, self.nw, self.o, self.buf = c, nw, o, buf
        self.sem_body, self.sem_tail, self.sem_out = sem_body, sem_tail, sem_out
        self.n_slots, self.nseq, self.rows = buf.shape[:3]
        n, self.l = c.shape[0], c.shape[1]
        assert n % self.nseq == 0 and self.l % self.rows == 0
        self.parts = self.l // self.rows
        self.n_chunks = (n // self.nseq) * self.parts

    def _where(self, k):
        return k % self.n_slots, pl.ds((k // self.parts) * self.nseq, self.nseq), (k % self.parts) * self.rows

    def _body(self, k):
        slot, seqs, r0 = self._where(k)
        return pltpu.make_async_copy(self.c.at[seqs, pl.ds(r0 + 1, self.rows - 1)],
                                     self.buf.at[slot, :, pl.ds(0, self.rows - 1)], self.sem_body.at[slot])

    def _tail_old(self, k):
        slot, seqs, r0 = self._where(k)
        return pltpu.make_async_copy(self.c.at[seqs, pl.ds(jnp.minimum(r0 + self.rows, self.l - 1), 1)],
                                     self.buf.at[slot, :, pl.ds(self.rows - 1, 1)], self.sem_tail.at[slot])

    def _tail_new(self, k):
        slot, seqs, _ = self._where(k)
        return pltpu.make_async_copy(self.nw.at[seqs], self.buf.at[slot, :, self.rows - 1], self.sem_tail.at[slot])

    def write(self, k):
        slot, seqs, r0 = self._where(k)
        return pltpu.make_async_copy(self.buf.at[slot], self.o.at[seqs, pl.ds(r0, self.rows)], self.sem_out.at[slot])

    def _read(self, k, action):
        action(self._body(k))
        if self.parts == 1:
            action(self._tail_new(k))
        else:
            is_last = (k % self.parts) == self.parts - 1
            pl.when(is_last)(lambda: action(self._tail_new(k)))
            pl.when(jnp.logical_not(is_last))(lambda: action(self._tail_old(k)))

    def start_read(self, k):
        self._read(k, lambda cp: cp.start())

    def wait_read(self, k):
        self._read(k, lambda cp: cp.wait())

    def step(self, k):
        n = self.n_chunks
        written = k + 1 - self.n_slots
        pl.when(k == 0)(lambda: self.start_read(0))
        pl.when((written >= 0) & (written < n))(lambda: self.write(jnp.clip(written, 0, n - 1)).wait())
        pl.when(k + 1 < n)(lambda: self.start_read(jnp.minimum(k + 1, n - 1)))

        @pl.when(k < n)
        def _():
            self.wait_read(jnp.minimum(k, n - 1))
            self.write(jnp.minimum(k, n - 1)).start()

    def drain(self, done_steps):
        for k in range(max(done_steps - self.n_slots + 1, 0), self.n_chunks):
            self.write(k).wait()


def _cache_ring(c, nw, o, buf, sem_body, sem_tail, sem_out):
    ring = _ShiftRing(c, nw, o, buf, sem_body, sem_tail, sem_out)

    def step(k, carry):
        ring.step(k)
        return carry

    lax.fori_loop(0, ring.n_chunks, step, 0)
    ring.drain(ring.n_chunks)


def _cache_chunk(c):
    n, l = c.shape[0], c.shape[1]
    seq_bytes = math.prod(c.shape[1:]) * c.dtype.itemsize
    if seq_bytes <= CACHE_CHUNK_BYTES:
        return min(n, CACHE_CHUNK_BYTES // seq_bytes), l
    return 1, l // (seq_bytes // CACHE_CHUNK_BYTES)


def _cache_kernel(*refs):
    n = len(refs) // 3
    caches, news, outs = refs[:n], refs[n:2 * n], refs[2 * n:]
    for c, nw, o in zip(caches, news, outs):
        nseq, rows = _cache_chunk(c)
        dma_sems = pltpu.SemaphoreType.DMA((CACHE_SLOTS,))
        pl.run_scoped(
            functools.partial(_cache_ring, c, nw, o),
            pltpu.VMEM((CACHE_SLOTS, nseq, rows) + tuple(c.shape[2:]), c.dtype), dma_sems, dma_sems, dma_sems)


def _cache_update(caches, news):
    n = len(caches)
    any_spec = pl.BlockSpec(memory_space=pl.ANY)
    return pl.pallas_call(
        _cache_kernel,
        in_specs=[any_spec] * (2 * n),
        out_specs=[any_spec] * n,
        out_shape=[jax.ShapeDtypeStruct(c.shape, c.dtype) for c in caches],
        compiler_params=pltpu.CompilerParams(vmem_limit_bytes=VMEM_LIMIT_BYTES),
        name="cache_update",
    )(*caches, *news)


def _block_kernel(x_ref, oa_ref, ob_ref, ga_ref, gb_ref, wa_ref, wb_ref, wo_ref, n2_ref, h_ref, hn_ref):
    ya = jnp.dot(oa_ref[...], wa_ref[...], preferred_element_type=F32)
    yb = jnp.dot(ob_ref[...], wb_ref[...], preferred_element_type=F32)
    mix = (ga_ref[...].astype(F32) * ya + gb_ref[...].astype(F32) * yb).astype(BF16)
    h = x_ref[...] + jnp.dot(mix, wo_ref[...], preferred_element_type=F32)
    h_ref[...] = h
    hn_ref[...] = (_rms(h) * n2_ref[...]).astype(BF16)


def _block(x2d, oa, ob, zg, wa, wb, wo, n2, tm):
    t, d = x2d.shape
    row = lambda w: pl.BlockSpec((tm, w), lambda i: (i, 0))
    full = lambda a: pl.BlockSpec(a.shape, lambda i: (0, 0))
    return pl.pallas_call(
        _block_kernel,
        grid=(t // tm,),
        in_specs=[row(d), row(oa.shape[1]), row(ob.shape[1]),
                  pl.BlockSpec((tm, d), lambda i: (i, 0)), pl.BlockSpec((tm, d), lambda i: (i, 1)),
                  full(wa), full(wb), full(wo), full(n2)],
        out_specs=[row(d), row(d)],
        out_shape=[jax.ShapeDtypeStruct((t, d), F32), jax.ShapeDtypeStruct((t, d), BF16)],
        compiler_params=_params(1),
        name="block",
    )(x2d, oa, ob, zg, zg, wa, wb, wo, n2)


RANK_NONE = float(PEER_NKEYS)
CAND_ROWS = tuple(PEER_TOPK // (a + 1) for a in range(PEER_TOPK))
CAND_PAD = tuple(-(-r // 8) * 8 for r in CAND_ROWS)


def _extract(work, n_take, break_ties):
    rows = lax.broadcasted_iota(jnp.int32, work.shape, 0).astype(F32)
    rank = jnp.full(work.shape, RANK_NONE, F32)
    vals = []
    for kk in range(n_take):
        mx = jnp.max(work, axis=0, keepdims=True)
        sel = work == mx
        if break_ties:
            first = jnp.min(jnp.where(sel, rows, float(work.shape[0])), axis=0, keepdims=True)
            sel = rows == first
        rank = jnp.where(sel, float(kk), rank)
        work = jnp.where(sel, -jnp.inf, work)
        vals.append(mx)
    return rank, jnp.concatenate(vals, axis=0)


def _extract_top(work, n_take):
    rank, vals = _extract(work, n_take, break_ties=False)
    ranked = jnp.sum(jnp.where(rank < RANK_NONE, 1.0, 0.0), axis=0, keepdims=True)
    return lax.cond(jnp.max(ranked) > n_take,
                    lambda: _extract(work, n_take, break_ties=True), lambda: (rank, vals))


ROUTE_SHIFT_SLOTS = 2
ROUTE_SHIFT_PER_STEP = 2


def _route_kernel(*refs, n_steps, with_shift):
    hn_ref, wq_ref, sk_ref = refs[:3]
    if with_shift:
        c_ref, nw_ref, rank1_ref, p1_ref, m_ref, p0_ref, oc_ref, s_scr, buf, sem_body, sem_tail, sem_out = refs[3:]
        ring = _ShiftRing(c_ref, nw_ref, oc_ref, buf, sem_body, sem_tail, sem_out)
        assert ring.n_chunks == n_steps * ROUTE_SHIFT_PER_STEP and PEER_HEADS % ROUTE_SHIFT_PER_STEP == 0
        first_chunk = pl.program_id(0) * ROUTE_SHIFT_PER_STEP
    else:
        rank1_ref, p1_ref, m_ref, p0_ref, s_scr = refs[3:]
        ring = None
    q = jnp.dot(hn_ref[...], wq_ref[...], preferred_element_type=F32).astype(BF16)
    s_scr[...] = lax.dot_general(sk_ref[...], q, NT_DIMS, preferred_element_type=F32)

    def head(hh, carry):
        if ring is not None:
            every = PEER_HEADS // ROUTE_SHIFT_PER_STEP
            pl.when(lax.rem(hh, every) == 0)(lambda: ring.step(first_chunk + hh // every))
        base = pl.multiple_of(hh * 2 * PEER_NKEYS, 2 * PEER_NKEYS)
        s0 = s_scr[pl.ds(base, PEER_NKEYS), :]
        s1 = s_scr[pl.ds(base + PEER_NKEYS, PEER_NKEYS), :]
        rank0, vals0 = _extract_top(s0, PEER_TOPK)
        rank1, vals1 = _extract_top(s1, PEER_TOPK)
        cand = []
        for a in range(PEER_TOPK):
            blk = vals0[a:a + 1] + vals1[:CAND_PAD[a]]
            rr = lax.broadcasted_iota(jnp.int32, blk.shape, 0)
            cand.append(jnp.where(rr < CAND_ROWS[a], blk, -jnp.inf))
        taken, best = _extract_top(jnp.concatenate(cand, axis=0), PEER_TOPK)
        zsum = jnp.sum(jnp.exp(best - best[0:1]), axis=0, keepdims=True)
        m = jnp.zeros_like(s0)
        off = 0
        for a in range(PEER_TOPK):
            cnt = jnp.sum(jnp.where(taken[off:off + CAND_PAD[a]] < RANK_NONE, 1.0, 0.0), axis=0, keepdims=True)
            m = jnp.where(rank0 == float(a), cnt, m)
            off += CAND_PAD[a]
        rank1_ref[hh] = rank1
        p1_ref[hh] = jnp.exp(s1 - vals1[0:1])
        m_ref[hh] = m
        p0_ref[hh] = jnp.exp(s0 - vals0[0:1]) / zsum
        return carry

    lax.fori_loop(0, PEER_HEADS, head, 0)
    if ring is not None:
        pl.when(pl.program_id(0) == n_steps - 1)(lambda: ring.drain(ring.n_chunks))


def _route(hn, wq, sk, tr, shift=None):
    t, d = hn.shape
    n_steps = t // tr
    table = jax.ShapeDtypeStruct((PEER_HEADS, PEER_NKEYS, t), F32)
    tspec = pl.BlockSpec((PEER_HEADS, PEER_NKEYS, tr), lambda i: (0, 0, i))
    full = lambda a: pl.BlockSpec(a.shape, lambda i: (0, 0))
    in_specs = [pl.BlockSpec((tr, d), lambda i: (i, 0)), full(wq), full(sk)]
    out_specs, out_shape = [tspec] * 4, [table] * 4
    scratch = [pltpu.VMEM((PEER_HEADS * 2 * PEER_NKEYS, tr), F32)]
    args = [hn, wq, sk]
    if shift is not None:
        c, nw = shift
        nseq = c.shape[0] // (n_steps * ROUTE_SHIFT_PER_STEP)
        assert nseq * n_steps * ROUTE_SHIFT_PER_STEP == c.shape[0]
        any_spec = pl.BlockSpec(memory_space=pl.ANY)
        in_specs += [any_spec, any_spec]
        out_specs = out_specs + [any_spec]
        out_shape = out_shape + [jax.ShapeDtypeStruct(c.shape, c.dtype)]
        dma_sems = pltpu.SemaphoreType.DMA((ROUTE_SHIFT_SLOTS,))
        scratch += [pltpu.VMEM((ROUTE_SHIFT_SLOTS, nseq) + tuple(c.shape[1:]), c.dtype), dma_sems, dma_sems, dma_sems]
        args += [c, nw]
    out = pl.pallas_call(
        functools.partial(_route_kernel, n_steps=n_steps, with_shift=shift is not None),
        grid=(n_steps,),
        in_specs=in_specs,
        out_specs=out_specs,
        out_shape=out_shape,
        scratch_shapes=scratch,
        compiler_params=_params(1),
        name="route",
    )(*args)
    return (out[:4], out[4]) if shift is not None else out


PEER_TE = 512
PEER_KEYS_PER_BLOCK = PEER_TE // PEER_NKEYS
PEER_TABLE_KEYS = 8
PEER_BLOCKS_PER_TABLE = PEER_TABLE_KEYS // PEER_KEYS_PER_BLOCK
assert PEER_BLOCKS_PER_TABLE == 2
SQRT_HALF = math.sqrt(0.5)


PEER_GATE_ROWS = 32


def _peer_gate_pieces(act_ref, w_ref, rank1_ref, p1_ref, m_ref, p0_ref, key_off):
    tm = act_ref.shape[1]

    def piece(ls, r0):
        gates = [jnp.zeros((PEER_GATE_ROWS, LANES), F32) for _ in range(PEER_KEYS_PER_BLOCK)]
        for hh in range(PEER_HEADS):
            r1 = rank1_ref[hh, r0:r0 + PEER_GATE_ROWS, ls]
            p1 = p1_ref[hh, r0:r0 + PEER_GATE_ROWS, ls]
            for j in range(PEER_KEYS_PER_BLOCK):
                kr = slice(key_off + j, key_off + j + 1)
                gates[j] = gates[j] + jnp.where(r1 < m_ref[hh, kr, ls], p1, 0.0) * p0_ref[hh, kr, ls]
        for j in range(PEER_KEYS_PER_BLOCK):
            rs = slice(j * PEER_NKEYS + r0, j * PEER_NKEYS + r0 + PEER_GATE_ROWS)
            a = act_ref[rs, ls]
            w_ref[rs, ls] = (gates[j] * (0.5 * a * (1.0 + lax.erf(a * SQRT_HALF)))).astype(BF16)

    return [functools.partial(piece, slice(c * LANES, (c + 1) * LANES), r0)
            for c in range(tm // LANES) for r0 in range(0, PEER_NKEYS, PEER_GATE_ROWS)]


PEER_DOT1_ROWS = 256
PEER_DOT2_ROWS = 512


def _interleave(*stages):
    tagged = [((i + 0.5) / len(st), si, piece) for si, st in enumerate(stages) for i, piece in enumerate(st)]
    return [piece for _, _, piece in sorted(tagged, key=lambda x: x[:2])]


def _peer_kernel(*refs, n_blocks, n_steps, with_shift):
    h_hbm, hn_ref, u_ref, vt_ref, rank1_ref, p1_ref, m_ref, p0_ref = refs[:8]
    g = pl.program_id(0)
    if with_shift:
        (c_ref, nw_ref, o_hbm, oc_ref, act0, act1, w0, w1, acc, obuf, sem_h, sem_o,
         buf, sem_body, sem_tail, sem_out) = refs[8:]
        ring = _ShiftRing(c_ref, nw_ref, oc_ref, buf, sem_body, sem_tail, sem_out)
        assert ring.n_chunks + ring.n_slots - 1 <= n_steps, "not enough grid steps to finish the shift"
        ring.step(g)
    else:
        o_hbm, act0, act1, w0, w1, acc, obuf, sem_h, sem_o = refs[8:]
    tm = obuf.shape[0]

    @pl.when(g == 0)
    def _():
        for ref in (act0, act1, w0, w1, acc):
            ref[...] = jnp.zeros(ref.shape, ref.dtype)

    blk3 = jnp.clip(g - 2, 0, n_steps - 3)
    tile = blk3 // n_blocks
    n_tiles = (n_steps - 2) // n_blocks
    first = (g >= 2) & (lax.rem(blk3, n_blocks) == 0)
    last = (g >= 2) & (lax.rem(blk3, n_blocks) == n_blocks - 1)
    tile_rows = lambda i: pl.ds(pl.multiple_of(i * tm, tm), tm)
    read_h = lambda i: pltpu.make_async_copy(h_hbm.at[tile_rows(i)], obuf, sem_h)
    write_o = lambda i: pltpu.make_async_copy(obuf, o_hbm.at[tile_rows(i)], sem_o)

    @pl.when(first)
    def _():
        pl.when(tile > 0)(lambda: write_o(jnp.maximum(tile - 1, 0)).wait())
        read_h(tile).start()

    def body(act_new, act_cur, w_new, w_old, key_off):
        def dot1(rows):
            act_new[rows, :] = lax.dot_general(u_ref[rows, :], hn_ref[...], NT_DIMS, preferred_element_type=F32)

        def dot2(rows):
            acc[rows, :] += jnp.dot(vt_ref[rows, :], w_old[...], preferred_element_type=F32)

        chunks = lambda n, size: [slice(r, r + size) for r in range(0, n, size)]
        for piece in _interleave(
                [functools.partial(dot1, rows) for rows in chunks(u_ref.shape[0], PEER_DOT1_ROWS)],
                _peer_gate_pieces(act_cur, w_new, rank1_ref, p1_ref, m_ref, p0_ref, key_off),
                [functools.partial(dot2, rows) for rows in chunks(vt_ref.shape[0], PEER_DOT2_ROWS)]):
            piece()

        @pl.when(last)
        def _():
            read_h(tile).wait()
            for c in range(tm // LANES):
                rows = slice(c * LANES, (c + 1) * LANES)
                obuf[rows, :] += acc[:, rows].T
            acc[...] = jnp.zeros(acc.shape, acc.dtype)
            write_o(tile).start()
            pl.when(tile == n_tiles - 1)(lambda: write_o(tile).wait())

    even = lax.rem(g, 2) == 0
    pl.when(even)(lambda: body(act0, act1, w1, w0, PEER_KEYS_PER_BLOCK))
    pl.when(jnp.logical_not(even))(lambda: body(act1, act0, w0, w1, 0))


PEER_SHIFT_SLOTS = 2


def _peer(h, hn, u, vt, tables, tm, shift=None):
    t, d = h.shape
    nb = u.shape[0] // PEER_TE
    assert nb % PEER_BLOCKS_PER_TABLE == 0
    n = (t // tm) * nb
    n_steps = n + 2
    blk1 = lambda g: jnp.minimum(g, n - 1)
    blk2 = lambda g: jnp.clip(g - 1, 0, n - 1)
    blk3 = lambda g: jnp.clip(g - 2, 0, n - 1)
    once = pl.Buffered(1)
    any_spec = pl.BlockSpec(memory_space=pl.ANY)
    tspec = pl.BlockSpec((PEER_HEADS, PEER_NKEYS, tm), lambda g: (0, 0, blk2(g) // nb), pipeline_mode=once)
    kspec = pl.BlockSpec((PEER_HEADS, PEER_TABLE_KEYS, tm),
                         lambda g: (0, (blk2(g) % nb) // PEER_BLOCKS_PER_TABLE, blk2(g) // nb))
    in_specs = [any_spec,
                pl.BlockSpec((tm, d), lambda g: (blk1(g) // nb, 0), pipeline_mode=once),
                pl.BlockSpec((PEER_TE, d), lambda g: (blk1(g) % nb, 0)),
                pl.BlockSpec((None, d, PEER_TE), lambda g: (blk3(g) % nb, 0, 0)), tspec, tspec, kspec, kspec]
    out_specs = [any_spec]
    out_shape = [jax.ShapeDtypeStruct((t, d), F32)]
    scratch = [pltpu.VMEM((PEER_TE, tm), F32), pltpu.VMEM((PEER_TE, tm), F32),
               pltpu.VMEM((PEER_TE, tm), BF16), pltpu.VMEM((PEER_TE, tm), BF16), pltpu.VMEM((d, tm), F32),
               pltpu.VMEM((tm, d), F32), pltpu.SemaphoreType.DMA(()), pltpu.SemaphoreType.DMA(())]
    args = [h, hn, u, vt, *tables]
    if shift is not None:
        c, nw = shift
        parts = (n_steps - PEER_SHIFT_SLOTS + 1) // c.shape[0]
        assert parts >= 1, "more sequences than grid steps"
        while c.shape[1] % parts:
            parts -= 1
        in_specs += [any_spec, any_spec]
        out_specs.append(any_spec)
        out_shape.append(jax.ShapeDtypeStruct(c.shape, c.dtype))
        dma_sems = pltpu.SemaphoreType.DMA((PEER_SHIFT_SLOTS,))
        scratch += [pltpu.VMEM((PEER_SHIFT_SLOTS, 1, c.shape[1] // parts) + tuple(c.shape[2:]), c.dtype),
                    dma_sems, dma_sems, dma_sems]
        args += [c, nw]
    out = pl.pallas_call(
        functools.partial(_peer_kernel, n_blocks=nb, n_steps=n_steps, with_shift=shift is not None),
        grid=(n_steps,),
        in_specs=in_specs,
        out_specs=out_specs,
        out_shape=out_shape,
        scratch_shapes=scratch,
        compiler_params=_params(1),
        name="peer",
    )(*args)
    return out if shift is not None else out[0]


def _alibi_slopes():
    return 2.0 ** (-8.0 * jnp.arange(1, N_ALIBI_HEADS + 1, dtype=F32) / N_ALIBI_HEADS)


def _subkey_matrix(subkeys):
    two, nk, dh = subkeys.shape
    eye = jnp.eye(PEER_HEADS * two, dtype=subkeys.dtype).reshape(PEER_HEADS, two, PEER_HEADS, two)
    sk = jnp.einsum("hcgb,cnd->hcngbd", eye, subkeys)
    return sk.reshape(PEER_HEADS * two * nk, PEER_HEADS * two * dh)


def _tail(x2d, oa, ob, zg, w, tm_block, tr, tm_peer, route_shift=None, peer_shift=None):
    h, hn = _block(x2d, oa, ob, zg, w["wa"], w["wb"], w["wo"], w["n2"], tm_block)
    routed = _route(hn, w["wq"], w["sk"], tr, route_shift)
    tables, shifted = routed if route_shift is not None else (routed, None)
    y = _peer(h, hn, w["u"], w["vt"], tables, tm_peer, peer_shift)
    if peer_shift is not None:
        return (y[0], shifted, y[1]) if route_shift is not None else y
    return (y, shifted) if route_shift is not None else y


def kernel(x_prompt, x_sample, cache_a_kv, cache_b1_kv, cache_b2_kv, cache_b3_kv, norm1_w, w_in, q_norm_a,
           k_norm_a, sink_a, q_norm_b, k_norm_b, w_branch_a, w_branch_b, w_out, norm2_w, peer_wq,
           peer_subkeys, peer_u, peer_v):
    assert norm1_w.shape[0] == 1, "single layer"
    b, s, d = x_prompt.shape
    n_dec = x_sample.shape[0]
    assert x_sample.shape[1] == 1

    slopes = _alibi_slopes()
    slopes_a, slopes_b = slopes[:A_Q_HEADS], slopes[A_Q_HEADS:]
    sink = sink_a[0].astype(F32)
    ones = jnp.ones((HEAD_DIM,), F32)
    col_w = jnp.concatenate(
        [jnp.tile(q_norm_a[0], A_Q_HEADS), jnp.tile(k_norm_a[0], A_KV_HEADS), jnp.tile(ones, A_KV_HEADS),
         jnp.tile(q_norm_b[0], B_HEADS), jnp.tile(k_norm_b[0], B_HEADS), jnp.tile(ones, B_HEADS)]
    ).astype(F32)[None]
    gate_w = jnp.ones((1, GATE_W), F32)
    n1 = norm1_w[0].astype(F32)[None]
    w_qkv = w_in[0, :, :QKV_W].astype(BF16)
    w_gate = w_in[0, :, QKV_W:].astype(BF16)
    w = dict(wa=w_branch_a[0].astype(BF16), wb=w_branch_b[0].astype(BF16), wo=w_out[0].astype(BF16),
             n2=norm2_w[0].astype(F32)[None], wq=peer_wq[0].astype(BF16),
             sk=_subkey_matrix(peer_subkeys[0]).astype(BF16),
             u=peer_u[0].astype(BF16),
             vt=peer_v[0].astype(BF16).reshape(PEER_EXPERTS // PEER_TE, PEER_TE, d).transpose(0, 2, 1))

    xs = x_sample.reshape(n_dec, d)
    zs = _proj(xs, n1, w_qkv, col_w, QKV_KINDS, F32, n_dec, "proj_qkv_s")
    zgs = _proj(xs, n1, w_gate, gate_w, GATE_KINDS, BF16, n_dec, "proj_gate_s")
    zs3 = zs.reshape(n_dec, N_QKV_HEADS, HEAD_DIM)
    caches = (cache_a_kv[0], cache_b1_kv[0], cache_b2_kv[0], cache_b3_kv[0])
    news = [jnp.stack([zs3[:, HEAD_KA:HEAD_KA + A_KV_HEADS], zs3[:, HEAD_VA:HEAD_VA + A_KV_HEADS]], axis=1)]
    for g in range(N_B_GROUPS):
        o4 = g * B_HEADS_PER_GROUP
        news.append(jnp.stack([zs3[:, HEAD_KB + o4:HEAD_KB + o4 + B_HEADS_PER_GROUP],
                               zs3[:, HEAD_VB + o4:HEAD_VB + o4 + B_HEADS_PER_GROUP]], axis=1))

    xp = x_prompt.reshape(b * s, d)
    zq = _proj(xp, n1, w_qkv, col_w, QKV_KINDS, F32, 1024, "proj_qkv")
    zg = _proj(xp, n1, w_gate, gate_w, GATE_KINDS, BF16, 1024, "proj_gate")
    zq3 = zq.reshape(b, s, QKV_W)
    oa = _attn_a(zq3, slopes_a, sink).reshape(b * s, A_Q_HEADS * HEAD_DIM)
    ob = _attn_b(zq3, slopes_b).reshape(b * s, B_HEADS_PER_GROUP * HEAD_DIM)
    y_prompt, new_b2, new_b3 = _tail(xp, oa, ob, zg, w, 256, 256, 1024, route_shift=(caches[2], news[2]),
                                     peer_shift=(caches[3], news[3]))
    y_prompt = y_prompt.reshape(b, s, d)

    def window(k0, v0, nh, length):
        part = lambda h0: zq3[:, s - length:, h0 * HEAD_DIM:(h0 + nh) * HEAD_DIM].reshape(b, length, nh, HEAD_DIM)
        return jnp.stack([part(k0), part(v0)], axis=2)[None]

    kv_prompt = [window(HEAD_KA, HEAD_VA, A_KV_HEADS, min(A_WINDOW, s))]
    for g, (win, _) in enumerate(B_GROUPS):
        o4 = g * B_HEADS_PER_GROUP
        kv_prompt.append(window(HEAD_KB + o4, HEAD_VB + o4, B_HEADS_PER_GROUP, min(win, s)))

    oa_s, ob_s = _sample_attn(zs3, *caches, slopes_a, sink, slopes_b)
    y_sample = _tail(xs, oa_s.astype(BF16), ob_s.astype(BF16), zgs, w, n_dec, n_dec, n_dec).reshape(n_dec, 1, d)

    kv_sample = [o[None] for o in (*_cache_update(caches[:2], news[:2]), new_b2, new_b3)]

    return (y_prompt, y_sample, *kv_prompt, *kv_sample)
```

```python
import functools
import math

import jax
import jax.numpy as jnp
from jax import lax
from jax.experimental import pallas as pl
from jax.experimental.pallas import tpu as pltpu

F32 = jnp.float32
BF16 = jnp.bfloat16

D_MODEL = 2048
HEAD_DIM = 128
A_Q_HEADS = 8
A_KV_HEADS = 2
A_GQA = A_Q_HEADS // A_KV_HEADS
A_WINDOW = 128
B_GROUPS = ((128, 1), (512, 4), (2048, 16))
B_HEADS_PER_GROUP = 4
N_B_GROUPS = len(B_GROUPS)
B_HEADS = N_B_GROUPS * B_HEADS_PER_GROUP
BAND = 128
N_ALIBI_HEADS = A_Q_HEADS + B_HEADS
ATTN_SCALE = HEAD_DIM ** -0.5
PEER_HEADS = 8
PEER_NKEYS = 128
PEER_EXPERTS = PEER_NKEYS * PEER_NKEYS
PEER_DKEY = 128
PEER_TOPK = 16
NORM_EPS = 1e-6
NEG_INF = -1e30

QKV_W = (A_Q_HEADS + 2 * A_KV_HEADS + 3 * B_HEADS) * HEAD_DIM
GATE_W = 2 * D_MODEL
HEAD_QA, HEAD_KA, HEAD_VA = 0, A_Q_HEADS, A_Q_HEADS + A_KV_HEADS
HEAD_QB = A_Q_HEADS + 2 * A_KV_HEADS
HEAD_KVB = HEAD_QB + B_HEADS
N_QKV_HEADS = QKV_W // HEAD_DIM


def _head_kb(g):
    return HEAD_KVB + 2 * B_HEADS_PER_GROUP * g

VMEM_LIMIT_BYTES = 56 * 1024 * 1024
LANES = 128

NT_DIMS = (((1,), (1,)), ((), ()))
TN_DIMS = (((0,), (0,)), ((), ()))


def _params(n_grid_axes):
    return pltpu.CompilerParams(
        dimension_semantics=("arbitrary",) * n_grid_axes,
        vmem_limit_bytes=VMEM_LIMIT_BYTES)


def _rms(x):
    return x * lax.rsqrt(jnp.mean(x * x, axis=-1, keepdims=True) + NORM_EPS)


PROJ_TN = 512
PROJ_HEADS_PER_TILE = PROJ_TN // HEAD_DIM


def _proj_kernel(x_ref, n1_ref, w_ref, cw_ref, o_ref, xn_ref, *, kinds):
    j = pl.program_id(1)

    @pl.when(j == 0)
    def _():
        xn_ref[...] = (_rms(x_ref[...]) * n1_ref[...]).astype(BF16)

    z = jnp.dot(xn_ref[...], w_ref[...], preferred_element_type=F32)

    def epilogue(head_is_normed):
        for hh, normed in enumerate(head_is_normed):
            cs = slice(hh * HEAD_DIM, (hh + 1) * HEAD_DIM)
            zh = z[:, cs]
            if normed:
                zh = _rms(zh) * cw_ref[:, cs]
            o_ref[:, cs] = zh.astype(o_ref.dtype)

    for kind in sorted(set(kinds)):
        cond = functools.reduce(jnp.logical_or, [j == jj for jj, k in enumerate(kinds) if k == kind])
        if kind == "gate":
            @pl.when(cond)
            def _():
                o_ref[...] = jax.nn.sigmoid(z).astype(o_ref.dtype)
        else:
            @pl.when(cond)
            def _(kind=kind):
                epilogue(kind)


def _proj(x2d, n1, w, cw, kinds, out_dtype, tm, name):
    t, d = x2d.shape
    n = w.shape[1]
    assert t % tm == 0 and n == PROJ_TN * len(kinds)
    return pl.pallas_call(
        functools.partial(_proj_kernel, kinds=kinds),
        grid=(t // tm, len(kinds)),
        in_specs=[
            pl.BlockSpec((tm, d), lambda i, j: (i, 0)),
            pl.BlockSpec((1, d), lambda i, j: (0, 0)),
            pl.BlockSpec((d, PROJ_TN), lambda i, j: (0, j)),
            pl.BlockSpec((1, PROJ_TN), lambda i, j: (0, j)),
        ],
        out_specs=pl.BlockSpec((tm, PROJ_TN), lambda i, j: (i, j)),
        out_shape=jax.ShapeDtypeStruct((t, n), out_dtype),
        scratch_shapes=[pltpu.VMEM((tm, d), BF16)],
        compiler_params=_params(2),
        name=name,
    )(x2d, n1, w, cw)


_N4, _I4 = (True,) * 4, (False,) * 4
QKV_KINDS = (_N4, _N4, (True, True, False, False)) + (_N4,) * 3 + (_N4, _I4) * N_B_GROUPS
GATE_KINDS = ("gate",) * (GATE_W // PROJ_TN)


def _band_softmax(q, k, v, slope_dist, mask, sink):
    s = lax.dot_general(q, k, NT_DIMS, preferred_element_type=F32) * ATTN_SCALE
    logits = jnp.where(mask, s - slope_dist, NEG_INF)
    m = jnp.max(logits, axis=-1, keepdims=True)
    if sink is not None:
        m = jnp.maximum(m, sink)
    p = jnp.exp(logits - m)
    denom = jnp.sum(p, axis=-1, keepdims=True)
    if sink is not None:
        denom = denom + jnp.exp(sink - m)
    o = jnp.dot(p.astype(BF16), v, preferred_element_type=F32) / denom
    return o, m + jnp.log(denom)


def _band_geometry(has_prev):
    nk = 2 * BAND if has_prev else BAND
    qi = lax.broadcasted_iota(jnp.int32, (BAND, nk), 0)
    sj = lax.broadcasted_iota(jnp.int32, (BAND, nk), 1)
    dist = qi - sj + (BAND if has_prev else 0)
    return dist, sj


def _attn_a_kernel(slopes_ref, sink_ref, q_ref, k_ref, v_ref, o_ref):
    kvh = pl.program_id(1)
    n_blocks = q_ref.shape[1] // BAND
    dist, sj = _band_geometry(True)
    in_window = (dist >= 0) & (dist <= A_WINDOW)
    distf = dist.astype(F32)

    def block(blk, carry):
        cur = pl.multiple_of(blk * BAND, BAND)
        prev = pl.multiple_of(jnp.maximum(blk - 1, 0) * BAND, BAND)
        k = jnp.concatenate([k_ref[0, pl.ds(prev, BAND), :], k_ref[0, pl.ds(cur, BAND), :]], axis=0).astype(BF16)
        v = jnp.concatenate([v_ref[0, pl.ds(prev, BAND), :], v_ref[0, pl.ds(cur, BAND), :]], axis=0).astype(BF16)
        mask = in_window & (sj >= jnp.where(blk > 0, 0, BAND))
        for g in range(A_GQA):
            cs = slice(g * HEAD_DIM, (g + 1) * HEAD_DIM)
            head = kvh * A_GQA + g
            q = q_ref[0, pl.ds(cur, BAND), cs].astype(BF16)
            o, _ = _band_softmax(q, k, v, slopes_ref[head] * distf, mask, sink_ref[head])
            o_ref[0, pl.ds(cur, BAND), cs] = o.astype(o_ref.dtype)
        return carry

    lax.fori_loop(0, n_blocks, block, 0)


def _attn_a(zq3, slopes_a, sink):
    b, s, _ = zq3.shape
    gw = A_GQA * HEAD_DIM
    smem = pl.BlockSpec(memory_space=pltpu.SMEM)
    return pl.pallas_call(
        _attn_a_kernel,
        grid=(b, A_KV_HEADS),
        in_specs=[
            smem, smem,
            pl.BlockSpec((1, s, gw), lambda bi, h: (bi, 0, h)),
            pl.BlockSpec((1, s, HEAD_DIM), lambda bi, h: (bi, 0, HEAD_KA + h)),
            pl.BlockSpec((1, s, HEAD_DIM), lambda bi, h: (bi, 0, HEAD_VA + h)),
        ],
        out_specs=pl.BlockSpec((1, s, gw), lambda bi, h: (bi, 0, h)),
        out_shape=jax.ShapeDtypeStruct((b, s, A_Q_HEADS * HEAD_DIM), BF16),
        compiler_params=_params(2),
        name="attn_a",
    )(slopes_a, sink, zq3, zq3, zq3)


def _attn_b_kernel(slopes_ref, *refs):
    q_refs, k_refs, v_refs = refs[0:3], refs[3:6], refs[6:9]
    o_ref, og_ref, lse_ref = refs[9:12]
    h = pl.program_id(1)
    s_len = o_ref.shape[1]

    for g, (win, dil) in enumerate(B_GROUPS):
        n_blocks = s_len // (BAND * dil)
        slope = slopes_ref[g * B_HEADS_PER_GROUP + h]
        for has_prev in (False, True):
            dist, _ = _band_geometry(has_prev)
            mask = (dist >= 0) & (dist <= win // dil)
            slope_dist = slope * (dil * dist).astype(F32)
            for blk in range(n_blocks):
                if (blk > 0) != has_prev:
                    continue
                for r in range(dil):
                    def rows(ref, b0):
                        start = BAND * dil * b0 + r
                        if dil == 1:
                            return ref[0, pl.ds(start, BAND), :]
                        return ref[0, pl.ds(start, BAND, stride=dil), :]

                    q = rows(q_refs[g], blk).astype(BF16)
                    k = rows(k_refs[g], blk)
                    v = rows(v_refs[g], blk)
                    if has_prev:
                        k = jnp.concatenate([rows(k_refs[g], blk - 1), k], axis=0)
                        v = jnp.concatenate([rows(v_refs[g], blk - 1), v], axis=0)
                    o, lse = _band_softmax(q, k.astype(BF16), v.astype(BF16), slope_dist, mask, None)
                    start = BAND * dil * blk + r
                    idx = pl.ds(start, BAND) if dil == 1 else pl.ds(start, BAND, stride=dil)
                    og_ref[g, idx, :] = o
                    lse_ref[g, idx, :] = jnp.broadcast_to(lse, (BAND, HEAD_DIM))

    def combine(c, carry):
        rs = pl.ds(pl.multiple_of(c * BAND, BAND), BAND)
        lses = [lse_ref[g, rs, :] for g in range(N_B_GROUPS)]
        mx = functools.reduce(jnp.maximum, lses)
        ws = [jnp.exp(l - mx) for l in lses]
        acc = functools.reduce(lambda a, b_: a + b_, [w * og_ref[g, rs, :] for g, w in enumerate(ws)])
        o_ref[0, rs, :] = (acc / functools.reduce(lambda a, b_: a + b_, ws)).astype(o_ref.dtype)
        return carry

    lax.fori_loop(0, s_len // BAND, combine, 0)


def _attn_b(zq3, slopes_b):
    b, s, _ = zq3.shape
    smem = pl.BlockSpec(memory_space=pltpu.SMEM)

    def head_spec(first_head):
        return [pl.BlockSpec((1, s, HEAD_DIM), lambda bi, h, c=first_head(g): (bi, 0, c + h))
                for g in range(N_B_GROUPS)]

    return pl.pallas_call(
        _attn_b_kernel,
        grid=(b, B_HEADS_PER_GROUP),
        in_specs=[smem] + head_spec(lambda g: HEAD_QB + g * B_HEADS_PER_GROUP) + head_spec(_head_kb)
        + head_spec(lambda g: _head_kb(g) + B_HEADS_PER_GROUP),
        out_specs=pl.BlockSpec((1, s, HEAD_DIM), lambda bi, h: (bi, 0, h)),
        out_shape=jax.ShapeDtypeStruct((b, s, B_HEADS_PER_GROUP * HEAD_DIM), BF16),
        scratch_shapes=[pltpu.VMEM((N_B_GROUPS, s, HEAD_DIM), F32),
                        pltpu.VMEM((N_B_GROUPS, s, HEAD_DIM), F32)],
        compiler_params=_params(2),
        name="attn_b",
    )(slopes_b, *([zq3] * 9))


SAMPLE_BS = 4
SUBLANES = 8
MASKED = 1e30


def _window_read(q8, kv8, c, bias, bias_new, sink8, hk):
    per_tile = SUBLANES // (2 * hk)
    row = lax.broadcasted_iota(jnp.int32, (1, 1, SUBLANES, HEAD_DIM), 2)
    is_key = lax.rem(row, 2 * hk) < hk
    first_key = row < hk
    first_value = (row >= hk) & (row < 2 * hk)
    roll = lambda x, shift: pltpu.roll(x, shift, 2)

    def over_positions(x, op):
        for i in range(1, per_tile):
            x = op(x, roll(x, i * 2 * hk))
        return x

    q8 = q8 * ATTN_SCALE
    logits = jnp.sum(c * q8, axis=-1, keepdims=True) - bias
    logit_new = jnp.sum(kv8 * q8, axis=-1, keepdims=True) - bias_new
    mx = over_positions(jnp.max(logits, axis=1, keepdims=True), jnp.maximum)
    mx = jnp.maximum(mx, logit_new)
    if sink8 is not None:
        mx = jnp.maximum(mx, sink8)
    mx = jnp.where(is_key, mx, 0.0)
    p = jnp.exp(logits - mx)
    p_new = jnp.where(first_key, jnp.exp(logit_new - mx), 0.0)
    den = over_positions(jnp.sum(p, axis=1, keepdims=True), jnp.add) + p_new
    if sink8 is not None:
        den = den + jnp.where(first_key, jnp.exp(sink8 - mx), 0.0)
    acc = over_positions(jnp.sum(roll(p, hk) * c, axis=1, keepdims=True), jnp.add) + roll(p_new, hk) * kv8
    o8 = acc / jnp.where(first_value, roll(den, hk), 1.0)
    lse8 = roll(mx + jnp.log(jnp.where(first_key, den, 1.0)), hk)
    return o8, lse8


def _sample_attn_kernel(qa_ref, kva_ref, qb_ref, kvb_ref, ca_ref, cb1_ref, cb2_ref, cb3_ref,
                        bias_a_ref, new_a_ref, sink_ref, bias_b_ref, new_b_ref, oa_ref, ob_ref):
    kva = kva_ref[...][:, None]
    ca = ca_ref[...]
    for g in range(A_GQA):
        o8, _ = _window_read(qa_ref[:, g][:, None], kva, ca, bias_a_ref[g], new_a_ref[...], sink_ref[g], A_KV_HEADS)
        oa_ref[:, g] = o8[:, 0]
    outs, lses = [], []
    for g, c_ref in enumerate((cb1_ref, cb2_ref, cb3_ref)):
        o8, lse8 = _window_read(qb_ref[:, g][:, None], kvb_ref[:, g][:, None], c_ref[...], bias_b_ref[g],
                                new_b_ref[...], None, B_HEADS_PER_GROUP)
        outs.append(o8[:, 0])
        lses.append(lse8[:, 0])
    mx = functools.reduce(jnp.maximum, lses)
    ws = [jnp.exp(l - mx) for l in lses]
    acc = functools.reduce(lambda a, b_: a + b_, [w * o for w, o in zip(ws, outs)])
    ob_ref[...] = acc / functools.reduce(lambda a, b_: a + b_, ws)


def _tile_rows(x, hk):
    per_tile = SUBLANES // (2 * hk)
    pos = jnp.concatenate([x, jnp.zeros_like(x)], axis=-2)
    return jnp.concatenate([pos] * per_tile, axis=-2)


def _key_row_bias(dist, slopes, hk):
    key = dist[:, :, None] * slopes[None, None, :]
    rows = jnp.concatenate([key, jnp.full_like(key, MASKED)], axis=-1).reshape(dist.shape[0], SUBLANES)
    return jnp.broadcast_to(rows[:, :, None], rows.shape + (HEAD_DIM,))


def _sample_attn(zs3, cache_a, cache_b1, cache_b2, cache_b3, slopes_a, sink, slopes_b):
    n = zs3.shape[0]
    bs = SAMPLE_BS
    f32 = lambda x: x.astype(F32)

    def tiles(c, dil):
        nb, l, two, hh, hd = c.shape
        assert l == A_WINDOW * dil
        per_tile = SUBLANES // (two * hh)
        view = c.reshape(nb, l // (dil * per_tile), dil * SUBLANES, hd)
        return view, pl.BlockSpec((bs, view.shape[1], SUBLANES, hd), lambda i: (i, 0, 0, 0))

    ca, ca_spec = tiles(cache_a, 1)
    cb = [tiles(c, dil) for c, (_, dil) in zip((cache_b1, cache_b2, cache_b3), B_GROUPS)]

    heads_a = lambda h0: zs3[:, h0:h0 + A_KV_HEADS]
    qa = zs3[:, HEAD_QA:HEAD_QA + A_Q_HEADS].reshape(n, A_KV_HEADS, A_GQA, HEAD_DIM).transpose(0, 2, 1, 3)
    qa8 = _tile_rows(qa, A_KV_HEADS)
    kva8 = jnp.concatenate([heads_a(HEAD_KA), heads_a(HEAD_VA)] * (SUBLANES // (2 * A_KV_HEADS)), axis=1)
    qb = zs3[:, HEAD_QB:HEAD_QB + B_HEADS].reshape(n, N_B_GROUPS, B_HEADS_PER_GROUP, HEAD_DIM)
    qb8 = _tile_rows(qb, B_HEADS_PER_GROUP)
    kvb8 = zs3[:, HEAD_KVB:].reshape(n, N_B_GROUPS, 2 * B_HEADS_PER_GROUP, HEAD_DIM)

    per_a = SUBLANES // (2 * A_KV_HEADS)
    dist_a = (A_WINDOW - jnp.arange(A_WINDOW, dtype=F32)).reshape(A_WINDOW // per_a, per_a)
    slopes_ag = f32(slopes_a).reshape(A_KV_HEADS, A_GQA).T
    bias_a = jnp.stack([_key_row_bias(dist_a, slopes_ag[g], A_KV_HEADS) for g in range(A_GQA)])
    new_a = _key_row_bias(jnp.zeros((1, per_a), F32), slopes_ag[0], A_KV_HEADS)[0]
    sink8 = _tile_rows(f32(sink).reshape(A_KV_HEADS, A_GQA).T[:, :, None], A_KV_HEADS)
    key_rows = new_a[None, :, :1] == 0.0
    sink8 = jnp.broadcast_to(jnp.where(key_rows, sink8, -MASKED), (A_GQA, SUBLANES, HEAD_DIM))
    slopes_bg = f32(slopes_b).reshape(N_B_GROUPS, B_HEADS_PER_GROUP)
    dist_b = lambda dil: ((A_WINDOW - jnp.arange(A_WINDOW, dtype=F32)) * dil)[:, None]
    bias_b = jnp.stack([_key_row_bias(dist_b(dil), slopes_bg[g], B_HEADS_PER_GROUP)
                        for g, (_, dil) in enumerate(B_GROUPS)])
    new_b = _key_row_bias(jnp.zeros((1, 1), F32), slopes_bg[0], B_HEADS_PER_GROUP)[0]

    full = lambda a: pl.BlockSpec(a.shape, lambda i: (0,) * a.ndim)
    per_seq = lambda a: pl.BlockSpec((bs,) + a.shape[1:], lambda i: (i,) + (0,) * (a.ndim - 1))
    oa8, ob8 = pl.pallas_call(
        _sample_attn_kernel,
        grid=(n // bs,),
        in_specs=[per_seq(qa8), per_seq(kva8), per_seq(qb8), per_seq(kvb8),
                  ca_spec, cb[0][1], cb[1][1], cb[2][1],
                  full(bias_a), full(new_a), full(sink8), full(bias_b), full(new_b)],
        out_specs=[pl.BlockSpec((bs, A_GQA, SUBLANES, HEAD_DIM), lambda i: (i, 0, 0, 0)),
                   pl.BlockSpec((bs, SUBLANES, HEAD_DIM), lambda i: (i, 0, 0))],
        out_shape=[jax.ShapeDtypeStruct((n, A_GQA, SUBLANES, HEAD_DIM), F32),
                   jax.ShapeDtypeStruct((n, SUBLANES, HEAD_DIM), F32)],
        compiler_params=_params(1),
        name="sample_attn",
    )(qa8, kva8, qb8, kvb8, ca, cb[0][0], cb[1][0], cb[2][0], bias_a, new_a, sink8, bias_b, new_b)
    oa = oa8[:, :, A_KV_HEADS:2 * A_KV_HEADS].transpose(0, 2, 1, 3).reshape(n, A_Q_HEADS * HEAD_DIM)
    ob = ob8[:, B_HEADS_PER_GROUP:].reshape(n, B_HEADS_PER_GROUP * HEAD_DIM)
    return oa, ob


CACHE_SLOTS = 3
CACHE_CHUNK_BYTES = 4 * 1024 * 1024


class _ShiftRing:
    def __init__(self, c, nw, o, buf, sem_body, sem_tail, sem_out):
        self.c, self.nw, self.o, self.buf = c, nw, o, buf
        self.sem_body, self.sem_tail, self.sem_out = sem_body, sem_tail, sem_out
        self.n_slots, self.nseq, self.rows = buf.shape[:3]
        n, self.l = c.shape[0], c.shape[1]
        assert n % self.nseq == 0 and self.l % self.rows == 0
        self.parts = self.l // self.rows
        self.n_chunks = (n // self.nseq) * self.parts

    def _where(self, k):
        return k % self.n_slots, pl.ds((k // self.parts) * self.nseq, self.nseq), (k % self.parts) * self.rows

    def _body(self, k):
        slot, seqs, r0 = self._where(k)
        return pltpu.make_async_copy(self.c.at[seqs, pl.ds(r0 + 1, self.rows - 1)],
                                     self.buf.at[slot, :, pl.ds(0, self.rows - 1)], self.sem_body.at[slot])

    def _tail_old(self, k):
        slot, seqs, r0 = self._where(k)
        return pltpu.make_async_copy(self.c.at[seqs, pl.ds(jnp.minimum(r0 + self.rows, self.l - 1), 1)],
                                     self.buf.at[slot, :, pl.ds(self.rows - 1, 1)], self.sem_tail.at[slot])

    def _tail_new(self, k):
        slot, seqs, _ = self._where(k)
        return pltpu.make_async_copy(self.nw.at[seqs], self.buf.at[slot, :, self.rows - 1], self.sem_tail.at[slot])

    def write(self, k):
        slot, seqs, r0 = self._where(k)
        return pltpu.make_async_copy(self.buf.at[slot], self.o.at[seqs, pl.ds(r0, self.rows)], self.sem_out.at[slot])

    def _read(self, k, action):
        action(self._body(k))
        if self.parts == 1:
            action(self._tail_new(k))
        else:
            is_last = (k % self.parts) == self.parts - 1
            pl.when(is_last)(lambda: action(self._tail_new(k)))
            pl.when(jnp.logical_not(is_last))(lambda: action(self._tail_old(k)))

    def start_read(self, k):
        self._read(k, lambda cp: cp.start())

    def wait_read(self, k):
        self._read(k, lambda cp: cp.wait())

    def step(self, k):
        n = self.n_chunks
        written = k + 1 - self.n_slots
        pl.when(k == 0)(lambda: self.start_read(0))
        pl.when((written >= 0) & (written < n))(lambda: self.write(jnp.clip(written, 0, n - 1)).wait())
        pl.when(k + 1 < n)(lambda: self.start_read(jnp.minimum(k + 1, n - 1)))

        @pl.when(k < n)
        def _():
            self.wait_read(jnp.minimum(k, n - 1))
            self.write(jnp.minimum(k, n - 1)).start()

    def drain(self, done_steps):
        for k in range(max(done_steps - self.n_slots + 1, 0), self.n_chunks):
            self.write(k).wait()


def _cache_ring(c, nw, o, buf, sem_body, sem_tail, sem_out):
    ring = _ShiftRing(c, nw, o, buf, sem_body, sem_tail, sem_out)

    def step(k, carry):
        ring.step(k)
        return carry

    lax.fori_loop(0, ring.n_chunks, step, 0)
    ring.drain(ring.n_chunks)


def _cache_chunk(c):
    n, l = c.shape[0], c.shape[1]
    seq_bytes = math.prod(c.shape[1:]) * c.dtype.itemsize
    if seq_bytes <= CACHE_CHUNK_BYTES:
        return min(n, CACHE_CHUNK_BYTES // seq_bytes), l
    return 1, l // (seq_bytes // CACHE_CHUNK_BYTES)


def _cache_kernel(*refs):
    n = len(refs) // 3
    caches, news, outs = refs[:n], refs[n:2 * n], refs[2 * n:]
    for c, nw, o in zip(caches, news, outs):
        nseq, rows = _cache_chunk(c)
        dma_sems = pltpu.SemaphoreType.DMA((CACHE_SLOTS,))
        pl.run_scoped(
            functools.partial(_cache_ring, c, nw, o),
            pltpu.VMEM((CACHE_SLOTS, nseq, rows) + tuple(c.shape[2:]), c.dtype), dma_sems, dma_sems, dma_sems)


def _cache_update(caches, news):
    n = len(caches)
    any_spec = pl.BlockSpec(memory_space=pl.ANY)
    return pl.pallas_call(
        _cache_kernel,
        in_specs=[any_spec] * (2 * n),
        out_specs=[any_spec] * n,
        out_shape=[jax.ShapeDtypeStruct(c.shape, c.dtype) for c in caches],
        compiler_params=pltpu.CompilerParams(vmem_limit_bytes=VMEM_LIMIT_BYTES),
        name="cache_update",
    )(*caches, *news)


def _block_kernel(x_ref, oa_ref, ob_ref, ga_ref, gb_ref, wa_ref, wb_ref, wo_ref, n2_ref, h_ref, hn_ref):
    ya = jnp.dot(oa_ref[...], wa_ref[...], preferred_element_type=F32)
    yb = jnp.dot(ob_ref[...], wb_ref[...], preferred_element_type=F32)
    mix = (ga_ref[...].astype(F32) * ya + gb_ref[...].astype(F32) * yb).astype(BF16)
    h = x_ref[...] + jnp.dot(mix, wo_ref[...], preferred_element_type=F32)
    h_ref[...] = h
    hn_ref[...] = (_rms(h) * n2_ref[...]).astype(BF16)


def _block(x2d, oa, ob, zg, wa, wb, wo, n2, tm):
    t, d = x2d.shape
    row = lambda w: pl.BlockSpec((tm, w), lambda i: (i, 0))
    full = lambda a: pl.BlockSpec(a.shape, lambda i: (0, 0))
    return pl.pallas_call(
        _block_kernel,
        grid=(t // tm,),
        in_specs=[row(d), row(oa.shape[1]), row(ob.shape[1]),
                  pl.BlockSpec((tm, d), lambda i: (i, 0)), pl.BlockSpec((tm, d), lambda i: (i, 1)),
                  full(wa), full(wb), full(wo), full(n2)],
        out_specs=[row(d), row(d)],
        out_shape=[jax.ShapeDtypeStruct((t, d), F32), jax.ShapeDtypeStruct((t, d), BF16)],
        compiler_params=_params(1),
        name="block",
    )(x2d, oa, ob, zg, zg, wa, wb, wo, n2)


RANK_NONE = float(PEER_NKEYS)
CAND_ROWS = tuple(PEER_TOPK // (a + 1) for a in range(PEER_TOPK))
CAND_PAD = tuple(-(-r // 8) * 8 for r in CAND_ROWS)


def _extract(work, n_take, break_ties):
    rows = lax.broadcasted_iota(jnp.int32, work.shape, 0).astype(F32)
    rank = jnp.full(work.shape, RANK_NONE, F32)
    vals = []
    for kk in range(n_take):
        mx = jnp.max(work, axis=0, keepdims=True)
        sel = work == mx
        if break_ties:
            first = jnp.min(jnp.where(sel, rows, float(work.shape[0])), axis=0, keepdims=True)
            sel = rows == first
        rank = jnp.where(sel, float(kk), rank)
        work = jnp.where(sel, -jnp.inf, work)
        vals.append(mx)
    return rank, jnp.concatenate(vals, axis=0)


def _extract_top(work, n_take):
    rank, vals = _extract(work, n_take, break_ties=False)
    ranked = jnp.sum(jnp.where(rank < RANK_NONE, 1.0, 0.0), axis=0, keepdims=True)
    return lax.cond(jnp.max(ranked) > n_take,
                    lambda: _extract(work, n_take, break_ties=True), lambda: (rank, vals))


ROUTE_SHIFT_SLOTS = 2
ROUTE_SHIFT_PER_STEP = 2


def _route_kernel(*refs, n_steps, with_shift):
    hn_ref, wq_ref, sk_ref = refs[:3]
    if with_shift:
        c_ref, nw_ref, rank1_ref, p1_ref, m_ref, p0_ref, oc_ref, s_scr, buf, sem_body, sem_tail, sem_out = refs[3:]
        ring = _ShiftRing(c_ref, nw_ref, oc_ref, buf, sem_body, sem_tail, sem_out)
        assert ring.n_chunks == n_steps * ROUTE_SHIFT_PER_STEP and PEER_HEADS % ROUTE_SHIFT_PER_STEP == 0
        first_chunk = pl.program_id(0) * ROUTE_SHIFT_PER_STEP
    else:
        rank1_ref, p1_ref, m_ref, p0_ref, s_scr = refs[3:]
        ring = None
    q = jnp.dot(hn_ref[...], wq_ref[...], preferred_element_type=F32).astype(BF16)
    s_scr[...] = lax.dot_general(sk_ref[...], q, NT_DIMS, preferred_element_type=F32)

    def head(hh, carry):
        if ring is not None:
            every = PEER_HEADS // ROUTE_SHIFT_PER_STEP
            pl.when(lax.rem(hh, every) == 0)(lambda: ring.step(first_chunk + hh // every))
        base = pl.multiple_of(hh * 2 * PEER_NKEYS, 2 * PEER_NKEYS)
        s0 = s_scr[pl.ds(base, PEER_NKEYS), :]
        s1 = s_scr[pl.ds(base + PEER_NKEYS, PEER_NKEYS), :]
        rank0, vals0 = _extract_top(s0, PEER_TOPK)
        rank1, vals1 = _extract_top(s1, PEER_TOPK)
        cand = []
        for a in range(PEER_TOPK):
            blk = vals0[a:a + 1] + vals1[:CAND_PAD[a]]
            rr = lax.broadcasted_iota(jnp.int32, blk.shape, 0)
            cand.append(jnp.where(rr < CAND_ROWS[a], blk, -jnp.inf))
        taken, best = _extract_top(jnp.concatenate(cand, axis=0), PEER_TOPK)
        zsum = jnp.sum(jnp.exp(best - best[0:1]), axis=0, keepdims=True)
        m = jnp.zeros_like(s0)
        off = 0
        for a in range(PEER_TOPK):
            cnt = jnp.sum(jnp.where(taken[off:off + CAND_PAD[a]] < RANK_NONE, 1.0, 0.0), axis=0, keepdims=True)
            m = jnp.where(rank0 == float(a), cnt, m)
            off += CAND_PAD[a]
        rank1_ref[hh] = rank1
        p1_ref[hh] = jnp.exp(s1 - vals1[0:1])
        m_ref[hh] = m
        p0_ref[hh] = jnp.exp(s0 - vals0[0:1]) / zsum
        return carry

    lax.fori_loop(0, PEER_HEADS, head, 0)
    if ring is not None:
        pl.when(pl.program_id(0) == n_steps - 1)(lambda: ring.drain(ring.n_chunks))


def _route(hn, wq, sk, tr, shift=None):
    t, d = hn.shape
    n_steps = t // tr
    table = jax.ShapeDtypeStruct((PEER_HEADS, PEER_NKEYS, t), F32)
    tspec = pl.BlockSpec((PEER_HEADS, PEER_NKEYS, tr), lambda i: (0, 0, i))
    full = lambda a: pl.BlockSpec(a.shape, lambda i: (0, 0))
    in_specs = [pl.BlockSpec((tr, d), lambda i: (i, 0)), full(wq), full(sk)]
    out_specs, out_shape = [tspec] * 4, [table] * 4
    scratch = [pltpu.VMEM((PEER_HEADS * 2 * PEER_NKEYS, tr), F32)]
    args = [hn, wq, sk]
    if shift is not None:
        c, nw = shift
        nseq = c.shape[0] // (n_steps * ROUTE_SHIFT_PER_STEP)
        assert nseq * n_steps * ROUTE_SHIFT_PER_STEP == c.shape[0]
        any_spec = pl.BlockSpec(memory_space=pl.ANY)
        in_specs += [any_spec, any_spec]
        out_specs = out_specs + [any_spec]
        out_shape = out_shape + [jax.ShapeDtypeStruct(c.shape, c.dtype)]
        dma_sems = pltpu.SemaphoreType.DMA((ROUTE_SHIFT_SLOTS,))
        scratch += [pltpu.VMEM((ROUTE_SHIFT_SLOTS, nseq) + tuple(c.shape[1:]), c.dtype), dma_sems, dma_sems, dma_sems]
        args += [c, nw]
    out = pl.pallas_call(
        functools.partial(_route_kernel, n_steps=n_steps, with_shift=shift is not None),
        grid=(n_steps,),
        in_specs=in_specs,
        out_specs=out_specs,
        out_shape=out_shape,
        scratch_shapes=scratch,
        compiler_params=_params(1),
        name="route",
    )(*args)
    return (out[:4], out[4]) if shift is not None else out


PEER_TE = 512
PEER_KEYS_PER_BLOCK = PEER_TE // PEER_NKEYS
PEER_TABLE_KEYS = 8
PEER_BLOCKS_PER_TABLE = PEER_TABLE_KEYS // PEER_KEYS_PER_BLOCK
assert PEER_BLOCKS_PER_TABLE == 2
SQRT_HALF = math.sqrt(0.5)


PEER_GATE_ROWS = 32


def _peer_gate_pieces(act_ref, w_ref, rank1_ref, p1_ref, m_ref, p0_ref, key_off):
    tm = act_ref.shape[1]

    def piece(ls, r0):
        gates = [jnp.zeros((PEER_GATE_ROWS, LANES), F32) for _ in range(PEER_KEYS_PER_BLOCK)]
        for hh in range(PEER_HEADS):
            r1 = rank1_ref[hh, r0:r0 + PEER_GATE_ROWS, ls]
            p1 = p1_ref[hh, r0:r0 + PEER_GATE_ROWS, ls]
            for j in range(PEER_KEYS_PER_BLOCK):
                kr = slice(key_off + j, key_off + j + 1)
                gates[j] = gates[j] + jnp.where(r1 < m_ref[hh, kr, ls], p1, 0.0) * p0_ref[hh, kr, ls]
        for j in range(PEER_KEYS_PER_BLOCK):
            rs = slice(j * PEER_NKEYS + r0, j * PEER_NKEYS + r0 + PEER_GATE_ROWS)
            a = act_ref[rs, ls]
            w_ref[rs, ls] = (gates[j] * (0.5 * a * (1.0 + lax.erf(a * SQRT_HALF)))).astype(BF16)

    return [functools.partial(piece, slice(c * LANES, (c + 1) * LANES), r0)
            for c in range(tm // LANES) for r0 in range(0, PEER_NKEYS, PEER_GATE_ROWS)]


PEER_DOT1_ROWS = 256
PEER_DOT2_ROWS = 512


def _interleave(*stages):
    tagged = [((i + 0.5) / len(st), si, piece) for si, st in enumerate(stages) for i, piece in enumerate(st)]
    return [piece for _, _, piece in sorted(tagged, key=lambda x: x[:2])]


def _peer_kernel(*refs, n_blocks, n_steps, with_shift):
    h_hbm, hn_ref, u_ref, vt_ref, rank1_ref, p1_ref, m_ref, p0_ref = refs[:8]
    g = pl.program_id(0)
    if with_shift:
        (c_ref, nw_ref, o_hbm, oc_ref, act0, act1, w0, w1, acc, obuf, sem_h, sem_o,
         buf, sem_body, sem_tail, sem_out) = refs[8:]
        ring = _ShiftRing(c_ref, nw_ref, oc_ref, buf, sem_body, sem_tail, sem_out)
        assert ring.n_chunks + ring.n_slots - 1 <= n_steps, "not enough grid steps to finish the shift"
        ring.step(g)
    else:
        o_hbm, act0, act1, w0, w1, acc, obuf, sem_h, sem_o = refs[8:]
    tm = obuf.shape[0]

    @pl.when(g == 0)
    def _():
        for ref in (act0, act1, w0, w1, acc):
            ref[...] = jnp.zeros(ref.shape, ref.dtype)

    blk3 = jnp.clip(g - 2, 0, n_steps - 3)
    tile = blk3 // n_blocks
    n_tiles = (n_steps - 2) // n_blocks
    first = (g >= 2) & (lax.rem(blk3, n_blocks) == 0)
    last = (g >= 2) & (lax.rem(blk3, n_blocks) == n_blocks - 1)
    tile_rows = lambda i: pl.ds(pl.multiple_of(i * tm, tm), tm)
    read_h = lambda i: pltpu.make_async_copy(h_hbm.at[tile_rows(i)], obuf, sem_h)
    write_o = lambda i: pltpu.make_async_copy(obuf, o_hbm.at[tile_rows(i)], sem_o)

    @pl.when(first)
    def _():
        pl.when(tile > 0)(lambda: write_o(jnp.maximum(tile - 1, 0)).wait())
        read_h(tile).start()

    def body(act_new, act_cur, w_new, w_old, key_off):
        def dot1(rows):
            act_new[rows, :] = lax.dot_general(u_ref[rows, :], hn_ref[...], NT_DIMS, preferred_element_type=F32)

        def dot2(rows):
            acc[rows, :] += jnp.dot(vt_ref[rows, :], w_old[...], preferred_element_type=F32)

        chunks = lambda n, size: [slice(r, r + size) for r in range(0, n, size)]
        for piece in _interleave(
                [functools.partial(dot1, rows) for rows in chunks(u_ref.shape[0], PEER_DOT1_ROWS)],
                _peer_gate_pieces(act_cur, w_new, rank1_ref, p1_ref, m_ref, p0_ref, key_off),
                [functools.partial(dot2, rows) for rows in chunks(vt_ref.shape[0], PEER_DOT2_ROWS)]):
            piece()

        @pl.when(last)
        def _():
            read_h(tile).wait()
            for c in range(tm // LANES):
                rows = slice(c * LANES, (c + 1) * LANES)
                obuf[rows, :] += acc[:, rows].T
            acc[...] = jnp.zeros(acc.shape, acc.dtype)
            write_o(tile).start()
            pl.when(tile == n_tiles - 1)(lambda: write_o(tile).wait())

    even = lax.rem(g, 2) == 0
    pl.when(even)(lambda: body(act0, act1, w1, w0, PEER_KEYS_PER_BLOCK))
    pl.when(jnp.logical_not(even))(lambda: body(act1, act0, w0, w1, 0))


PEER_SHIFT_SLOTS = 2


def _peer(h, hn, u, vt, tables, tm, shift=None):
    t, d = h.shape
    nb = u.shape[0] // PEER_TE
    assert nb % PEER_BLOCKS_PER_TABLE == 0
    n = (t // tm) * nb
    n_steps = n + 2
    blk1 = lambda g: jnp.minimum(g, n - 1)
    blk2 = lambda g: jnp.clip(g - 1, 0, n - 1)
    blk3 = lambda g: jnp.clip(g - 2, 0, n - 1)
    once = pl.Buffered(1)
    any_spec = pl.BlockSpec(memory_space=pl.ANY)
    tspec = pl.BlockSpec((PEER_HEADS, PEER_NKEYS, tm), lambda g: (0, 0, blk2(g) // nb), pipeline_mode=once)
    kspec = pl.BlockSpec((PEER_HEADS, PEER_TABLE_KEYS, tm),
                         lambda g: (0, (blk2(g) % nb) // PEER_BLOCKS_PER_TABLE, blk2(g) // nb))
    in_specs = [any_spec,
                pl.BlockSpec((tm, d), lambda g: (blk1(g) // nb, 0), pipeline_mode=once),
                pl.BlockSpec((PEER_TE, d), lambda g: (blk1(g) % nb, 0)),
                pl.BlockSpec((None, d, PEER_TE), lambda g: (blk3(g) % nb, 0, 0)), tspec, tspec, kspec, kspec]
    out_specs = [any_spec]
    out_shape = [jax.ShapeDtypeStruct((t, d), F32)]
    scratch = [pltpu.VMEM((PEER_TE, tm), F32), pltpu.VMEM((PEER_TE, tm), F32),
               pltpu.VMEM((PEER_TE, tm), BF16), pltpu.VMEM((PEER_TE, tm), BF16), pltpu.VMEM((d, tm), F32),
               pltpu.VMEM((tm, d), F32), pltpu.SemaphoreType.DMA(()), pltpu.SemaphoreType.DMA(())]
    args = [h, hn, u, vt, *tables]
    if shift is not None:
        c, nw = shift
        parts = (n_steps - PEER_SHIFT_SLOTS + 1) // c.shape[0]
        assert parts >= 1, "more sequences than grid steps"
        while c.shape[1] % parts:
            parts -= 1
        in_specs += [any_spec, any_spec]
        out_specs.append(any_spec)
        out_shape.append(jax.ShapeDtypeStruct(c.shape, c.dtype))
        dma_sems = pltpu.SemaphoreType.DMA((PEER_SHIFT_SLOTS,))
        scratch += [pltpu.VMEM((PEER_SHIFT_SLOTS, 1, c.shape[1] // parts) + tuple(c.shape[2:]), c.dtype),
                    dma_sems, dma_sems, dma_sems]
        args += [c, nw]
    out = pl.pallas_call(
        functools.partial(_peer_kernel, n_blocks=nb, n_steps=n_steps, with_shift=shift is not None),
        grid=(n_steps,),
        in_specs=in_specs,
        out_specs=out_specs,
        out_shape=out_shape,
        scratch_shapes=scratch,
        compiler_params=_params(1),
        name="peer",
    )(*args)
    return out if shift is not None else out[0]


def _alibi_slopes():
    return 2.0 ** (-8.0 * jnp.arange(1, N_ALIBI_HEADS + 1, dtype=F32) / N_ALIBI_HEADS)


def _subkey_matrix(subkeys):
    two, nk, dh = subkeys.shape
    eye = jnp.eye(PEER_HEADS * two, dtype=subkeys.dtype).reshape(PEER_HEADS, two, PEER_HEADS, two)
    sk = jnp.einsum("hcgb,cnd->hcngbd", eye, subkeys)
    return sk.reshape(PEER_HEADS * two * nk, PEER_HEADS * two * dh)


def _tail(x2d, oa, ob, zg, w, tm_block, tr, tm_peer, route_shift=None, peer_shift=None):
    h, hn = _block(x2d, oa, ob, zg, w["wa"], w["wb"], w["wo"], w["n2"], tm_block)
    routed = _route(hn, w["wq"], w["sk"], tr, route_shift)
    tables, shifted = routed if route_shift is not None else (routed, None)
    y = _peer(h, hn, w["u"], w["vt"], tables, tm_peer, peer_shift)
    if peer_shift is not None:
        return (y[0], shifted, y[1]) if route_shift is not None else y
    return (y, shifted) if route_shift is not None else y


def kernel(x_prompt, x_sample, cache_a_kv, cache_b1_kv, cache_b2_kv, cache_b3_kv, norm1_w, w_in, q_norm_a,
           k_norm_a, sink_a, q_norm_b, k_norm_b, w_branch_a, w_branch_b, w_out, norm2_w, peer_wq,
           peer_subkeys, peer_u, peer_v):
    assert norm1_w.shape[0] == 1, "single layer"
    b, s, d = x_prompt.shape
    n_dec = x_sample.shape[0]
    assert x_sample.shape[1] == 1

    slopes = _alibi_slopes()
    slopes_a, slopes_b = slopes[:A_Q_HEADS], slopes[A_Q_HEADS:]
    sink = sink_a[0].astype(F32)
    ones = jnp.ones((HEAD_DIM,), F32)
    col_w = jnp.concatenate(
        [jnp.tile(q_norm_a[0], A_Q_HEADS), jnp.tile(k_norm_a[0], A_KV_HEADS), jnp.tile(ones, A_KV_HEADS),
         jnp.tile(q_norm_b[0], B_HEADS)]
        + [jnp.tile(k_norm_b[0], B_HEADS_PER_GROUP), jnp.tile(ones, B_HEADS_PER_GROUP)] * N_B_GROUPS
    ).astype(F32)[None]
    gate_w = jnp.ones((1, GATE_W), F32)
    n1 = norm1_w[0].astype(F32)[None]
    w_cols = lambda h0, nh: w_in[0, :, h0 * HEAD_DIM:(h0 + nh) * HEAD_DIM]
    ref_kb, ref_vb = HEAD_QB + B_HEADS, HEAD_QB + 2 * B_HEADS
    w_qkv = jnp.concatenate(
        [w_cols(0, ref_kb)] + [w_cols(base + g * B_HEADS_PER_GROUP, B_HEADS_PER_GROUP)
                               for g in range(N_B_GROUPS) for base in (ref_kb, ref_vb)], axis=1).astype(BF16)
    w_gate = w_in[0, :, QKV_W:].astype(BF16)
    w = dict(wa=w_branch_a[0].astype(BF16), wb=w_branch_b[0].astype(BF16), wo=w_out[0].astype(BF16),
             n2=norm2_w[0].astype(F32)[None], wq=peer_wq[0].astype(BF16),
             sk=_subkey_matrix(peer_subkeys[0]).astype(BF16),
             u=peer_u[0].astype(BF16),
             vt=peer_v[0].astype(BF16).reshape(PEER_EXPERTS // PEER_TE, PEER_TE, d).transpose(0, 2, 1))

    xs = x_sample.reshape(n_dec, d)
    zs = _proj(xs, n1, w_qkv, col_w, QKV_KINDS, F32, n_dec, "proj_qkv_s")
    zgs = _proj(xs, n1, w_gate, gate_w, GATE_KINDS, BF16, n_dec, "proj_gate_s")
    zs3 = zs.reshape(n_dec, N_QKV_HEADS, HEAD_DIM)
    caches = (cache_a_kv[0], cache_b1_kv[0], cache_b2_kv[0], cache_b3_kv[0])
    kv_heads = [(HEAD_KA, A_KV_HEADS)] + [(_head_kb(g), B_HEADS_PER_GROUP) for g in range(N_B_GROUPS)]
    news = [zs3[:, h0:h0 + 2 * nh].reshape(n_dec, 2, nh, HEAD_DIM) for h0, nh in kv_heads]

    xp = x_prompt.reshape(b * s, d)
    zq = _proj(xp, n1, w_qkv, col_w, QKV_KINDS, F32, 1024, "proj_qkv")
    zg = _proj(xp, n1, w_gate, gate_w, GATE_KINDS, BF16, 1024, "proj_gate")
    zq3 = zq.reshape(b, s, QKV_W)
    oa = _attn_a(zq3, slopes_a, sink).reshape(b * s, A_Q_HEADS * HEAD_DIM)
    ob = _attn_b(zq3, slopes_b).reshape(b * s, B_HEADS_PER_GROUP * HEAD_DIM)
    y_prompt, new_b2, new_b3 = _tail(xp, oa, ob, zg, w, 256, 256, 1024, route_shift=(caches[2], news[2]),
                                     peer_shift=(caches[3], news[3]))
    y_prompt = y_prompt.reshape(b, s, d)

    def window(h0, nh, length):
        rows = zq3[:, s - length:, h0 * HEAD_DIM:(h0 + 2 * nh) * HEAD_DIM]
        return rows.reshape(1, b, length, 2, nh, HEAD_DIM)

    kv_prompt = [window(h0, nh, min(win, s))
                 for (h0, nh), win in zip(kv_heads, (A_WINDOW,) + tuple(w_ for w_, _ in B_GROUPS))]

    oa_s, ob_s = _sample_attn(zs3, *caches, slopes_a, sink, slopes_b)
    y_sample = _tail(xs, oa_s.astype(BF16), ob_s.astype(BF16), zgs, w, n_dec, n_dec, n_dec).reshape(n_dec, 1, d)

    kv_sample = [o[None] for o in (*_cache_update(caches[:2], news[:2]), new_b2, new_b3)]

    return (y_prompt, y_sample, *kv_prompt, *kv_sample)
```

```python
import functools
import math

import jax
import jax.numpy as jnp
from jax import lax
from jax.experimental import pallas as pl
from jax.experimental.pallas import tpu as pltpu

F32 = jnp.float32
BF16 = jnp.bfloat16

D_MODEL = 2048
HEAD_DIM = 128
A_Q_HEADS = 8
A_KV_HEADS = 2
A_GQA = A_Q_HEADS // A_KV_HEADS
A_WINDOW = 128
B_GROUPS = ((128, 1), (512, 4), (2048, 16))
B_HEADS_PER_GROUP = 4
N_B_GROUPS = len(B_GROUPS)
B_HEADS = N_B_GROUPS * B_HEADS_PER_GROUP
BAND = 128
N_ALIBI_HEADS = A_Q_HEADS + B_HEADS
ATTN_SCALE = HEAD_DIM ** -0.5
PEER_HEADS = 8
PEER_NKEYS = 128
PEER_EXPERTS = PEER_NKEYS * PEER_NKEYS
PEER_DKEY = 128
PEER_TOPK = 16
NORM_EPS = 1e-6
NEG_INF = -1e30

QKV_W = (A_Q_HEADS + 2 * A_KV_HEADS + 3 * B_HEADS) * HEAD_DIM
GATE_W = 2 * D_MODEL
HEAD_QA, HEAD_KA, HEAD_VA = 0, A_Q_HEADS, A_Q_HEADS + A_KV_HEADS
HEAD_QB = A_Q_HEADS + 2 * A_KV_HEADS
HEAD_KVB = HEAD_QB + B_HEADS
N_QKV_HEADS = QKV_W // HEAD_DIM


def _head_kb(g):
    return HEAD_KVB + 2 * B_HEADS_PER_GROUP * g

VMEM_LIMIT_BYTES = 56 * 1024 * 1024
LANES = 128

NT_DIMS = (((1,), (1,)), ((), ()))
TN_DIMS = (((0,), (0,)), ((), ()))


def _params(n_grid_axes):
    return pltpu.CompilerParams(
        dimension_semantics=("arbitrary",) * n_grid_axes,
        vmem_limit_bytes=VMEM_LIMIT_BYTES)


def _rms(x):
    return x * lax.rsqrt(jnp.mean(x * x, axis=-1, keepdims=True) + NORM_EPS)


PROJ_TN = 1024
PROJ_SUB = 512


def _proj_kernel(x_ref, n1_ref, w_ref, cw_ref, o_ref, xn_ref, *, gate):
    @pl.when(pl.program_id(1) == 0)
    def _():
        xn_ref[...] = (_rms(x_ref[...]) * n1_ref[...]).astype(BF16)

    for c0 in range(0, o_ref.shape[1], PROJ_SUB):
        z = jnp.dot(xn_ref[...], w_ref[:, c0:c0 + PROJ_SUB], preferred_element_type=F32)
        if gate:
            o_ref[:, c0:c0 + PROJ_SUB] = jax.nn.sigmoid(z).astype(o_ref.dtype)
            continue
        for h0 in range(0, PROJ_SUB, HEAD_DIM):
            cs = slice(c0 + h0, c0 + h0 + HEAD_DIM)
            zh = z[:, h0:h0 + HEAD_DIM]
            o_ref[:, cs] = jnp.where(cw_ref[1:2, cs] != 0.0, _rms(zh) * cw_ref[0:1, cs], zh).astype(o_ref.dtype)


def _proj(x2d, n1, w, cw, gate, out_dtype, tm, name):
    t, d = x2d.shape
    n = w.shape[1]
    assert t % tm == 0 and n % PROJ_TN == 0
    return pl.pallas_call(
        functools.partial(_proj_kernel, gate=gate),
        grid=(t // tm, n // PROJ_TN),
        in_specs=[
            pl.BlockSpec((tm, d), lambda i, j: (i, 0)),
            pl.BlockSpec((1, d), lambda i, j: (0, 0)),
            pl.BlockSpec((d, PROJ_TN), lambda i, j: (0, j)),
            pl.BlockSpec((2, PROJ_TN), lambda i, j: (0, j)),
        ],
        out_specs=pl.BlockSpec((tm, PROJ_TN), lambda i, j: (i, j)),
        out_shape=jax.ShapeDtypeStruct((t, n), out_dtype),
        scratch_shapes=[pltpu.VMEM((tm, d), BF16)],
        compiler_params=_params(2),
        name=name,
    )(x2d, n1, w, cw)


def _band_softmax(q, k, v, slope_dist, mask, sink):
    s = lax.dot_general(q, k, NT_DIMS, preferred_element_type=F32) * ATTN_SCALE
    logits = jnp.where(mask, s - slope_dist, NEG_INF)
    m = jnp.max(logits, axis=-1, keepdims=True)
    if sink is not None:
        m = jnp.maximum(m, sink)
    p = jnp.exp(logits - m)
    denom = jnp.sum(p, axis=-1, keepdims=True)
    if sink is not None:
        denom = denom + jnp.exp(sink - m)
    o = jnp.dot(p.astype(BF16), v, preferred_element_type=F32) / denom
    return o, m + jnp.log(denom)


def _band_geometry(has_prev):
    nk = 2 * BAND if has_prev else BAND
    qi = lax.broadcasted_iota(jnp.int32, (BAND, nk), 0)
    sj = lax.broadcasted_iota(jnp.int32, (BAND, nk), 1)
    dist = qi - sj + (BAND if has_prev else 0)
    return dist, sj


def _attn_a_kernel(slopes_ref, sink_ref, q_ref, k_ref, v_ref, o_ref):
    kvh = pl.program_id(1)
    n_blocks = q_ref.shape[1] // BAND
    dist, sj = _band_geometry(True)
    in_window = (dist >= 0) & (dist <= A_WINDOW)
    distf = dist.astype(F32)

    def block(blk, carry):
        cur = pl.multiple_of(blk * BAND, BAND)
        prev = pl.multiple_of(jnp.maximum(blk - 1, 0) * BAND, BAND)
        k = jnp.concatenate([k_ref[0, pl.ds(prev, BAND), :], k_ref[0, pl.ds(cur, BAND), :]], axis=0).astype(BF16)
        v = jnp.concatenate([v_ref[0, pl.ds(prev, BAND), :], v_ref[0, pl.ds(cur, BAND), :]], axis=0).astype(BF16)
        mask = in_window & (sj >= jnp.where(blk > 0, 0, BAND))
        for g in range(A_GQA):
            cs = slice(g * HEAD_DIM, (g + 1) * HEAD_DIM)
            head = kvh * A_GQA + g
            q = q_ref[0, pl.ds(cur, BAND), cs].astype(BF16)
            o, _ = _band_softmax(q, k, v, slopes_ref[head] * distf, mask, sink_ref[head])
            o_ref[0, pl.ds(cur, BAND), cs] = o.astype(o_ref.dtype)
        return carry

    lax.fori_loop(0, n_blocks, block, 0)


def _attn_a(zq3, slopes_a, sink):
    b, s, _ = zq3.shape
    gw = A_GQA * HEAD_DIM
    smem = pl.BlockSpec(memory_space=pltpu.SMEM)
    return pl.pallas_call(
        _attn_a_kernel,
        grid=(b, A_KV_HEADS),
        in_specs=[
            smem, smem,
            pl.BlockSpec((1, s, gw), lambda bi, h: (bi, 0, h)),
            pl.BlockSpec((1, s, HEAD_DIM), lambda bi, h: (bi, 0, HEAD_KA + h)),
            pl.BlockSpec((1, s, HEAD_DIM), lambda bi, h: (bi, 0, HEAD_VA + h)),
        ],
        out_specs=pl.BlockSpec((1, s, gw), lambda bi, h: (bi, 0, h)),
        out_shape=jax.ShapeDtypeStruct((b, s, A_Q_HEADS * HEAD_DIM), BF16),
        compiler_params=_params(2),
        name="attn_a",
    )(slopes_a, sink, zq3, zq3, zq3)


def _attn_b_kernel(slopes_ref, *refs):
    q_refs, k_refs, v_refs = refs[0:3], refs[3:6], refs[6:9]
    o_ref, og_ref, lse_ref = refs[9:12]
    h = pl.program_id(1)
    s_len = o_ref.shape[1]

    for g, (win, dil) in enumerate(B_GROUPS):
        n_blocks = s_len // (BAND * dil)
        slope = slopes_ref[g * B_HEADS_PER_GROUP + h]
        for has_prev in (False, True):
            dist, _ = _band_geometry(has_prev)
            mask = (dist >= 0) & (dist <= win // dil)
            slope_dist = slope * (dil * dist).astype(F32)
            for blk in range(n_blocks):
                if (blk > 0) != has_prev:
                    continue
                for r in range(dil):
                    def rows(ref, b0):
                        start = BAND * dil * b0 + r
                        if dil == 1:
                            return ref[0, pl.ds(start, BAND), :]
                        return ref[0, pl.ds(start, BAND, stride=dil), :]

                    q = rows(q_refs[g], blk).astype(BF16)
                    k = rows(k_refs[g], blk)
                    v = rows(v_refs[g], blk)
                    if has_prev:
                        k = jnp.concatenate([rows(k_refs[g], blk - 1), k], axis=0)
                        v = jnp.concatenate([rows(v_refs[g], blk - 1), v], axis=0)
                    o, lse = _band_softmax(q, k.astype(BF16), v.astype(BF16), slope_dist, mask, None)
                    start = BAND * dil * blk + r
                    idx = pl.ds(start, BAND) if dil == 1 else pl.ds(start, BAND, stride=dil)
                    og_ref[g, idx, :] = o
                    lse_ref[g, idx, :] = jnp.broadcast_to(lse, (BAND, HEAD_DIM))

    def combine(c, carry):
        rs = pl.ds(pl.multiple_of(c * BAND, BAND), BAND)
        lses = [lse_ref[g, rs, :] for g in range(N_B_GROUPS)]
        mx = functools.reduce(jnp.maximum, lses)
        ws = [jnp.exp(l - mx) for l in lses]
        acc = functools.reduce(lambda a, b_: a + b_, [w * og_ref[g, rs, :] for g, w in enumerate(ws)])
        o_ref[0, rs, :] = (acc / functools.reduce(lambda a, b_: a + b_, ws)).astype(o_ref.dtype)
        return carry

    lax.fori_loop(0, s_len // BAND, combine, 0)


def _attn_b(zq3, slopes_b):
    b, s, _ = zq3.shape
    smem = pl.BlockSpec(memory_space=pltpu.SMEM)

    def head_spec(first_head):
        return [pl.BlockSpec((1, s, HEAD_DIM), lambda bi, h, c=first_head(g): (bi, 0, c + h))
                for g in range(N_B_GROUPS)]

    return pl.pallas_call(
        _attn_b_kernel,
        grid=(b, B_HEADS_PER_GROUP),
        in_specs=[smem] + head_spec(lambda g: HEAD_QB + g * B_HEADS_PER_GROUP) + head_spec(_head_kb)
        + head_spec(lambda g: _head_kb(g) + B_HEADS_PER_GROUP),
        out_specs=pl.BlockSpec((1, s, HEAD_DIM), lambda bi, h: (bi, 0, h)),
        out_shape=jax.ShapeDtypeStruct((b, s, B_HEADS_PER_GROUP * HEAD_DIM), BF16),
        scratch_shapes=[pltpu.VMEM((N_B_GROUPS, s, HEAD_DIM), F32),
                        pltpu.VMEM((N_B_GROUPS, s, HEAD_DIM), F32)],
        compiler_params=_params(2),
        name="attn_b",
    )(slopes_b, *([zq3] * 9))


SAMPLE_BS = 4
SUBLANES = 8
MASKED = 1e30


def _window_read(q8, kv8, c, bias, bias_new, sink8, hk):
    per_tile = SUBLANES // (2 * hk)
    row = lax.broadcasted_iota(jnp.int32, (1, 1, SUBLANES, HEAD_DIM), 2)
    is_key = lax.rem(row, 2 * hk) < hk
    first_key = row < hk
    first_value = (row >= hk) & (row < 2 * hk)
    roll = lambda x, shift: pltpu.roll(x, shift, 2)

    def over_positions(x, op):
        for i in range(1, per_tile):
            x = op(x, roll(x, i * 2 * hk))
        return x

    q8 = q8 * ATTN_SCALE
    logits = jnp.sum(c * q8, axis=-1, keepdims=True) - bias
    logit_new = jnp.sum(kv8 * q8, axis=-1, keepdims=True) - bias_new
    mx = over_positions(jnp.max(logits, axis=1, keepdims=True), jnp.maximum)
    mx = jnp.maximum(mx, logit_new)
    if sink8 is not None:
        mx = jnp.maximum(mx, sink8)
    mx = jnp.where(is_key, mx, 0.0)
    p = jnp.exp(logits - mx)
    p_new = jnp.where(first_key, jnp.exp(logit_new - mx), 0.0)
    den = over_positions(jnp.sum(p, axis=1, keepdims=True), jnp.add) + p_new
    if sink8 is not None:
        den = den + jnp.where(first_key, jnp.exp(sink8 - mx), 0.0)
    acc = over_positions(jnp.sum(roll(p, hk) * c, axis=1, keepdims=True), jnp.add) + roll(p_new, hk) * kv8
    o8 = acc / jnp.where(first_value, roll(den, hk), 1.0)
    lse8 = roll(mx + jnp.log(jnp.where(first_key, den, 1.0)), hk)
    return o8, lse8


def _sample_attn_kernel(qa_ref, kva_ref, qb_ref, kvb_ref, ca_ref, cb1_ref, cb2_ref, cb3_ref,
                        bias_a_ref, new_a_ref, sink_ref, bias_b_ref, new_b_ref, oa_ref, ob_ref):
    kva = kva_ref[...][:, None]
    ca = ca_ref[...]
    for g in range(A_GQA):
        o8, _ = _window_read(qa_ref[:, g][:, None], kva, ca, bias_a_ref[g], new_a_ref[...], sink_ref[g], A_KV_HEADS)
        oa_ref[:, g] = o8[:, 0]
    outs, lses = [], []
    for g, c_ref in enumerate((cb1_ref, cb2_ref, cb3_ref)):
        o8, lse8 = _window_read(qb_ref[:, g][:, None], kvb_ref[:, g][:, None], c_ref[...], bias_b_ref[g],
                                new_b_ref[...], None, B_HEADS_PER_GROUP)
        outs.append(o8[:, 0])
        lses.append(lse8[:, 0])
    mx = functools.reduce(jnp.maximum, lses)
    ws = [jnp.exp(l - mx) for l in lses]
    acc = functools.reduce(lambda a, b_: a + b_, [w * o for w, o in zip(ws, outs)])
    ob_ref[...] = acc / functools.reduce(lambda a, b_: a + b_, ws)


def _tile_rows(x, hk):
    per_tile = SUBLANES // (2 * hk)
    pos = jnp.concatenate([x, jnp.zeros_like(x)], axis=-2)
    return jnp.concatenate([pos] * per_tile, axis=-2)


def _key_row_bias(dist, slopes, hk):
    key = dist[:, :, None] * slopes[None, None, :]
    rows = jnp.concatenate([key, jnp.full_like(key, MASKED)], axis=-1).reshape(dist.shape[0], SUBLANES)
    return jnp.broadcast_to(rows[:, :, None], rows.shape + (HEAD_DIM,))


def _sample_attn(zs3, cache_a, cache_b1, cache_b2, cache_b3, slopes_a, sink, slopes_b):
    n = zs3.shape[0]
    bs = SAMPLE_BS
    f32 = lambda x: x.astype(F32)

    def tiles(c, dil):
        nb, l, two, hh, hd = c.shape
        assert l == A_WINDOW * dil
        per_tile = SUBLANES // (two * hh)
        view = c.reshape(nb, l // (dil * per_tile), dil * SUBLANES, hd)
        return view, pl.BlockSpec((bs, view.shape[1], SUBLANES, hd), lambda i: (i, 0, 0, 0))

    ca, ca_spec = tiles(cache_a, 1)
    cb = [tiles(c, dil) for c, (_, dil) in zip((cache_b1, cache_b2, cache_b3), B_GROUPS)]

    heads_a = lambda h0: zs3[:, h0:h0 + A_KV_HEADS]
    qa = zs3[:, HEAD_QA:HEAD_QA + A_Q_HEADS].reshape(n, A_KV_HEADS, A_GQA, HEAD_DIM).transpose(0, 2, 1, 3)
    qa8 = _tile_rows(qa, A_KV_HEADS)
    kva8 = jnp.concatenate([heads_a(HEAD_KA), heads_a(HEAD_VA)] * (SUBLANES // (2 * A_KV_HEADS)), axis=1)
    qb = zs3[:, HEAD_QB:HEAD_QB + B_HEADS].reshape(n, N_B_GROUPS, B_HEADS_PER_GROUP, HEAD_DIM)
    qb8 = _tile_rows(qb, B_HEADS_PER_GROUP)
    kvb8 = zs3[:, HEAD_KVB:].reshape(n, N_B_GROUPS, 2 * B_HEADS_PER_GROUP, HEAD_DIM)

    per_a = SUBLANES // (2 * A_KV_HEADS)
    dist_a = (A_WINDOW - jnp.arange(A_WINDOW, dtype=F32)).reshape(A_WINDOW // per_a, per_a)
    slopes_ag = f32(slopes_a).reshape(A_KV_HEADS, A_GQA).T
    bias_a = jnp.stack([_key_row_bias(dist_a, slopes_ag[g], A_KV_HEADS) for g in range(A_GQA)])
    new_a = _key_row_bias(jnp.zeros((1, per_a), F32), slopes_ag[0], A_KV_HEADS)[0]
    sink8 = _tile_rows(f32(sink).reshape(A_KV_HEADS, A_GQA).T[:, :, None], A_KV_HEADS)
    key_rows = new_a[None, :, :1] == 0.0
    sink8 = jnp.broadcast_to(jnp.where(key_rows, sink8, -MASKED), (A_GQA, SUBLANES, HEAD_DIM))
    slopes_bg = f32(slopes_b).reshape(N_B_GROUPS, B_HEADS_PER_GROUP)
    dist_b = lambda dil: ((A_WINDOW - jnp.arange(A_WINDOW, dtype=F32)) * dil)[:, None]
    bias_b = jnp.stack([_key_row_bias(dist_b(dil), slopes_bg[g], B_HEADS_PER_GROUP)
                        for g, (_, dil) in enumerate(B_GROUPS)])
    new_b = _key_row_bias(jnp.zeros((1, 1), F32), slopes_bg[0], B_HEADS_PER_GROUP)[0]

    full = lambda a: pl.BlockSpec(a.shape, lambda i: (0,) * a.ndim)
    per_seq = lambda a: pl.BlockSpec((bs,) + a.shape[1:], lambda i: (i,) + (0,) * (a.ndim - 1))
    oa8, ob8 = pl.pallas_call(
        _sample_attn_kernel,
        grid=(n // bs,),
        in_specs=[per_seq(qa8), per_seq(kva8), per_seq(qb8), per_seq(kvb8),
                  ca_spec, cb[0][1], cb[1][1], cb[2][1],
                  full(bias_a), full(new_a), full(sink8), full(bias_b), full(new_b)],
        out_specs=[pl.BlockSpec((bs, A_GQA, SUBLANES, HEAD_DIM), lambda i: (i, 0, 0, 0)),
                   pl.BlockSpec((bs, SUBLANES, HEAD_DIM), lambda i: (i, 0, 0))],
        out_shape=[jax.ShapeDtypeStruct((n, A_GQA, SUBLANES, HEAD_DIM), F32),
                   jax.ShapeDtypeStruct((n, SUBLANES, HEAD_DIM), F32)],
        compiler_params=_params(1),
        name="sample_attn",
    )(qa8, kva8, qb8, kvb8, ca, cb[0][0], cb[1][0], cb[2][0], bias_a, new_a, sink8, bias_b, new_b)
    oa = oa8[:, :, A_KV_HEADS:2 * A_KV_HEADS].transpose(0, 2, 1, 3).reshape(n, A_Q_HEADS * HEAD_DIM)
    ob = ob8[:, B_HEADS_PER_GROUP:].reshape(n, B_HEADS_PER_GROUP * HEAD_DIM)
    return oa, ob


CACHE_SLOTS = 3
CACHE_CHUNK_BYTES = 4 * 1024 * 1024


class _ShiftRing:
    def __init__(self, c, nw, o, buf, sem_body, sem_tail, sem_out):
        self.c, self.nw, self.o, self.buf = c, nw, o, buf
        self.sem_body, self.sem_tail, self.sem_out = sem_body, sem_tail, sem_out
        self.n_slots, self.nseq, self.rows = buf.shape[:3]
        n, self.l = c.shape[0], c.shape[1]
        assert n % self.nseq == 0 and self.l % self.rows == 0
        self.parts = self.l // self.rows
        self.n_chunks = (n // self.nseq) * self.parts

    def _where(self, k):
        return k % self.n_slots, pl.ds((k // self.parts) * self.nseq, self.nseq), (k % self.parts) * self.rows

    def _body(self, k):
        slot, seqs, r0 = self._where(k)
        return pltpu.make_async_copy(self.c.at[seqs, pl.ds(r0 + 1, self.rows - 1)],
                                     self.buf.at[slot, :, pl.ds(0, self.rows - 1)], self.sem_body.at[slot])

    def _tail_old(self, k):
        slot, seqs, r0 = self._where(k)
        return pltpu.make_async_copy(self.c.at[seqs, pl.ds(jnp.minimum(r0 + self.rows, self.l - 1), 1)],
                                     self.buf.at[slot, :, pl.ds(self.rows - 1, 1)], self.sem_tail.at[slot])

    def _tail_new(self, k):
        slot, seqs, _ = self._where(k)
        return pltpu.make_async_copy(self.nw.at[seqs], self.buf.at[slot, :, self.rows - 1], self.sem_tail.at[slot])

    def write(self, k):
        slot, seqs, r0 = self._where(k)
        return pltpu.make_async_copy(self.buf.at[slot], self.o.at[seqs, pl.ds(r0, self.rows)], self.sem_out.at[slot])

    def _read(self, k, action):
        action(self._body(k))
        if self.parts == 1:
            action(self._tail_new(k))
        else:
            is_last = (k % self.parts) == self.parts - 1
            pl.when(is_last)(lambda: action(self._tail_new(k)))
            pl.when(jnp.logical_not(is_last))(lambda: action(self._tail_old(k)))

    def start_read(self, k):
        self._read(k, lambda cp: cp.start())

    def wait_read(self, k):
        self._read(k, lambda cp: cp.wait())

    def step(self, k):
        n = self.n_chunks
        written = k + 1 - self.n_slots
        pl.when(k == 0)(lambda: self.start_read(0))
        pl.when((written >= 0) & (written < n))(lambda: self.write(jnp.clip(written, 0, n - 1)).wait())
        pl.when(k + 1 < n)(lambda: self.start_read(jnp.minimum(k + 1, n - 1)))

        @pl.when(k < n)
        def _():
            self.wait_read(jnp.minimum(k, n - 1))
            self.write(jnp.minimum(k, n - 1)).start()

    def drain(self, done_steps):
        for k in range(max(done_steps - self.n_slots + 1, 0), self.n_chunks):
            self.write(k).wait()


def _cache_ring(c, nw, o, buf, sem_body, sem_tail, sem_out):
    ring = _ShiftRing(c, nw, o, buf, sem_body, sem_tail, sem_out)

    def step(k, carry):
        ring.step(k)
        return carry

    lax.fori_loop(0, ring.n_chunks, step, 0)
    ring.drain(ring.n_chunks)


def _cache_chunk(c):
    n, l = c.shape[0], c.shape[1]
    seq_bytes = math.prod(c.shape[1:]) * c.dtype.itemsize
    if seq_bytes <= CACHE_CHUNK_BYTES:
        return min(n, CACHE_CHUNK_BYTES // seq_bytes), l
    return 1, l // (seq_bytes // CACHE_CHUNK_BYTES)


def _cache_kernel(*refs):
    n = len(refs) // 3
    caches, news, outs = refs[:n], refs[n:2 * n], refs[2 * n:]
    for c, nw, o in zip(caches, news, outs):
        nseq, rows = _cache_chunk(c)
        dma_sems = pltpu.SemaphoreType.DMA((CACHE_SLOTS,))
        pl.run_scoped(
            functools.partial(_cache_ring, c, nw, o),
            pltpu.VMEM((CACHE_SLOTS, nseq, rows) + tuple(c.shape[2:]), c.dtype), dma_sems, dma_sems, dma_sems)


def _cache_update(caches, news):
    n = len(caches)
    any_spec = pl.BlockSpec(memory_space=pl.ANY)
    return pl.pallas_call(
        _cache_kernel,
        in_specs=[any_spec] * (2 * n),
        out_specs=[any_spec] * n,
        out_shape=[jax.ShapeDtypeStruct(c.shape, c.dtype) for c in caches],
        compiler_params=pltpu.CompilerParams(vmem_limit_bytes=VMEM_LIMIT_BYTES),
        name="cache_update",
    )(*caches, *news)


def _block_kernel(x_ref, oa_ref, ob_ref, ga_ref, gb_ref, wa_ref, wb_ref, wo_ref, n2_ref, h_ref, hn_ref):
    ya = jnp.dot(oa_ref[...], wa_ref[...], preferred_element_type=F32)
    yb = jnp.dot(ob_ref[...], wb_ref[...], preferred_element_type=F32)
    mix = (ga_ref[...].astype(F32) * ya + gb_ref[...].astype(F32) * yb).astype(BF16)
    h = x_ref[...] + jnp.dot(mix, wo_ref[...], preferred_element_type=F32)
    h_ref[...] = h
    hn_ref[...] = (_rms(h) * n2_ref[...]).astype(BF16)


def _block(x2d, oa, ob, zg, wa, wb, wo, n2, tm):
    t, d = x2d.shape
    row = lambda w: pl.BlockSpec((tm, w), lambda i: (i, 0))
    full = lambda a: pl.BlockSpec(a.shape, lambda i: (0, 0))
    return pl.pallas_call(
        _block_kernel,
        grid=(t // tm,),
        in_specs=[row(d), row(oa.shape[1]), row(ob.shape[1]),
                  pl.BlockSpec((tm, d), lambda i: (i, 0)), pl.BlockSpec((tm, d), lambda i: (i, 1)),
                  full(wa), full(wb), full(wo), full(n2)],
        out_specs=[row(d), row(d)],
        out_shape=[jax.ShapeDtypeStruct((t, d), F32), jax.ShapeDtypeStruct((t, d), BF16)],
        compiler_params=_params(1),
        name="block",
    )(x2d, oa, ob, zg, zg, wa, wb, wo, n2)


RANK_NONE = float(PEER_NKEYS)
CAND_ROWS = tuple(PEER_TOPK // (a + 1) for a in range(PEER_TOPK))
CAND_PAD = tuple(-(-r // 8) * 8 for r in CAND_ROWS)


def _extract(work, n_take, break_ties):
    rows = lax.broadcasted_iota(jnp.int32, work.shape, 0).astype(F32)
    rank = jnp.full(work.shape, RANK_NONE, F32)
    vals = []
    for kk in range(n_take):
        mx = jnp.max(work, axis=0, keepdims=True)
        sel = work == mx
        if break_ties:
            first = jnp.min(jnp.where(sel, rows, float(work.shape[0])), axis=0, keepdims=True)
            sel = rows == first
        rank = jnp.where(sel, float(kk), rank)
        work = jnp.where(sel, -jnp.inf, work)
        vals.append(mx)
    return rank, jnp.concatenate(vals, axis=0)


def _extract_top(work, n_take):
    rank, vals = _extract(work, n_take, break_ties=False)
    ranked = jnp.sum(jnp.where(rank < RANK_NONE, 1.0, 0.0), axis=0, keepdims=True)
    return lax.cond(jnp.max(ranked) > n_take,
                    lambda: _extract(work, n_take, break_ties=True), lambda: (rank, vals))


ROUTE_SHIFT_SLOTS = 2
ROUTE_SHIFT_PER_STEP = 2


def _route_kernel(*refs, n_steps, with_shift):
    hn_ref, wq_ref, sk_ref = refs[:3]
    if with_shift:
        c_ref, nw_ref, rank1_ref, p1_ref, m_ref, p0_ref, oc_ref, s_scr, buf, sem_body, sem_tail, sem_out = refs[3:]
        ring = _ShiftRing(c_ref, nw_ref, oc_ref, buf, sem_body, sem_tail, sem_out)
        assert ring.n_chunks == n_steps * ROUTE_SHIFT_PER_STEP and PEER_HEADS % ROUTE_SHIFT_PER_STEP == 0
        first_chunk = pl.program_id(0) * ROUTE_SHIFT_PER_STEP
    else:
        rank1_ref, p1_ref, m_ref, p0_ref, s_scr = refs[3:]
        ring = None
    q = jnp.dot(hn_ref[...], wq_ref[...], preferred_element_type=F32).astype(BF16)
    s_scr[...] = lax.dot_general(sk_ref[...], q, NT_DIMS, preferred_element_type=F32)

    def head(hh, carry):
        if ring is not None:
            every = PEER_HEADS // ROUTE_SHIFT_PER_STEP
            pl.when(lax.rem(hh, every) == 0)(lambda: ring.step(first_chunk + hh // every))
        base = pl.multiple_of(hh * 2 * PEER_NKEYS, 2 * PEER_NKEYS)
        s0 = s_scr[pl.ds(base, PEER_NKEYS), :]
        s1 = s_scr[pl.ds(base + PEER_NKEYS, PEER_NKEYS), :]
        rank0, vals0 = _extract_top(s0, PEER_TOPK)
        rank1, vals1 = _extract_top(s1, PEER_TOPK)
        cand = []
        for a in range(PEER_TOPK):
            blk = vals0[a:a + 1] + vals1[:CAND_PAD[a]]
            rr = lax.broadcasted_iota(jnp.int32, blk.shape, 0)
            cand.append(jnp.where(rr < CAND_ROWS[a], blk, -jnp.inf))
        taken, best = _extract_top(jnp.concatenate(cand, axis=0), PEER_TOPK)
        zsum = jnp.sum(jnp.exp(best - best[0:1]), axis=0, keepdims=True)
        m = jnp.zeros_like(s0)
        off = 0
        for a in range(PEER_TOPK):
            cnt = jnp.sum(jnp.where(taken[off:off + CAND_PAD[a]] < RANK_NONE, 1.0, 0.0), axis=0, keepdims=True)
            m = jnp.where(rank0 == float(a), cnt, m)
            off += CAND_PAD[a]
        rank1_ref[hh] = rank1
        p1_ref[hh] = jnp.exp(s1 - vals1[0:1])
        m_ref[hh] = m
        p0_ref[hh] = jnp.exp(s0 - vals0[0:1]) / zsum
        return carry

    lax.fori_loop(0, PEER_HEADS, head, 0)
    if ring is not None:
        pl.when(pl.program_id(0) == n_steps - 1)(lambda: ring.drain(ring.n_chunks))


def _route(hn, wq, sk, tr, shift=None):
    t, d = hn.shape
    n_steps = t // tr
    table = jax.ShapeDtypeStruct((PEER_HEADS, PEER_NKEYS, t), F32)
    tspec = pl.BlockSpec((PEER_HEADS, PEER_NKEYS, tr), lambda i: (0, 0, i))
    full = lambda a: pl.BlockSpec(a.shape, lambda i: (0, 0))
    in_specs = [pl.BlockSpec((tr, d), lambda i: (i, 0)), full(wq), full(sk)]
    out_specs, out_shape = [tspec] * 4, [table] * 4
    scratch = [pltpu.VMEM((PEER_HEADS * 2 * PEER_NKEYS, tr), F32)]
    args = [hn, wq, sk]
    if shift is not None:
        c, nw = shift
        nseq = c.shape[0] // (n_steps * ROUTE_SHIFT_PER_STEP)
        assert nseq * n_steps * ROUTE_SHIFT_PER_STEP == c.shape[0]
        any_spec = pl.BlockSpec(memory_space=pl.ANY)
        in_specs += [any_spec, any_spec]
        out_specs = out_specs + [any_spec]
        out_shape = out_shape + [jax.ShapeDtypeStruct(c.shape, c.dtype)]
        dma_sems = pltpu.SemaphoreType.DMA((ROUTE_SHIFT_SLOTS,))
        scratch += [pltpu.VMEM((ROUTE_SHIFT_SLOTS, nseq) + tuple(c.shape[1:]), c.dtype), dma_sems, dma_sems, dma_sems]
        args += [c, nw]
    out = pl.pallas_call(
        functools.partial(_route_kernel, n_steps=n_steps, with_shift=shift is not None),
        grid=(n_steps,),
        in_specs=in_specs,
        out_specs=out_specs,
        out_shape=out_shape,
        scratch_shapes=scratch,
        compiler_params=_params(1),
        name="route",
    )(*args)
    return (out[:4], out[4]) if shift is not None else out


PEER_TE = 512
PEER_KEYS_PER_BLOCK = PEER_TE // PEER_NKEYS
PEER_TABLE_KEYS = 8
PEER_BLOCKS_PER_TABLE = PEER_TABLE_KEYS // PEER_KEYS_PER_BLOCK
assert PEER_BLOCKS_PER_TABLE == 2
SQRT_HALF = math.sqrt(0.5)


PEER_GATE_ROWS = 32


def _peer_gate_pieces(act_ref, w_ref, rank1_ref, p1_ref, m_ref, p0_ref, key_off):
    tm = act_ref.shape[1]

    def piece(ls, r0):
        gates = [jnp.zeros((PEER_GATE_ROWS, LANES), F32) for _ in range(PEER_KEYS_PER_BLOCK)]
        for hh in range(PEER_HEADS):
            r1 = rank1_ref[hh, r0:r0 + PEER_GATE_ROWS, ls]
            p1 = p1_ref[hh, r0:r0 + PEER_GATE_ROWS, ls]
            for j in range(PEER_KEYS_PER_BLOCK):
                kr = slice(key_off + j, key_off + j + 1)
                gates[j] = gates[j] + jnp.where(r1 < m_ref[hh, kr, ls], p1, 0.0) * p0_ref[hh, kr, ls]
        for j in range(PEER_KEYS_PER_BLOCK):
            rs = slice(j * PEER_NKEYS + r0, j * PEER_NKEYS + r0 + PEER_GATE_ROWS)
            a = act_ref[rs, ls]
            w_ref[rs, ls] = (gates[j] * (0.5 * a * (1.0 + lax.erf(a * SQRT_HALF)))).astype(BF16)

    return [functools.partial(piece, slice(c * LANES, (c + 1) * LANES), r0)
            for c in range(tm // LANES) for r0 in range(0, PEER_NKEYS, PEER_GATE_ROWS)]


PEER_DOT1_ROWS = 256
PEER_DOT2_ROWS = 512


def _interleave(*stages):
    tagged = [((i + 0.5) / len(st), si, piece) for si, st in enumerate(stages) for i, piece in enumerate(st)]
    return [piece for _, _, piece in sorted(tagged, key=lambda x: x[:2])]


def _peer_kernel(*refs, n_blocks, n_steps, with_shift):
    h_hbm, hn_ref, u_ref, vt_ref, rank1_ref, p1_ref, m_ref, p0_ref = refs[:8]
    g = pl.program_id(0)
    if with_shift:
        (c_ref, nw_ref, o_hbm, oc_ref, act0, act1, w0, w1, acc, obuf, sem_h, sem_o,
         buf, sem_body, sem_tail, sem_out) = refs[8:]
        ring = _ShiftRing(c_ref, nw_ref, oc_ref, buf, sem_body, sem_tail, sem_out)
        assert ring.n_chunks + ring.n_slots - 1 <= n_steps, "not enough grid steps to finish the shift"
        ring.step(g)
    else:
        o_hbm, act0, act1, w0, w1, acc, obuf, sem_h, sem_o = refs[8:]
    tm = obuf.shape[0]

    @pl.when(g == 0)
    def _():
        for ref in (act0, act1, w0, w1, acc):
            ref[...] = jnp.zeros(ref.shape, ref.dtype)

    blk3 = jnp.clip(g - 2, 0, n_steps - 3)
    tile = blk3 // n_blocks
    n_tiles = (n_steps - 2) // n_blocks
    first = (g >= 2) & (lax.rem(blk3, n_blocks) == 0)
    last = (g >= 2) & (lax.rem(blk3, n_blocks) == n_blocks - 1)
    tile_rows = lambda i: pl.ds(pl.multiple_of(i * tm, tm), tm)
    read_h = lambda i: pltpu.make_async_copy(h_hbm.at[tile_rows(i)], obuf, sem_h)
    write_o = lambda i: pltpu.make_async_copy(obuf, o_hbm.at[tile_rows(i)], sem_o)

    @pl.when(first)
    def _():
        pl.when(tile > 0)(lambda: write_o(jnp.maximum(tile - 1, 0)).wait())
        read_h(tile).start()

    def body(act_new, act_cur, w_new, w_old, key_off):
        def dot1(rows):
            act_new[rows, :] = lax.dot_general(u_ref[rows, :], hn_ref[...], NT_DIMS, preferred_element_type=F32)

        def dot2(rows):
            acc[rows, :] += jnp.dot(vt_ref[rows, :], w_old[...], preferred_element_type=F32)

        chunks = lambda n, size: [slice(r, r + size) for r in range(0, n, size)]
        for piece in _interleave(
                [functools.partial(dot1, rows) for rows in chunks(u_ref.shape[0], PEER_DOT1_ROWS)],
                _peer_gate_pieces(act_cur, w_new, rank1_ref, p1_ref, m_ref, p0_ref, key_off),
                [functools.partial(dot2, rows) for rows in chunks(vt_ref.shape[0], PEER_DOT2_ROWS)]):
            piece()

        @pl.when(last)
        def _():
            read_h(tile).wait()
            for c in range(tm // LANES):
                rows = slice(c * LANES, (c + 1) * LANES)
                obuf[rows, :] += acc[:, rows].T
            acc[...] = jnp.zeros(acc.shape, acc.dtype)
            write_o(tile).start()
            pl.when(tile == n_tiles - 1)(lambda: write_o(tile).wait())

    even = lax.rem(g, 2) == 0
    pl.when(even)(lambda: body(act0, act1, w1, w0, PEER_KEYS_PER_BLOCK))
    pl.when(jnp.logical_not(even))(lambda: body(act1, act0, w0, w1, 0))


PEER_SHIFT_SLOTS = 2


def _peer(h, hn, u, vt, tables, tm, shift=None):
    t, d = h.shape
    nb = u.shape[0] // PEER_TE
    assert nb % PEER_BLOCKS_PER_TABLE == 0
    n = (t // tm) * nb
    n_steps = n + 2
    blk1 = lambda g: jnp.minimum(g, n - 1)
    blk2 = lambda g: jnp.clip(g - 1, 0, n - 1)
    blk3 = lambda g: jnp.clip(g - 2, 0, n - 1)
    once = pl.Buffered(1)
    any_spec = pl.BlockSpec(memory_space=pl.ANY)
    tspec = pl.BlockSpec((PEER_HEADS, PEER_NKEYS, tm), lambda g: (0, 0, blk2(g) // nb), pipeline_mode=once)
    kspec = pl.BlockSpec((PEER_HEADS, PEER_TABLE_KEYS, tm),
                         lambda g: (0, (blk2(g) % nb) // PEER_BLOCKS_PER_TABLE, blk2(g) // nb))
    in_specs = [any_spec,
                pl.BlockSpec((tm, d), lambda g: (blk1(g) // nb, 0), pipeline_mode=once),
                pl.BlockSpec((PEER_TE, d), lambda g: (blk1(g) % nb, 0)),
                pl.BlockSpec((None, d, PEER_TE), lambda g: (blk3(g) % nb, 0, 0)), tspec, tspec, kspec, kspec]
    out_specs = [any_spec]
    out_shape = [jax.ShapeDtypeStruct((t, d), F32)]
    scratch = [pltpu.VMEM((PEER_TE, tm), F32), pltpu.VMEM((PEER_TE, tm), F32),
               pltpu.VMEM((PEER_TE, tm), BF16), pltpu.VMEM((PEER_TE, tm), BF16), pltpu.VMEM((d, tm), F32),
               pltpu.VMEM((tm, d), F32), pltpu.SemaphoreType.DMA(()), pltpu.SemaphoreType.DMA(())]
    args = [h, hn, u, vt, *tables]
    if shift is not None:
        c, nw = shift
        parts = (n_steps - PEER_SHIFT_SLOTS + 1) // c.shape[0]
        assert parts >= 1, "more sequences than grid steps"
        while c.shape[1] % parts:
            parts -= 1
        in_specs += [any_spec, any_spec]
        out_specs.append(any_spec)
        out_shape.append(jax.ShapeDtypeStruct(c.shape, c.dtype))
        dma_sems = pltpu.SemaphoreType.DMA((PEER_SHIFT_SLOTS,))
        scratch += [pltpu.VMEM((PEER_SHIFT_SLOTS, 1, c.shape[1] // parts) + tuple(c.shape[2:]), c.dtype),
                    dma_sems, dma_sems, dma_sems]
        args += [c, nw]
    out = pl.pallas_call(
        functools.partial(_peer_kernel, n_blocks=nb, n_steps=n_steps, with_shift=shift is not None),
        grid=(n_steps,),
        in_specs=in_specs,
        out_specs=out_specs,
        out_shape=out_shape,
        scratch_shapes=scratch,
        compiler_params=_params(1),
        name="peer",
    )(*args)
    return out if shift is not None else out[0]


def _alibi_slopes():
    return 2.0 ** (-8.0 * jnp.arange(1, N_ALIBI_HEADS + 1, dtype=F32) / N_ALIBI_HEADS)


def _subkey_matrix(subkeys):
    two, nk, dh = subkeys.shape
    eye = jnp.eye(PEER_HEADS * two, dtype=subkeys.dtype).reshape(PEER_HEADS, two, PEER_HEADS, two)
    sk = jnp.einsum("hcgb,cnd->hcngbd", eye, subkeys)
    return sk.reshape(PEER_HEADS * two * nk, PEER_HEADS * two * dh)


def _tail(x2d, oa, ob, zg, w, tm_block, tr, tm_peer, route_shift=None, peer_shift=None):
    h, hn = _block(x2d, oa, ob, zg, w["wa"], w["wb"], w["wo"], w["n2"], tm_block)
    routed = _route(hn, w["wq"], w["sk"], tr, route_shift)
    tables, shifted = routed if route_shift is not None else (routed, None)
    y = _peer(h, hn, w["u"], w["vt"], tables, tm_peer, peer_shift)
    if peer_shift is not None:
        return (y[0], shifted, y[1]) if route_shift is not None else y
    return (y, shifted) if route_shift is not None else y


def kernel(x_prompt, x_sample, cache_a_kv, cache_b1_kv, cache_b2_kv, cache_b3_kv, norm1_w, w_in, q_norm_a,
           k_norm_a, sink_a, q_norm_b, k_norm_b, w_branch_a, w_branch_b, w_out, norm2_w, peer_wq,
           peer_subkeys, peer_u, peer_v):
    assert norm1_w.shape[0] == 1, "single layer"
    b, s, d = x_prompt.shape
    n_dec = x_sample.shape[0]
    assert x_sample.shape[1] == 1

    slopes = _alibi_slopes()
    slopes_a, slopes_b = slopes[:A_Q_HEADS], slopes[A_Q_HEADS:]
    sink = sink_a[0].astype(F32)
    ones = jnp.ones((HEAD_DIM,), F32)
    col_w = jnp.concatenate(
        [jnp.tile(q_norm_a[0], A_Q_HEADS), jnp.tile(k_norm_a[0], A_KV_HEADS), jnp.tile(ones, A_KV_HEADS),
         jnp.tile(q_norm_b[0], B_HEADS)]
        + [jnp.tile(k_norm_b[0], B_HEADS_PER_GROUP), jnp.tile(ones, B_HEADS_PER_GROUP)] * N_B_GROUPS
    ).astype(F32)
    normed = lambda flag, nh: jnp.full((nh * HEAD_DIM,), flag, F32)
    col_flag = jnp.concatenate(
        [normed(1.0, A_Q_HEADS + A_KV_HEADS), normed(0.0, A_KV_HEADS), normed(1.0, B_HEADS)]
        + [normed(1.0, B_HEADS_PER_GROUP), normed(0.0, B_HEADS_PER_GROUP)] * N_B_GROUPS)
    col_w = jnp.stack([col_w, col_flag])
    gate_w = jnp.zeros((2, GATE_W), F32)
    n1 = norm1_w[0].astype(F32)[None]
    w_cols = lambda h0, nh: w_in[0, :, h0 * HEAD_DIM:(h0 + nh) * HEAD_DIM]
    ref_kb, ref_vb = HEAD_QB + B_HEADS, HEAD_QB + 2 * B_HEADS
    w_qkv = jnp.concatenate(
        [w_cols(0, ref_kb)] + [w_cols(base + g * B_HEADS_PER_GROUP, B_HEADS_PER_GROUP)
                               for g in range(N_B_GROUPS) for base in (ref_kb, ref_vb)], axis=1).astype(BF16)
    w_gate = w_in[0, :, QKV_W:].astype(BF16)
    w = dict(wa=w_branch_a[0].astype(BF16), wb=w_branch_b[0].astype(BF16), wo=w_out[0].astype(BF16),
             n2=norm2_w[0].astype(F32)[None], wq=peer_wq[0].astype(BF16),
             sk=_subkey_matrix(peer_subkeys[0]).astype(BF16),
             u=peer_u[0].astype(BF16),
             vt=peer_v[0].astype(BF16).reshape(PEER_EXPERTS // PEER_TE, PEER_TE, d).transpose(0, 2, 1))

    xs = x_sample.reshape(n_dec, d)
    zs = _proj(xs, n1, w_qkv, col_w, False, F32, n_dec, "proj_qkv_s")
    zgs = _proj(xs, n1, w_gate, gate_w, True, BF16, n_dec, "proj_gate_s")
    zs3 = zs.reshape(n_dec, N_QKV_HEADS, HEAD_DIM)
    caches = (cache_a_kv[0], cache_b1_kv[0], cache_b2_kv[0], cache_b3_kv[0])
    kv_heads = [(HEAD_KA, A_KV_HEADS)] + [(_head_kb(g), B_HEADS_PER_GROUP) for g in range(N_B_GROUPS)]
    news = [zs3[:, h0:h0 + 2 * nh].reshape(n_dec, 2, nh, HEAD_DIM) for h0, nh in kv_heads]

    xp = x_prompt.reshape(b * s, d)
    zq = _proj(xp, n1, w_qkv, col_w, False, F32, 1024, "proj_qkv")
    zg = _proj(xp, n1, w_gate, gate_w, True, BF16, 1024, "proj_gate")
    zq3 = zq.reshape(b, s, QKV_W)
    oa = _attn_a(zq3, slopes_a, sink).reshape(b * s, A_Q_HEADS * HEAD_DIM)
    ob = _attn_b(zq3, slopes_b).reshape(b * s, B_HEADS_PER_GROUP * HEAD_DIM)
    y_prompt, new_b2, new_b3 = _tail(xp, oa, ob, zg, w, 256, 256, 1024, route_shift=(caches[2], news[2]),
                                     peer_shift=(caches[3], news[3]))
    y_prompt = y_prompt.reshape(b, s, d)

    def window(h0, nh, length):
        rows = zq3[:, s - length:, h0 * HEAD_DIM:(h0 + 2 * nh) * HEAD_DIM]
        return rows.reshape(1, b, length, 2, nh, HEAD_DIM)

    kv_prompt = [window(h0, nh, min(win, s))
                 for (h0, nh), win in zip(kv_heads, (A_WINDOW,) + tuple(w_ for w_, _ in B_GROUPS))]

    oa_s, ob_s = _sample_attn(zs3, *caches, slopes_a, sink, slopes_b)
    y_sample = _tail(xs, oa_s.astype(BF16), ob_s.astype(BF16), zgs, w, n_dec, n_dec, n_dec).reshape(n_dec, 1, d)

    kv_sample = [o[None] for o in (*_cache_update(caches[:2], news[:2]), new_b2, new_b3)]

    return (y_prompt, y_sample, *kv_prompt, *kv_sample)
```

```python
import functools
import math

import jax
import jax.numpy as jnp
from jax import lax
from jax.experimental import pallas as pl
from jax.experimental.pallas import tpu as pltpu

F32 = jnp.float32
BF16 = jnp.bfloat16

D_MODEL = 2048
HEAD_DIM = 128
A_Q_HEADS = 8
A_KV_HEADS = 2
A_GQA = A_Q_HEADS // A_KV_HEADS
A_WINDOW = 128
B_GROUPS = ((128, 1), (512, 4), (2048, 16))
B_HEADS_PER_GROUP = 4
N_B_GROUPS = len(B_GROUPS)
B_HEADS = N_B_GROUPS * B_HEADS_PER_GROUP
BAND = 128
N_ALIBI_HEADS = A_Q_HEADS + B_HEADS
ATTN_SCALE = HEAD_DIM ** -0.5
PEER_HEADS = 8
PEER_NKEYS = 128
PEER_EXPERTS = PEER_NKEYS * PEER_NKEYS
PEER_DKEY = 128
PEER_TOPK = 16
NORM_EPS = 1e-6
NEG_INF = -1e30

QKV_W = (A_Q_HEADS + 2 * A_KV_HEADS + 3 * B_HEADS) * HEAD_DIM
GATE_W = 2 * D_MODEL
HEAD_QA, HEAD_KA, HEAD_VA = 0, A_Q_HEADS, A_Q_HEADS + A_KV_HEADS
HEAD_QB = A_Q_HEADS + 2 * A_KV_HEADS
HEAD_KVB = HEAD_QB + B_HEADS
N_QKV_HEADS = QKV_W // HEAD_DIM


def _head_kb(g):
    return HEAD_KVB + 2 * B_HEADS_PER_GROUP * g

VMEM_LIMIT_BYTES = 56 * 1024 * 1024
LANES = 128

NT_DIMS = (((1,), (1,)), ((), ()))
TN_DIMS = (((0,), (0,)), ((), ()))


def _params(n_grid_axes):
    return pltpu.CompilerParams(
        dimension_semantics=("arbitrary",) * n_grid_axes,
        vmem_limit_bytes=VMEM_LIMIT_BYTES)


def _rms(x):
    return x * lax.rsqrt(jnp.mean(x * x, axis=-1, keepdims=True) + NORM_EPS)


PROJ_TN = 2048
PROJ_SUB = 512


def _proj_kernel(x_ref, n1_ref, w_ref, cw_ref, o_ref, xn_ref, *, gate):
    @pl.when(pl.program_id(1) == 0)
    def _():
        xn_ref[...] = (_rms(x_ref[...]) * n1_ref[...]).astype(BF16)

    for c0 in range(0, o_ref.shape[1], PROJ_SUB):
        z = jnp.dot(xn_ref[...], w_ref[:, c0:c0 + PROJ_SUB], preferred_element_type=F32)
        if gate:
            o_ref[:, c0:c0 + PROJ_SUB] = jax.nn.sigmoid(z).astype(o_ref.dtype)
            continue
        for h0 in range(0, PROJ_SUB, HEAD_DIM):
            cs = slice(c0 + h0, c0 + h0 + HEAD_DIM)
            zh = z[:, h0:h0 + HEAD_DIM]
            o_ref[:, cs] = jnp.where(cw_ref[1:2, cs] != 0.0, _rms(zh) * cw_ref[0:1, cs], zh).astype(o_ref.dtype)


def _proj(x2d, n1, w, cw, gate, out_dtype, tm, name):
    t, d = x2d.shape
    n = w.shape[1]
    assert t % tm == 0 and n % PROJ_TN == 0
    return pl.pallas_call(
        functools.partial(_proj_kernel, gate=gate),
        grid=(t // tm, n // PROJ_TN),
        in_specs=[
            pl.BlockSpec((tm, d), lambda i, j: (i, 0)),
            pl.BlockSpec((1, d), lambda i, j: (0, 0)),
            pl.BlockSpec((d, PROJ_TN), lambda i, j: (0, j)),
            pl.BlockSpec((2, PROJ_TN), lambda i, j: (0, j)),
        ],
        out_specs=pl.BlockSpec((tm, PROJ_TN), lambda i, j: (i, j)),
        out_shape=jax.ShapeDtypeStruct((t, n), out_dtype),
        scratch_shapes=[pltpu.VMEM((tm, d), BF16)],
        compiler_params=_params(2),
        name=name,
    )(x2d, n1, w, cw)


def _band_softmax(q, k, v, slope_dist, mask, sink):
    s = lax.dot_general(q, k, NT_DIMS, preferred_element_type=F32) * ATTN_SCALE
    logits = jnp.where(mask, s - slope_dist, NEG_INF)
    m = jnp.max(logits, axis=-1, keepdims=True)
    if sink is not None:
        m = jnp.maximum(m, sink)
    p = jnp.exp(logits - m)
    denom = jnp.sum(p, axis=-1, keepdims=True)
    if sink is not None:
        denom = denom + jnp.exp(sink - m)
    o = jnp.dot(p.astype(BF16), v, preferred_element_type=F32) / denom
    return o, m + jnp.log(denom)


def _band_geometry(has_prev):
    nk = 2 * BAND if has_prev else BAND
    qi = lax.broadcasted_iota(jnp.int32, (BAND, nk), 0)
    sj = lax.broadcasted_iota(jnp.int32, (BAND, nk), 1)
    dist = qi - sj + (BAND if has_prev else 0)
    return dist, sj


def _attn_a_kernel(slopes_ref, sink_ref, q_ref, k_ref, v_ref, *rest):
    o_ref = _ride_shift(rest, 1, pl.program_id(0) * A_KV_HEADS + pl.program_id(1))[0]
    kvh = pl.program_id(1)
    n_blocks = q_ref.shape[1] // BAND
    dist, sj = _band_geometry(True)
    in_window = (dist >= 0) & (dist <= A_WINDOW)
    distf = dist.astype(F32)

    def block(blk, carry):
        cur = pl.multiple_of(blk * BAND, BAND)
        prev = pl.multiple_of(jnp.maximum(blk - 1, 0) * BAND, BAND)
        k = jnp.concatenate([k_ref[0, pl.ds(prev, BAND), :], k_ref[0, pl.ds(cur, BAND), :]], axis=0).astype(BF16)
        v = jnp.concatenate([v_ref[0, pl.ds(prev, BAND), :], v_ref[0, pl.ds(cur, BAND), :]], axis=0).astype(BF16)
        mask = in_window & (sj >= jnp.where(blk > 0, 0, BAND))
        for g in range(A_GQA):
            cs = slice(g * HEAD_DIM, (g + 1) * HEAD_DIM)
            head = kvh * A_GQA + g
            q = q_ref[0, pl.ds(cur, BAND), cs].astype(BF16)
            o, _ = _band_softmax(q, k, v, slopes_ref[head] * distf, mask, sink_ref[head])
            o_ref[0, pl.ds(cur, BAND), cs] = o.astype(o_ref.dtype)
        return carry

    lax.fori_loop(0, n_blocks, block, 0)


def _attn_a(zq3, slopes_a, sink, shift):
    b, s, _ = zq3.shape
    gw = A_GQA * HEAD_DIM
    smem = pl.BlockSpec(memory_space=pltpu.SMEM)
    rider = _shift_rider(shift, b * A_KV_HEADS)
    return pl.pallas_call(
        _attn_a_kernel,
        grid=(b, A_KV_HEADS),
        in_specs=[
            smem, smem,
            pl.BlockSpec((1, s, gw), lambda bi, h: (bi, 0, h)),
            pl.BlockSpec((1, s, HEAD_DIM), lambda bi, h: (bi, 0, HEAD_KA + h)),
            pl.BlockSpec((1, s, HEAD_DIM), lambda bi, h: (bi, 0, HEAD_VA + h)),
        ] + rider.in_specs,
        out_specs=[pl.BlockSpec((1, s, gw), lambda bi, h: (bi, 0, h))] + rider.out_specs,
        out_shape=[jax.ShapeDtypeStruct((b, s, A_Q_HEADS * HEAD_DIM), BF16)] + rider.out_shape,
        scratch_shapes=rider.scratch,
        compiler_params=_params(2),
        name="attn_a",
    )(slopes_a, sink, zq3, zq3, zq3, *shift)


def _attn_b_kernel(slopes_ref, *refs):
    q_refs, k_refs, v_refs = refs[0:3], refs[3:6], refs[6:9]
    o_ref, og_ref, lse_ref = refs[9:12]
    h = pl.program_id(1)
    s_len = o_ref.shape[1]

    for g, (win, dil) in enumerate(B_GROUPS):
        n_blocks = s_len // (BAND * dil)
        slope = slopes_ref[g * B_HEADS_PER_GROUP + h]
        for has_prev in (False, True):
            dist, _ = _band_geometry(has_prev)
            mask = (dist >= 0) & (dist <= win // dil)
            slope_dist = slope * (dil * dist).astype(F32)
            for blk in range(n_blocks):
                if (blk > 0) != has_prev:
                    continue
                for r in range(dil):
                    def rows(ref, b0):
                        start = BAND * dil * b0 + r
                        if dil == 1:
                            return ref[0, pl.ds(start, BAND), :]
                        return ref[0, pl.ds(start, BAND, stride=dil), :]

                    q = rows(q_refs[g], blk).astype(BF16)
                    k = rows(k_refs[g], blk)
                    v = rows(v_refs[g], blk)
                    if has_prev:
                        k = jnp.concatenate([rows(k_refs[g], blk - 1), k], axis=0)
                        v = jnp.concatenate([rows(v_refs[g], blk - 1), v], axis=0)
                    o, lse = _band_softmax(q, k.astype(BF16), v.astype(BF16), slope_dist, mask, None)
                    start = BAND * dil * blk + r
                    idx = pl.ds(start, BAND) if dil == 1 else pl.ds(start, BAND, stride=dil)
                    og_ref[g, idx, :] = o
                    lse_ref[g, idx, :] = jnp.broadcast_to(lse, (BAND, HEAD_DIM))

    def combine(c, carry):
        rs = pl.ds(pl.multiple_of(c * BAND, BAND), BAND)
        lses = [lse_ref[g, rs, :] for g in range(N_B_GROUPS)]
        mx = functools.reduce(jnp.maximum, lses)
        ws = [jnp.exp(l - mx) for l in lses]
        acc = functools.reduce(lambda a, b_: a + b_, [w * og_ref[g, rs, :] for g, w in enumerate(ws)])
        o_ref[0, rs, :] = (acc / functools.reduce(lambda a, b_: a + b_, ws)).astype(o_ref.dtype)
        return carry

    lax.fori_loop(0, s_len // BAND, combine, 0)


def _attn_b(zq3, slopes_b):
    b, s, _ = zq3.shape
    smem = pl.BlockSpec(memory_space=pltpu.SMEM)

    def head_spec(first_head):
        return [pl.BlockSpec((1, s, HEAD_DIM), lambda bi, h, c=first_head(g): (bi, 0, c + h))
                for g in range(N_B_GROUPS)]

    return pl.pallas_call(
        _attn_b_kernel,
        grid=(b, B_HEADS_PER_GROUP),
        in_specs=[smem] + head_spec(lambda g: HEAD_QB + g * B_HEADS_PER_GROUP) + head_spec(_head_kb)
        + head_spec(lambda g: _head_kb(g) + B_HEADS_PER_GROUP),
        out_specs=pl.BlockSpec((1, s, HEAD_DIM), lambda bi, h: (bi, 0, h)),
        out_shape=jax.ShapeDtypeStruct((b, s, B_HEADS_PER_GROUP * HEAD_DIM), BF16),
        scratch_shapes=[pltpu.VMEM((N_B_GROUPS, s, HEAD_DIM), F32),
                        pltpu.VMEM((N_B_GROUPS, s, HEAD_DIM), F32)],
        compiler_params=_params(2),
        name="attn_b",
    )(slopes_b, *([zq3] * 9))


SAMPLE_BS = 4
SUBLANES = 8
MASKED = 1e30


def _window_read(q8, kv8, c, bias, bias_new, sink8, hk):
    per_tile = SUBLANES // (2 * hk)
    row = lax.broadcasted_iota(jnp.int32, (1, 1, SUBLANES, HEAD_DIM), 2)
    is_key = lax.rem(row, 2 * hk) < hk
    first_key = row < hk
    first_value = (row >= hk) & (row < 2 * hk)
    roll = lambda x, shift: pltpu.roll(x, shift, 2)

    def over_positions(x, op):
        for i in range(1, per_tile):
            x = op(x, roll(x, i * 2 * hk))
        return x

    q8 = q8 * ATTN_SCALE
    logits = jnp.sum(c * q8, axis=-1, keepdims=True) - bias
    logit_new = jnp.sum(kv8 * q8, axis=-1, keepdims=True) - bias_new
    mx = over_positions(jnp.max(logits, axis=1, keepdims=True), jnp.maximum)
    mx = jnp.maximum(mx, logit_new)
    if sink8 is not None:
        mx = jnp.maximum(mx, sink8)
    mx = jnp.where(is_key, mx, 0.0)
    p = jnp.exp(logits - mx)
    p_new = jnp.where(first_key, jnp.exp(logit_new - mx), 0.0)
    den = over_positions(jnp.sum(p, axis=1, keepdims=True), jnp.add) + p_new
    if sink8 is not None:
        den = den + jnp.where(first_key, jnp.exp(sink8 - mx), 0.0)
    acc = over_positions(jnp.sum(roll(p, hk) * c, axis=1, keepdims=True), jnp.add) + roll(p_new, hk) * kv8
    o8 = acc / jnp.where(first_value, roll(den, hk), 1.0)
    lse8 = roll(mx + jnp.log(jnp.where(first_key, den, 1.0)), hk)
    return o8, lse8


def _sample_attn_kernel(qa_ref, kva_ref, qb_ref, kvb_ref, ca_ref, cb1_ref, cb2_ref, cb3_ref,
                        bias_a_ref, new_a_ref, sink_ref, bias_b_ref, new_b_ref, oa_ref, ob_ref):
    kva = kva_ref[...][:, None]
    ca = ca_ref[...]
    for g in range(A_GQA):
        o8, _ = _window_read(qa_ref[:, g][:, None], kva, ca, bias_a_ref[g], new_a_ref[...], sink_ref[g], A_KV_HEADS)
        oa_ref[:, g] = o8[:, 0]
    outs, lses = [], []
    for g, c_ref in enumerate((cb1_ref, cb2_ref, cb3_ref)):
        o8, lse8 = _window_read(qb_ref[:, g][:, None], kvb_ref[:, g][:, None], c_ref[...], bias_b_ref[g],
                                new_b_ref[...], None, B_HEADS_PER_GROUP)
        outs.append(o8[:, 0])
        lses.append(lse8[:, 0])
    mx = functools.reduce(jnp.maximum, lses)
    ws = [jnp.exp(l - mx) for l in lses]
    acc = functools.reduce(lambda a, b_: a + b_, [w * o for w, o in zip(ws, outs)])
    ob_ref[...] = acc / functools.reduce(lambda a, b_: a + b_, ws)


def _tile_rows(x, hk):
    per_tile = SUBLANES // (2 * hk)
    pos = jnp.concatenate([x, jnp.zeros_like(x)], axis=-2)
    return jnp.concatenate([pos] * per_tile, axis=-2)


def _key_row_bias(dist, slopes, hk):
    key = dist[:, :, None] * slopes[None, None, :]
    rows = jnp.concatenate([key, jnp.full_like(key, MASKED)], axis=-1).reshape(dist.shape[0], SUBLANES)
    return jnp.broadcast_to(rows[:, :, None], rows.shape + (HEAD_DIM,))


def _sample_attn(zs3, cache_a, cache_b1, cache_b2, cache_b3, slopes_a, sink, slopes_b):
    n = zs3.shape[0]
    bs = SAMPLE_BS
    f32 = lambda x: x.astype(F32)

    def tiles(c, dil):
        nb, l, two, hh, hd = c.shape
        assert l == A_WINDOW * dil
        per_tile = SUBLANES // (two * hh)
        view = c.reshape(nb, l // (dil * per_tile), dil * SUBLANES, hd)
        return view, pl.BlockSpec((bs, view.shape[1], SUBLANES, hd), lambda i: (i, 0, 0, 0))

    ca, ca_spec = tiles(cache_a, 1)
    cb = [tiles(c, dil) for c, (_, dil) in zip((cache_b1, cache_b2, cache_b3), B_GROUPS)]

    heads_a = lambda h0: zs3[:, h0:h0 + A_KV_HEADS]
    qa = zs3[:, HEAD_QA:HEAD_QA + A_Q_HEADS].reshape(n, A_KV_HEADS, A_GQA, HEAD_DIM).transpose(0, 2, 1, 3)
    qa8 = _tile_rows(qa, A_KV_HEADS)
    kva8 = jnp.concatenate([heads_a(HEAD_KA), heads_a(HEAD_VA)] * (SUBLANES // (2 * A_KV_HEADS)), axis=1)
    qb = zs3[:, HEAD_QB:HEAD_QB + B_HEADS].reshape(n, N_B_GROUPS, B_HEADS_PER_GROUP, HEAD_DIM)
    qb8 = _tile_rows(qb, B_HEADS_PER_GROUP)
    kvb8 = zs3[:, HEAD_KVB:].reshape(n, N_B_GROUPS, 2 * B_HEADS_PER_GROUP, HEAD_DIM)

    per_a = SUBLANES // (2 * A_KV_HEADS)
    dist_a = (A_WINDOW - jnp.arange(A_WINDOW, dtype=F32)).reshape(A_WINDOW // per_a, per_a)
    slopes_ag = f32(slopes_a).reshape(A_KV_HEADS, A_GQA).T
    bias_a = jnp.stack([_key_row_bias(dist_a, slopes_ag[g], A_KV_HEADS) for g in range(A_GQA)])
    new_a = _key_row_bias(jnp.zeros((1, per_a), F32), slopes_ag[0], A_KV_HEADS)[0]
    sink8 = _tile_rows(f32(sink).reshape(A_KV_HEADS, A_GQA).T[:, :, None], A_KV_HEADS)
    key_rows = new_a[None, :, :1] == 0.0
    sink8 = jnp.broadcast_to(jnp.where(key_rows, sink8, -MASKED), (A_GQA, SUBLANES, HEAD_DIM))
    slopes_bg = f32(slopes_b).reshape(N_B_GROUPS, B_HEADS_PER_GROUP)
    dist_b = lambda dil: ((A_WINDOW - jnp.arange(A_WINDOW, dtype=F32)) * dil)[:, None]
    bias_b = jnp.stack([_key_row_bias(dist_b(dil), slopes_bg[g], B_HEADS_PER_GROUP)
                        for g, (_, dil) in enumerate(B_GROUPS)])
    new_b = _key_row_bias(jnp.zeros((1, 1), F32), slopes_bg[0], B_HEADS_PER_GROUP)[0]

    full = lambda a: pl.BlockSpec(a.shape, lambda i: (0,) * a.ndim)
    per_seq = lambda a: pl.BlockSpec((bs,) + a.shape[1:], lambda i: (i,) + (0,) * (a.ndim - 1))
    oa8, ob8 = pl.pallas_call(
        _sample_attn_kernel,
        grid=(n // bs,),
        in_specs=[per_seq(qa8), per_seq(kva8), per_seq(qb8), per_seq(kvb8),
                  ca_spec, cb[0][1], cb[1][1], cb[2][1],
                  full(bias_a), full(new_a), full(sink8), full(bias_b), full(new_b)],
        out_specs=[pl.BlockSpec((bs, A_GQA, SUBLANES, HEAD_DIM), lambda i: (i, 0, 0, 0)),
                   pl.BlockSpec((bs, SUBLANES, HEAD_DIM), lambda i: (i, 0, 0))],
        out_shape=[jax.ShapeDtypeStruct((n, A_GQA, SUBLANES, HEAD_DIM), F32),
                   jax.ShapeDtypeStruct((n, SUBLANES, HEAD_DIM), F32)],
        compiler_params=_params(1),
        name="sample_attn",
    )(qa8, kva8, qb8, kvb8, ca, cb[0][0], cb[1][0], cb[2][0], bias_a, new_a, sink8, bias_b, new_b)
    oa = oa8[:, :, A_KV_HEADS:2 * A_KV_HEADS].transpose(0, 2, 1, 3).reshape(n, A_Q_HEADS * HEAD_DIM)
    ob = ob8[:, B_HEADS_PER_GROUP:].reshape(n, B_HEADS_PER_GROUP * HEAD_DIM)
    return oa, ob


class _ShiftRing:
    def __init__(self, c, nw, o, buf, sem_body, sem_tail, sem_out):
        self.c, self.nw, self.o, self.buf = c, nw, o, buf
        self.sem_body, self.sem_tail, self.sem_out = sem_body, sem_tail, sem_out
        self.n_slots, self.nseq, self.rows = buf.shape[:3]
        n, self.l = c.shape[0], c.shape[1]
        assert n % self.nseq == 0 and self.l % self.rows == 0
        self.parts = self.l // self.rows
        self.n_chunks = (n // self.nseq) * self.parts

    def _where(self, k):
        return k % self.n_slots, pl.ds((k // self.parts) * self.nseq, self.nseq), (k % self.parts) * self.rows

    def _body(self, k):
        slot, seqs, r0 = self._where(k)
        return pltpu.make_async_copy(self.c.at[seqs, pl.ds(r0 + 1, self.rows - 1)],
                                     self.buf.at[slot, :, pl.ds(0, self.rows - 1)], self.sem_body.at[slot])

    def _tail_old(self, k):
        slot, seqs, r0 = self._where(k)
        return pltpu.make_async_copy(self.c.at[seqs, pl.ds(jnp.minimum(r0 + self.rows, self.l - 1), 1)],
                                     self.buf.at[slot, :, pl.ds(self.rows - 1, 1)], self.sem_tail.at[slot])

    def _tail_new(self, k):
        slot, seqs, _ = self._where(k)
        return pltpu.make_async_copy(self.nw.at[seqs], self.buf.at[slot, :, self.rows - 1], self.sem_tail.at[slot])

    def write(self, k):
        slot, seqs, r0 = self._where(k)
        return pltpu.make_async_copy(self.buf.at[slot], self.o.at[seqs, pl.ds(r0, self.rows)], self.sem_out.at[slot])

    def _read(self, k, action):
        action(self._body(k))
        if self.parts == 1:
            action(self._tail_new(k))
        else:
            is_last = (k % self.parts) == self.parts - 1
            pl.when(is_last)(lambda: action(self._tail_new(k)))
            pl.when(jnp.logical_not(is_last))(lambda: action(self._tail_old(k)))

    def start_read(self, k):
        self._read(k, lambda cp: cp.start())

    def wait_read(self, k):
        self._read(k, lambda cp: cp.wait())

    def step(self, k):
        n = self.n_chunks
        written = k + 1 - self.n_slots
        pl.when(k == 0)(lambda: self.start_read(0))
        pl.when((written >= 0) & (written < n))(lambda: self.write(jnp.clip(written, 0, n - 1)).wait())
        pl.when(k + 1 < n)(lambda: self.start_read(jnp.minimum(k + 1, n - 1)))

        @pl.when(k < n)
        def _():
            self.wait_read(jnp.minimum(k, n - 1))
            self.write(jnp.minimum(k, n - 1)).start()

    def drain(self, done_steps):
        for k in range(max(done_steps - self.n_slots + 1, 0), self.n_chunks):
            self.write(k).wait()


class _ShiftRider:
    def __init__(self, shift, n_steps):
        self.in_specs, self.out_specs, self.out_shape, self.scratch = [], [], [], []
        if shift is None:
            return
        c, _ = shift
        nseq = next(k for k in range(1, c.shape[0] + 1)
                    if c.shape[0] % k == 0 and c.shape[0] // k + RIDER_SLOTS - 1 <= n_steps)
        any_spec = pl.BlockSpec(memory_space=pl.ANY)
        dma_sems = pltpu.SemaphoreType.DMA((RIDER_SLOTS,))
        self.in_specs = [any_spec, any_spec]
        self.out_specs = [any_spec]
        self.out_shape = [jax.ShapeDtypeStruct(c.shape, c.dtype)]
        self.scratch = [pltpu.VMEM((RIDER_SLOTS, nseq) + tuple(c.shape[1:]), c.dtype), dma_sems, dma_sems, dma_sems]


RIDER_SLOTS = 2
_shift_rider = _ShiftRider


def _ride_shift(rest, n_outs, step):
    if len(rest) == n_outs:
        return rest
    c_ref, nw_ref = rest[:2]
    outs, oc_ref = rest[2:2 + n_outs], rest[2 + n_outs]
    _ShiftRing(c_ref, nw_ref, oc_ref, *rest[3 + n_outs:]).step(step)
    return outs


def _block_kernel(x_ref, oa_ref, ob_ref, ga_ref, gb_ref, wa_ref, wb_ref, wo_ref, n2_ref, *rest):
    h_ref, hn_ref = _ride_shift(rest, 2, pl.program_id(0))
    ya = jnp.dot(oa_ref[...], wa_ref[...], preferred_element_type=F32)
    yb = jnp.dot(ob_ref[...], wb_ref[...], preferred_element_type=F32)
    mix = (ga_ref[...].astype(F32) * ya + gb_ref[...].astype(F32) * yb).astype(BF16)
    h = x_ref[...] + jnp.dot(mix, wo_ref[...], preferred_element_type=F32)
    h_ref[...] = h
    hn_ref[...] = (_rms(h) * n2_ref[...]).astype(BF16)


def _block(x2d, oa, ob, zg, wa, wb, wo, n2, tm, shift=None):
    t, d = x2d.shape
    row = lambda w: pl.BlockSpec((tm, w), lambda i: (i, 0))
    full = lambda a: pl.BlockSpec(a.shape, lambda i: (0, 0))
    rider = _shift_rider(shift, t // tm)
    return pl.pallas_call(
        _block_kernel,
        grid=(t // tm,),
        in_specs=[row(d), row(oa.shape[1]), row(ob.shape[1]),
                  pl.BlockSpec((tm, d), lambda i: (i, 0)), pl.BlockSpec((tm, d), lambda i: (i, 1)),
                  full(wa), full(wb), full(wo), full(n2)] + rider.in_specs,
        out_specs=[row(d), row(d)] + rider.out_specs,
        out_shape=[jax.ShapeDtypeStruct((t, d), F32), jax.ShapeDtypeStruct((t, d), BF16)] + rider.out_shape,
        scratch_shapes=rider.scratch,
        compiler_params=_params(1),
        name="block",
    )(x2d, oa, ob, zg, zg, wa, wb, wo, n2, *(shift or ()))


RANK_NONE = float(PEER_NKEYS)
CAND_ROWS = tuple(PEER_TOPK // (a + 1) for a in range(PEER_TOPK))
CAND_PAD = tuple(-(-r // 8) * 8 for r in CAND_ROWS)


def _extract(work, n_take, break_ties):
    rows = lax.broadcasted_iota(jnp.int32, work.shape, 0).astype(F32)
    rank = jnp.full(work.shape, RANK_NONE, F32)
    vals = []
    for kk in range(n_take):
        mx = jnp.max(work, axis=0, keepdims=True)
        sel = work == mx
        if break_ties:
            first = jnp.min(jnp.where(sel, rows, float(work.shape[0])), axis=0, keepdims=True)
            sel = rows == first
        rank = jnp.where(sel, float(kk), rank)
        work = jnp.where(sel, -jnp.inf, work)
        vals.append(mx)
    return rank, jnp.concatenate(vals, axis=0)


def _extract_top(work, n_take):
    rank, vals = _extract(work, n_take, break_ties=False)
    ranked = jnp.sum(jnp.where(rank < RANK_NONE, 1.0, 0.0), axis=0, keepdims=True)
    return lax.cond(jnp.max(ranked) > n_take,
                    lambda: _extract(work, n_take, break_ties=True), lambda: (rank, vals))


ROUTE_SHIFT_SLOTS = 2
ROUTE_SHIFT_PER_STEP = 2


def _route_kernel(*refs, n_steps, with_shift):
    hn_ref, wq_ref, sk_ref = refs[:3]
    if with_shift:
        c_ref, nw_ref, rank1_ref, p1_ref, m_ref, p0_ref, oc_ref, s_scr, buf, sem_body, sem_tail, sem_out = refs[3:]
        ring = _ShiftRing(c_ref, nw_ref, oc_ref, buf, sem_body, sem_tail, sem_out)
        assert ring.n_chunks == n_steps * ROUTE_SHIFT_PER_STEP and PEER_HEADS % ROUTE_SHIFT_PER_STEP == 0
        first_chunk = pl.program_id(0) * ROUTE_SHIFT_PER_STEP
    else:
        rank1_ref, p1_ref, m_ref, p0_ref, s_scr = refs[3:]
        ring = None
    q = jnp.dot(hn_ref[...], wq_ref[...], preferred_element_type=F32).astype(BF16)
    s_scr[...] = lax.dot_general(sk_ref[...], q, NT_DIMS, preferred_element_type=F32)

    def head(hh, carry):
        if ring is not None:
            every = PEER_HEADS // ROUTE_SHIFT_PER_STEP
            pl.when(lax.rem(hh, every) == 0)(lambda: ring.step(first_chunk + hh // every))
        base = pl.multiple_of(hh * 2 * PEER_NKEYS, 2 * PEER_NKEYS)
        s0 = s_scr[pl.ds(base, PEER_NKEYS), :]
        s1 = s_scr[pl.ds(base + PEER_NKEYS, PEER_NKEYS), :]
        rank0, vals0 = _extract_top(s0, PEER_TOPK)
        rank1, vals1 = _extract_top(s1, PEER_TOPK)
        cand = []
        for a in range(PEER_TOPK):
            blk = vals0[a:a + 1] + vals1[:CAND_PAD[a]]
            rr = lax.broadcasted_iota(jnp.int32, blk.shape, 0)
            cand.append(jnp.where(rr < CAND_ROWS[a], blk, -jnp.inf))
        taken, best = _extract_top(jnp.concatenate(cand, axis=0), PEER_TOPK)
        zsum = jnp.sum(jnp.exp(best - best[0:1]), axis=0, keepdims=True)
        m = jnp.zeros_like(s0)
        off = 0
        for a in range(PEER_TOPK):
            cnt = jnp.sum(jnp.where(taken[off:off + CAND_PAD[a]] < RANK_NONE, 1.0, 0.0), axis=0, keepdims=True)
            m = jnp.where(rank0 == float(a), cnt, m)
            off += CAND_PAD[a]
        rank1_ref[hh] = rank1
        p1_ref[hh] = jnp.exp(s1 - vals1[0:1])
        m_ref[hh] = m
        p0_ref[hh] = jnp.exp(s0 - vals0[0:1]) / zsum
        return carry

    lax.fori_loop(0, PEER_HEADS, head, 0)
    if ring is not None:
        pl.when(pl.program_id(0) == n_steps - 1)(lambda: ring.drain(ring.n_chunks))


def _route(hn, wq, sk, tr, shift=None):
    t, d = hn.shape
    n_steps = t // tr
    table = jax.ShapeDtypeStruct((PEER_HEADS, PEER_NKEYS, t), F32)
    tspec = pl.BlockSpec((PEER_HEADS, PEER_NKEYS, tr), lambda i: (0, 0, i))
    full = lambda a: pl.BlockSpec(a.shape, lambda i: (0, 0))
    in_specs = [pl.BlockSpec((tr, d), lambda i: (i, 0)), full(wq), full(sk)]
    out_specs, out_shape = [tspec] * 4, [table] * 4
    scratch = [pltpu.VMEM((PEER_HEADS * 2 * PEER_NKEYS, tr), F32)]
    args = [hn, wq, sk]
    if shift is not None:
        c, nw = shift
        nseq = c.shape[0] // (n_steps * ROUTE_SHIFT_PER_STEP)
        assert nseq * n_steps * ROUTE_SHIFT_PER_STEP == c.shape[0]
        any_spec = pl.BlockSpec(memory_space=pl.ANY)
        in_specs += [any_spec, any_spec]
        out_specs = out_specs + [any_spec]
        out_shape = out_shape + [jax.ShapeDtypeStruct(c.shape, c.dtype)]
        dma_sems = pltpu.SemaphoreType.DMA((ROUTE_SHIFT_SLOTS,))
        scratch += [pltpu.VMEM((ROUTE_SHIFT_SLOTS, nseq) + tuple(c.shape[1:]), c.dtype), dma_sems, dma_sems, dma_sems]
        args += [c, nw]
    out = pl.pallas_call(
        functools.partial(_route_kernel, n_steps=n_steps, with_shift=shift is not None),
        grid=(n_steps,),
        in_specs=in_specs,
        out_specs=out_specs,
        out_shape=out_shape,
        scratch_shapes=scratch,
        compiler_params=_params(1),
        name="route",
    )(*args)
    return (out[:4], out[4]) if shift is not None else out


PEER_TE = 512
PEER_KEYS_PER_BLOCK = PEER_TE // PEER_NKEYS
PEER_TABLE_KEYS = 8
PEER_BLOCKS_PER_TABLE = PEER_TABLE_KEYS // PEER_KEYS_PER_BLOCK
assert PEER_BLOCKS_PER_TABLE == 2
SQRT_HALF = math.sqrt(0.5)


PEER_GATE_ROWS = 32


def _peer_gate_pieces(act_ref, w_ref, rank1_ref, p1_ref, m_ref, p0_ref, key_off):
    tm = act_ref.shape[1]

    def piece(ls, r0):
        gates = [jnp.zeros((PEER_GATE_ROWS, LANES), F32) for _ in range(PEER_KEYS_PER_BLOCK)]
        for hh in range(PEER_HEADS):
            r1 = rank1_ref[hh, r0:r0 + PEER_GATE_ROWS, ls]
            p1 = p1_ref[hh, r0:r0 + PEER_GATE_ROWS, ls]
            for j in range(PEER_KEYS_PER_BLOCK):
                kr = slice(key_off + j, key_off + j + 1)
                gates[j] = gates[j] + jnp.where(r1 < m_ref[hh, kr, ls], p1, 0.0) * p0_ref[hh, kr, ls]
        for j in range(PEER_KEYS_PER_BLOCK):
            rs = slice(j * PEER_NKEYS + r0, j * PEER_NKEYS + r0 + PEER_GATE_ROWS)
            a = act_ref[rs, ls]
            w_ref[rs, ls] = (gates[j] * (0.5 * a * (1.0 + lax.erf(a * SQRT_HALF)))).astype(BF16)

    return [functools.partial(piece, slice(c * LANES, (c + 1) * LANES), r0)
            for c in range(tm // LANES) for r0 in range(0, PEER_NKEYS, PEER_GATE_ROWS)]


PEER_DOT1_ROWS = 256
PEER_DOT2_ROWS = 512


def _interleave(*stages):
    tagged = [((i + 0.5) / len(st), si, piece) for si, st in enumerate(stages) for i, piece in enumerate(st)]
    return [piece for _, _, piece in sorted(tagged, key=lambda x: x[:2])]


def _peer_kernel(*refs, n_blocks, n_steps, with_shift):
    h_hbm, hn_ref, u_ref, vt_ref, rank1_ref, p1_ref, m_ref, p0_ref = refs[:8]
    g = pl.program_id(0)
    if with_shift:
        (c_ref, nw_ref, o_hbm, oc_ref, act0, act1, w0, w1, acc, obuf, sem_h, sem_o,
         buf, sem_body, sem_tail, sem_out) = refs[8:]
        ring = _ShiftRing(c_ref, nw_ref, oc_ref, buf, sem_body, sem_tail, sem_out)
        assert ring.n_chunks + ring.n_slots - 1 <= n_steps, "not enough grid steps to finish the shift"
        ring.step(g)
    else:
        o_hbm, act0, act1, w0, w1, acc, obuf, sem_h, sem_o = refs[8:]
    tm = obuf.shape[0]

    @pl.when(g == 0)
    def _():
        for ref in (act0, act1, w0, w1, acc):
            ref[...] = jnp.zeros(ref.shape, ref.dtype)

    blk3 = jnp.clip(g - 2, 0, n_steps - 3)
    tile = blk3 // n_blocks
    n_tiles = (n_steps - 2) // n_blocks
    first = (g >= 2) & (lax.rem(blk3, n_blocks) == 0)
    last = (g >= 2) & (lax.rem(blk3, n_blocks) == n_blocks - 1)
    tile_rows = lambda i: pl.ds(pl.multiple_of(i * tm, tm), tm)
    read_h = lambda i: pltpu.make_async_copy(h_hbm.at[tile_rows(i)], obuf, sem_h)
    write_o = lambda i: pltpu.make_async_copy(obuf, o_hbm.at[tile_rows(i)], sem_o)

    @pl.when(first)
    def _():
        pl.when(tile > 0)(lambda: write_o(jnp.maximum(tile - 1, 0)).wait())
        read_h(tile).start()

    def body(act_new, act_cur, w_new, w_old, key_off):
        def dot1(rows):
            act_new[rows, :] = lax.dot_general(u_ref[rows, :], hn_ref[...], NT_DIMS, preferred_element_type=F32)

        def dot2(rows):
            acc[rows, :] += jnp.dot(vt_ref[rows, :], w_old[...], preferred_element_type=F32)

        chunks = lambda n, size: [slice(r, r + size) for r in range(0, n, size)]
        for piece in _interleave(
                [functools.partial(dot1, rows) for rows in chunks(u_ref.shape[0], PEER_DOT1_ROWS)],
                _peer_gate_pieces(act_cur, w_new, rank1_ref, p1_ref, m_ref, p0_ref, key_off),
                [functools.partial(dot2, rows) for rows in chunks(vt_ref.shape[0], PEER_DOT2_ROWS)]):
            piece()

        @pl.when(last)
        def _():
            read_h(tile).wait()
            for c in range(tm // LANES):
                rows = slice(c * LANES, (c + 1) * LANES)
                obuf[rows, :] += acc[:, rows].T
            acc[...] = jnp.zeros(acc.shape, acc.dtype)
            write_o(tile).start()
            pl.when(tile == n_tiles - 1)(lambda: write_o(tile).wait())

    even = lax.rem(g, 2) == 0
    pl.when(even)(lambda: body(act0, act1, w1, w0, PEER_KEYS_PER_BLOCK))
    pl.when(jnp.logical_not(even))(lambda: body(act1, act0, w0, w1, 0))


PEER_SHIFT_SLOTS = 2


def _peer(h, hn, u, vt, tables, tm, shift=None):
    t, d = h.shape
    nb = u.shape[0] // PEER_TE
    assert nb % PEER_BLOCKS_PER_TABLE == 0
    n = (t // tm) * nb
    n_steps = n + 2
    blk1 = lambda g: jnp.minimum(g, n - 1)
    blk2 = lambda g: jnp.clip(g - 1, 0, n - 1)
    blk3 = lambda g: jnp.clip(g - 2, 0, n - 1)
    once = pl.Buffered(1)
    any_spec = pl.BlockSpec(memory_space=pl.ANY)
    tspec = pl.BlockSpec((PEER_HEADS, PEER_NKEYS, tm), lambda g: (0, 0, blk2(g) // nb), pipeline_mode=once)
    kspec = pl.BlockSpec((PEER_HEADS, PEER_TABLE_KEYS, tm),
                         lambda g: (0, (blk2(g) % nb) // PEER_BLOCKS_PER_TABLE, blk2(g) // nb))
    in_specs = [any_spec,
                pl.BlockSpec((tm, d), lambda g: (blk1(g) // nb, 0), pipeline_mode=once),
                pl.BlockSpec((PEER_TE, d), lambda g: (blk1(g) % nb, 0)),
                pl.BlockSpec((None, d, PEER_TE), lambda g: (blk3(g) % nb, 0, 0)), tspec, tspec, kspec, kspec]
    out_specs = [any_spec]
    out_shape = [jax.ShapeDtypeStruct((t, d), F32)]
    scratch = [pltpu.VMEM((PEER_TE, tm), F32), pltpu.VMEM((PEER_TE, tm), F32),
               pltpu.VMEM((PEER_TE, tm), BF16), pltpu.VMEM((PEER_TE, tm), BF16), pltpu.VMEM((d, tm), F32),
               pltpu.VMEM((tm, d), F32), pltpu.SemaphoreType.DMA(()), pltpu.SemaphoreType.DMA(())]
    args = [h, hn, u, vt, *tables]
    if shift is not None:
        c, nw = shift
        parts = (n_steps - PEER_SHIFT_SLOTS + 1) // c.shape[0]
        assert parts >= 1, "more sequences than grid steps"
        while c.shape[1] % parts:
            parts -= 1
        in_specs += [any_spec, any_spec]
        out_specs.append(any_spec)
        out_shape.append(jax.ShapeDtypeStruct(c.shape, c.dtype))
        dma_sems = pltpu.SemaphoreType.DMA((PEER_SHIFT_SLOTS,))
        scratch += [pltpu.VMEM((PEER_SHIFT_SLOTS, 1, c.shape[1] // parts) + tuple(c.shape[2:]), c.dtype),
                    dma_sems, dma_sems, dma_sems]
        args += [c, nw]
    out = pl.pallas_call(
        functools.partial(_peer_kernel, n_blocks=nb, n_steps=n_steps, with_shift=shift is not None),
        grid=(n_steps,),
        in_specs=in_specs,
        out_specs=out_specs,
        out_shape=out_shape,
        scratch_shapes=scratch,
        compiler_params=_params(1),
        name="peer",
    )(*args)
    return out if shift is not None else out[0]


def _alibi_slopes():
    return 2.0 ** (-8.0 * jnp.arange(1, N_ALIBI_HEADS + 1, dtype=F32) / N_ALIBI_HEADS)


def _subkey_matrix(subkeys):
    two, nk, dh = subkeys.shape
    eye = jnp.eye(PEER_HEADS * two, dtype=subkeys.dtype).reshape(PEER_HEADS, two, PEER_HEADS, two)
    sk = jnp.einsum("hcgb,cnd->hcngbd", eye, subkeys)
    return sk.reshape(PEER_HEADS * two * nk, PEER_HEADS * two * dh)


def _tail(x2d, oa, ob, zg, w, tm_block, tr, tm_peer, shifts=(None, None, None)):
    h, hn, *s_block = _block(x2d, oa, ob, zg, w["wa"], w["wb"], w["wo"], w["n2"], tm_block, shifts[0])
    routed = _route(hn, w["wq"], w["sk"], tr, shifts[1])
    tables, s_route = routed if shifts[1] is not None else (routed, None)
    y = _peer(h, hn, w["u"], w["vt"], tables, tm_peer, shifts[2])
    y, s_peer = y if shifts[2] is not None else (y, None)
    return y, (s_block[0] if s_block else None, s_route, s_peer)


def kernel(x_prompt, x_sample, cache_a_kv, cache_b1_kv, cache_b2_kv, cache_b3_kv, norm1_w, w_in, q_norm_a,
           k_norm_a, sink_a, q_norm_b, k_norm_b, w_branch_a, w_branch_b, w_out, norm2_w, peer_wq,
           peer_subkeys, peer_u, peer_v):
    assert norm1_w.shape[0] == 1, "single layer"
    b, s, d = x_prompt.shape
    n_dec = x_sample.shape[0]
    assert x_sample.shape[1] == 1

    slopes = _alibi_slopes()
    slopes_a, slopes_b = slopes[:A_Q_HEADS], slopes[A_Q_HEADS:]
    sink = sink_a[0].astype(F32)
    ones = jnp.ones((HEAD_DIM,), F32)
    col_w = jnp.concatenate(
        [jnp.tile(q_norm_a[0], A_Q_HEADS), jnp.tile(k_norm_a[0], A_KV_HEADS), jnp.tile(ones, A_KV_HEADS),
         jnp.tile(q_norm_b[0], B_HEADS)]
        + [jnp.tile(k_norm_b[0], B_HEADS_PER_GROUP), jnp.tile(ones, B_HEADS_PER_GROUP)] * N_B_GROUPS
    ).astype(F32)
    normed = lambda flag, nh: jnp.full((nh * HEAD_DIM,), flag, F32)
    col_flag = jnp.concatenate(
        [normed(1.0, A_Q_HEADS + A_KV_HEADS), normed(0.0, A_KV_HEADS), normed(1.0, B_HEADS)]
        + [normed(1.0, B_HEADS_PER_GROUP), normed(0.0, B_HEADS_PER_GROUP)] * N_B_GROUPS)
    col_w = jnp.stack([col_w, col_flag])
    gate_w = jnp.zeros((2, GATE_W), F32)
    n1 = norm1_w[0].astype(F32)[None]
    w_cols = lambda h0, nh: w_in[0, :, h0 * HEAD_DIM:(h0 + nh) * HEAD_DIM]
    ref_kb, ref_vb = HEAD_QB + B_HEADS, HEAD_QB + 2 * B_HEADS
    w_qkv = jnp.concatenate(
        [w_cols(0, ref_kb)] + [w_cols(base + g * B_HEADS_PER_GROUP, B_HEADS_PER_GROUP)
                               for g in range(N_B_GROUPS) for base in (ref_kb, ref_vb)], axis=1).astype(BF16)
    w_gate = w_in[0, :, QKV_W:].astype(BF16)
    w = dict(wa=w_branch_a[0].astype(BF16), wb=w_branch_b[0].astype(BF16), wo=w_out[0].astype(BF16),
             n2=norm2_w[0].astype(F32)[None], wq=peer_wq[0].astype(BF16),
             sk=_subkey_matrix(peer_subkeys[0]).astype(BF16),
             u=peer_u[0].astype(BF16),
             vt=peer_v[0].astype(BF16).reshape(PEER_EXPERTS // PEER_TE, PEER_TE, d).transpose(0, 2, 1))

    xs = x_sample.reshape(n_dec, d)
    zs = _proj(xs, n1, w_qkv, col_w, False, F32, n_dec, "proj_qkv_s")
    zgs = _proj(xs, n1, w_gate, gate_w, True, BF16, n_dec, "proj_gate_s")
    zs3 = zs.reshape(n_dec, N_QKV_HEADS, HEAD_DIM)
    caches = (cache_a_kv[0], cache_b1_kv[0], cache_b2_kv[0], cache_b3_kv[0])
    kv_heads = [(HEAD_KA, A_KV_HEADS)] + [(_head_kb(g), B_HEADS_PER_GROUP) for g in range(N_B_GROUPS)]
    news = [zs3[:, h0:h0 + 2 * nh].reshape(n_dec, 2, nh, HEAD_DIM) for h0, nh in kv_heads]

    xp = x_prompt.reshape(b * s, d)
    zq = _proj(xp, n1, w_qkv, col_w, False, F32, 1024, "proj_qkv")
    zg = _proj(xp, n1, w_gate, gate_w, True, BF16, 1024, "proj_gate")
    zq3 = zq.reshape(b, s, QKV_W)
    oa, new_a = _attn_a(zq3, slopes_a, sink, (caches[0], news[0]))
    oa = oa.reshape(b * s, A_Q_HEADS * HEAD_DIM)
    ob = _attn_b(zq3, slopes_b).reshape(b * s, B_HEADS_PER_GROUP * HEAD_DIM)
    y_prompt, new_b = _tail(xp, oa, ob, zg, w, 256, 256, 1024, shifts=tuple(zip(caches[1:], news[1:])))
    y_prompt = y_prompt.reshape(b, s, d)

    def window(h0, nh, length):
        rows = zq3[:, s - length:, h0 * HEAD_DIM:(h0 + 2 * nh) * HEAD_DIM]
        return rows.reshape(1, b, length, 2, nh, HEAD_DIM)

    kv_prompt = [window(h0, nh, min(win, s))
                 for (h0, nh), win in zip(kv_heads, (A_WINDOW,) + tuple(w_ for w_, _ in B_GROUPS))]

    oa_s, ob_s = _sample_attn(zs3, *caches, slopes_a, sink, slopes_b)
    y_sample, _ = _tail(xs, oa_s.astype(BF16), ob_s.astype(BF16), zgs, w, n_dec, n_dec, n_dec)
    y_sample = y_sample.reshape(n_dec, 1, d)

    kv_sample = [o[None] for o in (new_a, *new_b)]

    return (y_prompt, y_sample, *kv_prompt, *kv_sample)
```

```python
import functools
import math

import jax
import jax.numpy as jnp
from jax import lax
from jax.experimental import pallas as pl
from jax.experimental.pallas import tpu as pltpu

F32 = jnp.float32
BF16 = jnp.bfloat16

D_MODEL = 2048
HEAD_DIM = 128
A_Q_HEADS = 8
A_KV_HEADS = 2
A_GQA = A_Q_HEADS // A_KV_HEADS
A_WINDOW = 128
B_GROUPS = ((128, 1), (512, 4), (2048, 16))
B_HEADS_PER_GROUP = 4
N_B_GROUPS = len(B_GROUPS)
B_HEADS = N_B_GROUPS * B_HEADS_PER_GROUP
BAND = 128
N_ALIBI_HEADS = A_Q_HEADS + B_HEADS
ATTN_SCALE = HEAD_DIM ** -0.5
PEER_HEADS = 8
PEER_NKEYS = 128
PEER_EXPERTS = PEER_NKEYS * PEER_NKEYS
PEER_DKEY = 128
PEER_TOPK = 16
NORM_EPS = 1e-6
NEG_INF = -1e30

QKV_W = (A_Q_HEADS + 2 * A_KV_HEADS + 3 * B_HEADS) * HEAD_DIM
GATE_W = 2 * D_MODEL
HEAD_QA, HEAD_KA, HEAD_VA = 0, A_Q_HEADS, A_Q_HEADS + A_KV_HEADS
HEAD_QB = A_Q_HEADS + 2 * A_KV_HEADS
HEAD_KVB = HEAD_QB + B_HEADS
N_QKV_HEADS = QKV_W // HEAD_DIM


def _head_kb(g):
    return HEAD_KVB + 2 * B_HEADS_PER_GROUP * g

VMEM_LIMIT_BYTES = 56 * 1024 * 1024
LANES = 128

NT_DIMS = (((1,), (1,)), ((), ()))
TN_DIMS = (((0,), (0,)), ((), ()))


def _params(n_grid_axes):
    return pltpu.CompilerParams(
        dimension_semantics=("arbitrary",) * n_grid_axes,
        vmem_limit_bytes=VMEM_LIMIT_BYTES)


def _rms(x):
    return x * lax.rsqrt(jnp.mean(x * x, axis=-1, keepdims=True) + NORM_EPS)


PROJ_TN = 2048
PROJ_SUB = 512


def _proj_kernel(x_ref, n1_ref, w_ref, cw_ref, o_ref, xn_ref, *, gate):
    @pl.when(pl.program_id(1) == 0)
    def _():
        xn_ref[...] = (_rms(x_ref[...]) * n1_ref[...]).astype(BF16)

    for c0 in range(0, o_ref.shape[1], PROJ_SUB):
        z = jnp.dot(xn_ref[...], w_ref[:, c0:c0 + PROJ_SUB], preferred_element_type=F32)
        if gate:
            o_ref[:, c0:c0 + PROJ_SUB] = jax.nn.sigmoid(z).astype(o_ref.dtype)
            continue
        for h0 in range(0, PROJ_SUB, HEAD_DIM):
            cs = slice(c0 + h0, c0 + h0 + HEAD_DIM)
            zh = z[:, h0:h0 + HEAD_DIM]
            o_ref[:, cs] = jnp.where(cw_ref[1:2, cs] != 0.0, _rms(zh) * cw_ref[0:1, cs], zh).astype(o_ref.dtype)


def _proj(x2d, n1, w, cw, gate, out_dtype, tm, name):
    t, d = x2d.shape
    n = w.shape[1]
    assert t % tm == 0 and n % PROJ_TN == 0
    return pl.pallas_call(
        functools.partial(_proj_kernel, gate=gate),
        grid=(t // tm, n // PROJ_TN),
        in_specs=[
            pl.BlockSpec((tm, d), lambda i, j: (i, 0)),
            pl.BlockSpec((1, d), lambda i, j: (0, 0)),
            pl.BlockSpec((d, PROJ_TN), lambda i, j: (0, j)),
            pl.BlockSpec((2, PROJ_TN), lambda i, j: (0, j)),
        ],
        out_specs=pl.BlockSpec((tm, PROJ_TN), lambda i, j: (i, j)),
        out_shape=jax.ShapeDtypeStruct((t, n), out_dtype),
        scratch_shapes=[pltpu.VMEM((tm, d), BF16)],
        compiler_params=_params(2),
        name=name,
    )(x2d, n1, w, cw)


def _band_softmax(q, k, v, slope_dist, mask, sink):
    s = lax.dot_general(q, k, NT_DIMS, preferred_element_type=F32) * ATTN_SCALE
    logits = jnp.where(mask, s - slope_dist, NEG_INF)
    m = jnp.max(logits, axis=-1, keepdims=True)
    if sink is not None:
        m = jnp.maximum(m, sink)
    p = jnp.exp(logits - m)
    denom = jnp.sum(p, axis=-1, keepdims=True)
    if sink is not None:
        denom = denom + jnp.exp(sink - m)
    o = jnp.dot(p.astype(BF16), v, preferred_element_type=F32) / denom
    return o, m + jnp.log(denom)


def _band_geometry(has_prev):
    nk = 2 * BAND if has_prev else BAND
    qi = lax.broadcasted_iota(jnp.int32, (BAND, nk), 0)
    sj = lax.broadcasted_iota(jnp.int32, (BAND, nk), 1)
    dist = qi - sj + (BAND if has_prev else 0)
    return dist, sj


def _attn_a_kernel(slopes_ref, sink_ref, q_ref, k_ref, v_ref, *rest):
    o_ref = _ride_shift(rest, 1, pl.program_id(0) * A_KV_HEADS + pl.program_id(1))[0]
    kvh = pl.program_id(1)
    n_blocks = q_ref.shape[1] // BAND
    dist, sj = _band_geometry(True)
    in_window = (dist >= 0) & (dist <= A_WINDOW)
    distf = dist.astype(F32)

    def block(blk, carry):
        cur = pl.multiple_of(blk * BAND, BAND)
        prev = pl.multiple_of(jnp.maximum(blk - 1, 0) * BAND, BAND)
        k = jnp.concatenate([k_ref[0, pl.ds(prev, BAND), :], k_ref[0, pl.ds(cur, BAND), :]], axis=0).astype(BF16)
        v = jnp.concatenate([v_ref[0, pl.ds(prev, BAND), :], v_ref[0, pl.ds(cur, BAND), :]], axis=0).astype(BF16)
        mask = in_window & (sj >= jnp.where(blk > 0, 0, BAND))
        for g in range(A_GQA):
            cs = slice(g * HEAD_DIM, (g + 1) * HEAD_DIM)
            head = kvh * A_GQA + g
            q = q_ref[0, pl.ds(cur, BAND), cs].astype(BF16)
            o, _ = _band_softmax(q, k, v, slopes_ref[head] * distf, mask, sink_ref[head])
            o_ref[0, pl.ds(cur, BAND), cs] = o.astype(o_ref.dtype)
        return carry

    lax.fori_loop(0, n_blocks, block, 0)


def _attn_a(zq3, slopes_a, sink, shift):
    b, s, _ = zq3.shape
    gw = A_GQA * HEAD_DIM
    smem = pl.BlockSpec(memory_space=pltpu.SMEM)
    rider = _shift_rider(shift, b * A_KV_HEADS)
    return pl.pallas_call(
        _attn_a_kernel,
        grid=(b, A_KV_HEADS),
        in_specs=[
            smem, smem,
            pl.BlockSpec((1, s, gw), lambda bi, h: (bi, 0, h)),
            pl.BlockSpec((1, s, HEAD_DIM), lambda bi, h: (bi, 0, HEAD_KA + h)),
            pl.BlockSpec((1, s, HEAD_DIM), lambda bi, h: (bi, 0, HEAD_VA + h)),
        ] + rider.in_specs,
        out_specs=[pl.BlockSpec((1, s, gw), lambda bi, h: (bi, 0, h))] + rider.out_specs,
        out_shape=[jax.ShapeDtypeStruct((b, s, A_Q_HEADS * HEAD_DIM), BF16)] + rider.out_shape,
        scratch_shapes=rider.scratch,
        compiler_params=_params(2),
        name="attn_a",
    )(slopes_a, sink, zq3, zq3, zq3, *shift)


def _attn_b_kernel(slopes_ref, *refs):
    q_refs, k_refs, v_refs = refs[0:3], refs[3:6], refs[6:9]
    if len(refs) > 12:
        c_ref, nw_ref, o_ref, oc_ref, og_ref, lse_ref = refs[9:15]
        _ShiftRing(c_ref, nw_ref, oc_ref, *refs[15:]).step(pl.program_id(0) * B_HEADS_PER_GROUP + pl.program_id(1))
    else:
        o_ref, og_ref, lse_ref = refs[9:12]
    h = pl.program_id(1)
    s_len = o_ref.shape[1]

    for g, (win, dil) in enumerate(B_GROUPS):
        n_blocks = s_len // (BAND * dil)
        slope = slopes_ref[g * B_HEADS_PER_GROUP + h]
        for has_prev in (False, True):
            dist, _ = _band_geometry(has_prev)
            mask = (dist >= 0) & (dist <= win // dil)
            slope_dist = slope * (dil * dist).astype(F32)
            for blk in range(n_blocks):
                if (blk > 0) != has_prev:
                    continue
                for r in range(dil):
                    def rows(ref, b0):
                        start = BAND * dil * b0 + r
                        if dil == 1:
                            return ref[0, pl.ds(start, BAND), :]
                        return ref[0, pl.ds(start, BAND, stride=dil), :]

                    q = rows(q_refs[g], blk).astype(BF16)
                    k = rows(k_refs[g], blk)
                    v = rows(v_refs[g], blk)
                    if has_prev:
                        k = jnp.concatenate([rows(k_refs[g], blk - 1), k], axis=0)
                        v = jnp.concatenate([rows(v_refs[g], blk - 1), v], axis=0)
                    o, lse = _band_softmax(q, k.astype(BF16), v.astype(BF16), slope_dist, mask, None)
                    start = BAND * dil * blk + r
                    idx = pl.ds(start, BAND) if dil == 1 else pl.ds(start, BAND, stride=dil)
                    og_ref[g, idx, :] = o
                    lse_ref[g, idx, :] = jnp.broadcast_to(lse, (BAND, HEAD_DIM))

    def combine(c, carry):
        rs = pl.ds(pl.multiple_of(c * BAND, BAND), BAND)
        lses = [lse_ref[g, rs, :] for g in range(N_B_GROUPS)]
        mx = functools.reduce(jnp.maximum, lses)
        ws = [jnp.exp(l - mx) for l in lses]
        acc = functools.reduce(lambda a, b_: a + b_, [w * og_ref[g, rs, :] for g, w in enumerate(ws)])
        o_ref[0, rs, :] = (acc / functools.reduce(lambda a, b_: a + b_, ws)).astype(o_ref.dtype)
        return carry

    lax.fori_loop(0, s_len // BAND, combine, 0)


def _attn_b(zq3, slopes_b, shift=None):
    b, s, _ = zq3.shape
    smem = pl.BlockSpec(memory_space=pltpu.SMEM)
    rider = _shift_rider(shift, b * B_HEADS_PER_GROUP)

    def head_spec(first_head):
        return [pl.BlockSpec((1, s, HEAD_DIM), lambda bi, h, c=first_head(g): (bi, 0, c + h))
                for g in range(N_B_GROUPS)]

    return pl.pallas_call(
        _attn_b_kernel,
        grid=(b, B_HEADS_PER_GROUP),
        in_specs=[smem] + head_spec(lambda g: HEAD_QB + g * B_HEADS_PER_GROUP) + head_spec(_head_kb)
        + head_spec(lambda g: _head_kb(g) + B_HEADS_PER_GROUP) + rider.in_specs,
        out_specs=[pl.BlockSpec((1, s, HEAD_DIM), lambda bi, h: (bi, 0, h))] + rider.out_specs,
        out_shape=[jax.ShapeDtypeStruct((b, s, B_HEADS_PER_GROUP * HEAD_DIM), BF16)] + rider.out_shape,
        scratch_shapes=[pltpu.VMEM((N_B_GROUPS, s, HEAD_DIM), F32),
                        pltpu.VMEM((N_B_GROUPS, s, HEAD_DIM), F32)] + rider.scratch,
        compiler_params=_params(2),
        name="attn_b",
    )(slopes_b, *([zq3] * 9), *(shift or ()))


SAMPLE_BS = 4
SUBLANES = 8
MASKED = 1e30


def _window_read(q8, kv8, c, bias, bias_new, sink8, hk):
    per_tile = SUBLANES // (2 * hk)
    row = lax.broadcasted_iota(jnp.int32, (1, 1, SUBLANES, HEAD_DIM), 2)
    is_key = lax.rem(row, 2 * hk) < hk
    first_key = row < hk
    first_value = (row >= hk) & (row < 2 * hk)
    roll = lambda x, shift: pltpu.roll(x, shift, 2)

    def over_positions(x, op):
        for i in range(1, per_tile):
            x = op(x, roll(x, i * 2 * hk))
        return x

    q8 = q8 * ATTN_SCALE
    logits = jnp.sum(c * q8, axis=-1, keepdims=True) - bias
    logit_new = jnp.sum(kv8 * q8, axis=-1, keepdims=True) - bias_new
    mx = over_positions(jnp.max(logits, axis=1, keepdims=True), jnp.maximum)
    mx = jnp.maximum(mx, logit_new)
    if sink8 is not None:
        mx = jnp.maximum(mx, sink8)
    mx = jnp.where(is_key, mx, 0.0)
    p = jnp.exp(logits - mx)
    p_new = jnp.where(first_key, jnp.exp(logit_new - mx), 0.0)
    den = over_positions(jnp.sum(p, axis=1, keepdims=True), jnp.add) + p_new
    if sink8 is not None:
        den = den + jnp.where(first_key, jnp.exp(sink8 - mx), 0.0)
    acc = over_positions(jnp.sum(roll(p, hk) * c, axis=1, keepdims=True), jnp.add) + roll(p_new, hk) * kv8
    o8 = acc / jnp.where(first_value, roll(den, hk), 1.0)
    lse8 = roll(mx + jnp.log(jnp.where(first_key, den, 1.0)), hk)
    return o8, lse8


def _sample_attn_kernel(qa_ref, kva_ref, qb_ref, kvb_ref, ca_ref, cb1_ref, cb2_ref, cb3_ref,
                        bias_a_ref, new_a_ref, sink_ref, bias_b_ref, new_b_ref, oa_ref, ob_ref):
    kva = kva_ref[...][:, None]
    ca = ca_ref[...]
    for g in range(A_GQA):
        o8, _ = _window_read(qa_ref[:, g][:, None], kva, ca, bias_a_ref[g], new_a_ref[...], sink_ref[g], A_KV_HEADS)
        oa_ref[:, g] = o8[:, 0]
    outs, lses = [], []
    for g, c_ref in enumerate((cb1_ref, cb2_ref, cb3_ref)):
        o8, lse8 = _window_read(qb_ref[:, g][:, None], kvb_ref[:, g][:, None], c_ref[...], bias_b_ref[g],
                                new_b_ref[...], None, B_HEADS_PER_GROUP)
        outs.append(o8[:, 0])
        lses.append(lse8[:, 0])
    mx = functools.reduce(jnp.maximum, lses)
    ws = [jnp.exp(l - mx) for l in lses]
    acc = functools.reduce(lambda a, b_: a + b_, [w * o for w, o in zip(ws, outs)])
    ob_ref[...] = acc / functools.reduce(lambda a, b_: a + b_, ws)


def _tile_rows(x, hk):
    per_tile = SUBLANES // (2 * hk)
    pos = jnp.concatenate([x, jnp.zeros_like(x)], axis=-2)
    return jnp.concatenate([pos] * per_tile, axis=-2)


def _key_row_bias(dist, slopes, hk):
    key = dist[:, :, None] * slopes[None, None, :]
    rows = jnp.concatenate([key, jnp.full_like(key, MASKED)], axis=-1).reshape(dist.shape[0], SUBLANES)
    return jnp.broadcast_to(rows[:, :, None], rows.shape + (HEAD_DIM,))


def _sample_attn(zs3, cache_a, cache_b1, cache_b2, cache_b3, slopes_a, sink, slopes_b):
    n = zs3.shape[0]
    bs = SAMPLE_BS
    f32 = lambda x: x.astype(F32)

    def tiles(c, dil):
        nb, l, two, hh, hd = c.shape
        assert l == A_WINDOW * dil
        per_tile = SUBLANES // (two * hh)
        view = c.reshape(nb, l // (dil * per_tile), dil * SUBLANES, hd)
        return view, pl.BlockSpec((bs, view.shape[1], SUBLANES, hd), lambda i: (i, 0, 0, 0))

    ca, ca_spec = tiles(cache_a, 1)
    cb = [tiles(c, dil) for c, (_, dil) in zip((cache_b1, cache_b2, cache_b3), B_GROUPS)]

    heads_a = lambda h0: zs3[:, h0:h0 + A_KV_HEADS]
    qa = zs3[:, HEAD_QA:HEAD_QA + A_Q_HEADS].reshape(n, A_KV_HEADS, A_GQA, HEAD_DIM).transpose(0, 2, 1, 3)
    qa8 = _tile_rows(qa, A_KV_HEADS)
    kva8 = jnp.concatenate([heads_a(HEAD_KA), heads_a(HEAD_VA)] * (SUBLANES // (2 * A_KV_HEADS)), axis=1)
    qb = zs3[:, HEAD_QB:HEAD_QB + B_HEADS].reshape(n, N_B_GROUPS, B_HEADS_PER_GROUP, HEAD_DIM)
    qb8 = _tile_rows(qb, B_HEADS_PER_GROUP)
    kvb8 = zs3[:, HEAD_KVB:].reshape(n, N_B_GROUPS, 2 * B_HEADS_PER_GROUP, HEAD_DIM)

    per_a = SUBLANES // (2 * A_KV_HEADS)
    dist_a = (A_WINDOW - jnp.arange(A_WINDOW, dtype=F32)).reshape(A_WINDOW // per_a, per_a)
    slopes_ag = f32(slopes_a).reshape(A_KV_HEADS, A_GQA).T
    bias_a = jnp.stack([_key_row_bias(dist_a, slopes_ag[g], A_KV_HEADS) for g in range(A_GQA)])
    new_a = _key_row_bias(jnp.zeros((1, per_a), F32), slopes_ag[0], A_KV_HEADS)[0]
    sink8 = _tile_rows(f32(sink).reshape(A_KV_HEADS, A_GQA).T[:, :, None], A_KV_HEADS)
    key_rows = new_a[None, :, :1] == 0.0
    sink8 = jnp.broadcast_to(jnp.where(key_rows, sink8, -MASKED), (A_GQA, SUBLANES, HEAD_DIM))
    slopes_bg = f32(slopes_b).reshape(N_B_GROUPS, B_HEADS_PER_GROUP)
    dist_b = lambda dil: ((A_WINDOW - jnp.arange(A_WINDOW, dtype=F32)) * dil)[:, None]
    bias_b = jnp.stack([_key_row_bias(dist_b(dil), slopes_bg[g], B_HEADS_PER_GROUP)
                        for g, (_, dil) in enumerate(B_GROUPS)])
    new_b = _key_row_bias(jnp.zeros((1, 1), F32), slopes_bg[0], B_HEADS_PER_GROUP)[0]

    full = lambda a: pl.BlockSpec(a.shape, lambda i: (0,) * a.ndim)
    per_seq = lambda a: pl.BlockSpec((bs,) + a.shape[1:], lambda i: (i,) + (0,) * (a.ndim - 1))
    oa8, ob8 = pl.pallas_call(
        _sample_attn_kernel,
        grid=(n // bs,),
        in_specs=[per_seq(qa8), per_seq(kva8), per_seq(qb8), per_seq(kvb8),
                  ca_spec, cb[0][1], cb[1][1], cb[2][1],
                  full(bias_a), full(new_a), full(sink8), full(bias_b), full(new_b)],
        out_specs=[pl.BlockSpec((bs, A_GQA, SUBLANES, HEAD_DIM), lambda i: (i, 0, 0, 0)),
                   pl.BlockSpec((bs, SUBLANES, HEAD_DIM), lambda i: (i, 0, 0))],
        out_shape=[jax.ShapeDtypeStruct((n, A_GQA, SUBLANES, HEAD_DIM), F32),
                   jax.ShapeDtypeStruct((n, SUBLANES, HEAD_DIM), F32)],
        compiler_params=_params(1),
        name="sample_attn",
    )(qa8, kva8, qb8, kvb8, ca, cb[0][0], cb[1][0], cb[2][0], bias_a, new_a, sink8, bias_b, new_b)
    oa = oa8[:, :, A_KV_HEADS:2 * A_KV_HEADS].transpose(0, 2, 1, 3).reshape(n, A_Q_HEADS * HEAD_DIM)
    ob = ob8[:, B_HEADS_PER_GROUP:].reshape(n, B_HEADS_PER_GROUP * HEAD_DIM)
    return oa, ob


class _ShiftRing:
    def __init__(self, c, nw, o, buf, sem_body, sem_tail, sem_out):
        self.c, self.nw, self.o, self.buf = c, nw, o, buf
        self.sem_body, self.sem_tail, self.sem_out = sem_body, sem_tail, sem_out
        self.n_slots, self.nseq, self.rows = buf.shape[:3]
        n, self.l = c.shape[0], c.shape[1]
        assert n % self.nseq == 0 and self.l % self.rows == 0
        self.parts = self.l // self.rows
        self.n_chunks = (n // self.nseq) * self.parts

    def _where(self, k):
        return k % self.n_slots, pl.ds((k // self.parts) * self.nseq, self.nseq), (k % self.parts) * self.rows

    def _body(self, k):
        slot, seqs, r0 = self._where(k)
        return pltpu.make_async_copy(self.c.at[seqs, pl.ds(r0 + 1, self.rows - 1)],
                                     self.buf.at[slot, :, pl.ds(0, self.rows - 1)], self.sem_body.at[slot])

    def _tail_old(self, k):
        slot, seqs, r0 = self._where(k)
        return pltpu.make_async_copy(self.c.at[seqs, pl.ds(jnp.minimum(r0 + self.rows, self.l - 1), 1)],
                                     self.buf.at[slot, :, pl.ds(self.rows - 1, 1)], self.sem_tail.at[slot])

    def _tail_new(self, k):
        slot, seqs, _ = self._where(k)
        return pltpu.make_async_copy(self.nw.at[seqs], self.buf.at[slot, :, self.rows - 1], self.sem_tail.at[slot])

    def write(self, k):
        slot, seqs, r0 = self._where(k)
        return pltpu.make_async_copy(self.buf.at[slot], self.o.at[seqs, pl.ds(r0, self.rows)], self.sem_out.at[slot])

    def _read(self, k, action):
        action(self._body(k))
        if self.parts == 1:
            action(self._tail_new(k))
        else:
            is_last = (k % self.parts) == self.parts - 1
            pl.when(is_last)(lambda: action(self._tail_new(k)))
            pl.when(jnp.logical_not(is_last))(lambda: action(self._tail_old(k)))

    def start_read(self, k):
        self._read(k, lambda cp: cp.start())

    def wait_read(self, k):
        self._read(k, lambda cp: cp.wait())

    def step(self, k):
        n = self.n_chunks
        written = k + 1 - self.n_slots
        pl.when(k == 0)(lambda: self.start_read(0))
        pl.when((written >= 0) & (written < n))(lambda: self.write(jnp.clip(written, 0, n - 1)).wait())
        pl.when(k + 1 < n)(lambda: self.start_read(jnp.minimum(k + 1, n - 1)))

        @pl.when(k < n)
        def _():
            self.wait_read(jnp.minimum(k, n - 1))
            self.write(jnp.minimum(k, n - 1)).start()

    def drain(self, done_steps):
        for k in range(max(done_steps - self.n_slots + 1, 0), self.n_chunks):
            self.write(k).wait()


class _ShiftRider:
    def __init__(self, shift, n_steps):
        self.in_specs, self.out_specs, self.out_shape, self.scratch = [], [], [], []
        if shift is None:
            return
        c, _ = shift
        nseq = next(k for k in range(1, c.shape[0] + 1)
                    if c.shape[0] % k == 0 and c.shape[0] // k + RIDER_SLOTS - 1 <= n_steps)
        any_spec = pl.BlockSpec(memory_space=pl.ANY)
        dma_sems = pltpu.SemaphoreType.DMA((RIDER_SLOTS,))
        self.in_specs = [any_spec, any_spec]
        self.out_specs = [any_spec]
        self.out_shape = [jax.ShapeDtypeStruct(c.shape, c.dtype)]
        self.scratch = [pltpu.VMEM((RIDER_SLOTS, nseq) + tuple(c.shape[1:]), c.dtype), dma_sems, dma_sems, dma_sems]


RIDER_SLOTS = 2
_shift_rider = _ShiftRider


def _ride_shift(rest, n_outs, step):
    if len(rest) == n_outs:
        return rest
    c_ref, nw_ref = rest[:2]
    outs, oc_ref = rest[2:2 + n_outs], rest[2 + n_outs]
    _ShiftRing(c_ref, nw_ref, oc_ref, *rest[3 + n_outs:]).step(step)
    return outs


def _block_kernel(x_ref, oa_ref, ob_ref, ga_ref, gb_ref, wa_ref, wb_ref, wo_ref, n2_ref, *rest):
    h_ref, hn_ref = _ride_shift(rest, 2, pl.program_id(0))
    ya = jnp.dot(oa_ref[...], wa_ref[...], preferred_element_type=F32)
    yb = jnp.dot(ob_ref[...], wb_ref[...], preferred_element_type=F32)
    mix = (ga_ref[...].astype(F32) * ya + gb_ref[...].astype(F32) * yb).astype(BF16)
    h = x_ref[...] + jnp.dot(mix, wo_ref[...], preferred_element_type=F32)
    h_ref[...] = h
    hn_ref[...] = (_rms(h) * n2_ref[...]).astype(BF16)


def _block(x2d, oa, ob, zg, wa, wb, wo, n2, tm, shift=None):
    t, d = x2d.shape
    row = lambda w: pl.BlockSpec((tm, w), lambda i: (i, 0))
    full = lambda a: pl.BlockSpec(a.shape, lambda i: (0, 0))
    rider = _shift_rider(shift, t // tm)
    return pl.pallas_call(
        _block_kernel,
        grid=(t // tm,),
        in_specs=[row(d), row(oa.shape[1]), row(ob.shape[1]),
                  pl.BlockSpec((tm, d), lambda i: (i, 0)), pl.BlockSpec((tm, d), lambda i: (i, 1)),
                  full(wa), full(wb), full(wo), full(n2)] + rider.in_specs,
        out_specs=[row(d), row(d)] + rider.out_specs,
        out_shape=[jax.ShapeDtypeStruct((t, d), F32), jax.ShapeDtypeStruct((t, d), BF16)] + rider.out_shape,
        scratch_shapes=rider.scratch,
        compiler_params=_params(1),
        name="block",
    )(x2d, oa, ob, zg, zg, wa, wb, wo, n2, *(shift or ()))


RANK_NONE = float(PEER_NKEYS)
CAND_ROWS = tuple(PEER_TOPK // (a + 1) for a in range(PEER_TOPK))
CAND_PAD = tuple(-(-r // 8) * 8 for r in CAND_ROWS)


def _extract(work, n_take, break_ties):
    rows = lax.broadcasted_iota(jnp.int32, work.shape, 0).astype(F32)
    rank = jnp.full(work.shape, RANK_NONE, F32)
    vals = []
    for kk in range(n_take):
        mx = jnp.max(work, axis=0, keepdims=True)
        sel = work == mx
        if break_ties:
            first = jnp.min(jnp.where(sel, rows, float(work.shape[0])), axis=0, keepdims=True)
            sel = rows == first
        rank = jnp.where(sel, float(kk), rank)
        work = jnp.where(sel, -jnp.inf, work)
        vals.append(mx)
    return rank, jnp.concatenate(vals, axis=0)


def _extract_top(work, n_take):
    rank, vals = _extract(work, n_take, break_ties=False)
    ranked = jnp.sum(jnp.where(rank < RANK_NONE, 1.0, 0.0), axis=0, keepdims=True)
    return lax.cond(jnp.max(ranked) > n_take,
                    lambda: _extract(work, n_take, break_ties=True), lambda: (rank, vals))


ROUTE_SHIFT_SLOTS = 2
ROUTE_SHIFT_PER_STEP = 2


def _route_kernel(*refs, n_steps, with_shift):
    hn_ref, wq_ref, sk_ref = refs[:3]
    if with_shift:
        c_ref, nw_ref, rank1_ref, p1_ref, m_ref, p0_ref, oc_ref, s_scr, buf, sem_body, sem_tail, sem_out = refs[3:]
        ring = _ShiftRing(c_ref, nw_ref, oc_ref, buf, sem_body, sem_tail, sem_out)
        assert ring.n_chunks == n_steps * ROUTE_SHIFT_PER_STEP and PEER_HEADS % ROUTE_SHIFT_PER_STEP == 0
        first_chunk = pl.program_id(0) * ROUTE_SHIFT_PER_STEP
    else:
        rank1_ref, p1_ref, m_ref, p0_ref, s_scr = refs[3:]
        ring = None
    q = jnp.dot(hn_ref[...], wq_ref[...], preferred_element_type=F32).astype(BF16)
    s_scr[...] = lax.dot_general(sk_ref[...], q, NT_DIMS, preferred_element_type=F32)

    def head(hh, carry):
        if ring is not None:
            every = PEER_HEADS // ROUTE_SHIFT_PER_STEP
            pl.when(lax.rem(hh, every) == 0)(lambda: ring.step(first_chunk + hh // every))
        base = pl.multiple_of(hh * 2 * PEER_NKEYS, 2 * PEER_NKEYS)
        s0 = s_scr[pl.ds(base, PEER_NKEYS), :]
        s1 = s_scr[pl.ds(base + PEER_NKEYS, PEER_NKEYS), :]
        rank0, vals0 = _extract_top(s0, PEER_TOPK)
        rank1, vals1 = _extract_top(s1, PEER_TOPK)
        cand = []
        for a in range(PEER_TOPK):
            blk = vals0[a:a + 1] + vals1[:CAND_PAD[a]]
            rr = lax.broadcasted_iota(jnp.int32, blk.shape, 0)
            cand.append(jnp.where(rr < CAND_ROWS[a], blk, -jnp.inf))
        taken, best = _extract_top(jnp.concatenate(cand, axis=0), PEER_TOPK)
        zsum = jnp.sum(jnp.exp(best - best[0:1]), axis=0, keepdims=True)
        m = jnp.zeros_like(s0)
        off = 0
        for a in range(PEER_TOPK):
            cnt = jnp.sum(jnp.where(taken[off:off + CAND_PAD[a]] < RANK_NONE, 1.0, 0.0), axis=0, keepdims=True)
            m = jnp.where(rank0 == float(a), cnt, m)
            off += CAND_PAD[a]
        rank1_ref[hh] = rank1
        p1_ref[hh] = jnp.exp(s1 - vals1[0:1])
        m_ref[hh] = m
        p0_ref[hh] = jnp.exp(s0 - vals0[0:1]) / zsum
        return carry

    lax.fori_loop(0, PEER_HEADS, head, 0)
    if ring is not None:
        pl.when(pl.program_id(0) == n_steps - 1)(lambda: ring.drain(ring.n_chunks))


def _route(hn, wq, sk, tr, shift=None):
    t, d = hn.shape
    n_steps = t // tr
    table = jax.ShapeDtypeStruct((PEER_HEADS, PEER_NKEYS, t), F32)
    tspec = pl.BlockSpec((PEER_HEADS, PEER_NKEYS, tr), lambda i: (0, 0, i))
    full = lambda a: pl.BlockSpec(a.shape, lambda i: (0, 0))
    in_specs = [pl.BlockSpec((tr, d), lambda i: (i, 0)), full(wq), full(sk)]
    out_specs, out_shape = [tspec] * 4, [table] * 4
    scratch = [pltpu.VMEM((PEER_HEADS * 2 * PEER_NKEYS, tr), F32)]
    args = [hn, wq, sk]
    if shift is not None:
        c, nw = shift
        nseq = c.shape[0] // (n_steps * ROUTE_SHIFT_PER_STEP)
        assert nseq * n_steps * ROUTE_SHIFT_PER_STEP == c.shape[0]
        any_spec = pl.BlockSpec(memory_space=pl.ANY)
        in_specs += [any_spec, any_spec]
        out_specs = out_specs + [any_spec]
        out_shape = out_shape + [jax.ShapeDtypeStruct(c.shape, c.dtype)]
        dma_sems = pltpu.SemaphoreType.DMA((ROUTE_SHIFT_SLOTS,))
        scratch += [pltpu.VMEM((ROUTE_SHIFT_SLOTS, nseq) + tuple(c.shape[1:]), c.dtype), dma_sems, dma_sems, dma_sems]
        args += [c, nw]
    out = pl.pallas_call(
        functools.partial(_route_kernel, n_steps=n_steps, with_shift=shift is not None),
        grid=(n_steps,),
        in_specs=in_specs,
        out_specs=out_specs,
        out_shape=out_shape,
        scratch_shapes=scratch,
        compiler_params=_params(1),
        name="route",
    )(*args)
    return (out[:4], out[4]) if shift is not None else out


PEER_TE = 512
PEER_KEYS_PER_BLOCK = PEER_TE // PEER_NKEYS
PEER_TABLE_KEYS = 8
PEER_BLOCKS_PER_TABLE = PEER_TABLE_KEYS // PEER_KEYS_PER_BLOCK
assert PEER_BLOCKS_PER_TABLE == 2
SQRT_HALF = math.sqrt(0.5)


PEER_GATE_ROWS = 32


def _peer_gate_pieces(act_ref, w_ref, rank1_ref, p1_ref, m_ref, p0_ref, key_off):
    tm = act_ref.shape[1]

    def piece(ls, r0):
        gates = [jnp.zeros((PEER_GATE_ROWS, LANES), F32) for _ in range(PEER_KEYS_PER_BLOCK)]
        for hh in range(PEER_HEADS):
            r1 = rank1_ref[hh, r0:r0 + PEER_GATE_ROWS, ls]
            p1 = p1_ref[hh, r0:r0 + PEER_GATE_ROWS, ls]
            for j in range(PEER_KEYS_PER_BLOCK):
                kr = slice(key_off + j, key_off + j + 1)
                gates[j] = gates[j] + jnp.where(r1 < m_ref[hh, kr, ls], p1, 0.0) * p0_ref[hh, kr, ls]
        for j in range(PEER_KEYS_PER_BLOCK):
            rs = slice(j * PEER_NKEYS + r0, j * PEER_NKEYS + r0 + PEER_GATE_ROWS)
            a = act_ref[rs, ls]
            w_ref[rs, ls] = (gates[j] * (0.5 * a * (1.0 + lax.erf(a * SQRT_HALF)))).astype(BF16)

    return [functools.partial(piece, slice(c * LANES, (c + 1) * LANES), r0)
            for c in range(tm // LANES) for r0 in range(0, PEER_NKEYS, PEER_GATE_ROWS)]


PEER_DOT1_ROWS = 256
PEER_DOT2_ROWS = 512


def _interleave(*stages):
    tagged = [((i + 0.5) / len(st), si, piece) for si, st in enumerate(stages) for i, piece in enumerate(st)]
    return [piece for _, _, piece in sorted(tagged, key=lambda x: x[:2])]


def _peer_kernel(*refs, n_blocks, n_steps, with_shift):
    h_hbm, hn_ref, u_ref, vt_ref, rank1_ref, p1_ref, m_ref, p0_ref = refs[:8]
    g = pl.program_id(0)
    if with_shift:
        (c_ref, nw_ref, o_hbm, oc_ref, act0, act1, w0, w1, acc, obuf, sem_h, sem_o,
         buf, sem_body, sem_tail, sem_out) = refs[8:]
        ring = _ShiftRing(c_ref, nw_ref, oc_ref, buf, sem_body, sem_tail, sem_out)
        assert ring.n_chunks + ring.n_slots - 1 <= n_steps, "not enough grid steps to finish the shift"
        ring.step(g)
    else:
        o_hbm, act0, act1, w0, w1, acc, obuf, sem_h, sem_o = refs[8:]
    tm = obuf.shape[0]

    @pl.when(g == 0)
    def _():
        for ref in (act0, act1, w0, w1, acc):
            ref[...] = jnp.zeros(ref.shape, ref.dtype)

    blk3 = jnp.clip(g - 2, 0, n_steps - 3)
    tile = blk3 // n_blocks
    n_tiles = (n_steps - 2) // n_blocks
    first = (g >= 2) & (lax.rem(blk3, n_blocks) == 0)
    last = (g >= 2) & (lax.rem(blk3, n_blocks) == n_blocks - 1)
    tile_rows = lambda i: pl.ds(pl.multiple_of(i * tm, tm), tm)
    read_h = lambda i: pltpu.make_async_copy(h_hbm.at[tile_rows(i)], obuf, sem_h)
    write_o = lambda i: pltpu.make_async_copy(obuf, o_hbm.at[tile_rows(i)], sem_o)

    @pl.when(first)
    def _():
        pl.when(tile > 0)(lambda: write_o(jnp.maximum(tile - 1, 0)).wait())
        read_h(tile).start()

    def body(act_new, act_cur, w_new, w_old, key_off):
        def dot1(rows):
            act_new[rows, :] = lax.dot_general(u_ref[rows, :], hn_ref[...], NT_DIMS, preferred_element_type=F32)

        def dot2(rows):
            acc[rows, :] += jnp.dot(vt_ref[rows, :], w_old[...], preferred_element_type=F32)

        chunks = lambda n, size: [slice(r, r + size) for r in range(0, n, size)]
        for piece in _interleave(
                [functools.partial(dot1, rows) for rows in chunks(u_ref.shape[0], PEER_DOT1_ROWS)],
                _peer_gate_pieces(act_cur, w_new, rank1_ref, p1_ref, m_ref, p0_ref, key_off),
                [functools.partial(dot2, rows) for rows in chunks(vt_ref.shape[0], PEER_DOT2_ROWS)]):
            piece()

        @pl.when(last)
        def _():
            read_h(tile).wait()
            for c in range(tm // LANES):
                rows = slice(c * LANES, (c + 1) * LANES)
                obuf[rows, :] += acc[:, rows].T
            acc[...] = jnp.zeros(acc.shape, acc.dtype)
            write_o(tile).start()
            pl.when(tile == n_tiles - 1)(lambda: write_o(tile).wait())

    even = lax.rem(g, 2) == 0
    pl.when(even)(lambda: body(act0, act1, w1, w0, PEER_KEYS_PER_BLOCK))
    pl.when(jnp.logical_not(even))(lambda: body(act1, act0, w0, w1, 0))


PEER_SHIFT_SLOTS = 2


def _peer(h, hn, u, vt, tables, tm, shift=None):
    t, d = h.shape
    nb = u.shape[0] // PEER_TE
    assert nb % PEER_BLOCKS_PER_TABLE == 0
    n = (t // tm) * nb
    n_steps = n + 2
    blk1 = lambda g: jnp.minimum(g, n - 1)
    blk2 = lambda g: jnp.clip(g - 1, 0, n - 1)
    blk3 = lambda g: jnp.clip(g - 2, 0, n - 1)
    once = pl.Buffered(1)
    any_spec = pl.BlockSpec(memory_space=pl.ANY)
    tspec = pl.BlockSpec((PEER_HEADS, PEER_NKEYS, tm), lambda g: (0, 0, blk2(g) // nb), pipeline_mode=once)
    kspec = pl.BlockSpec((PEER_HEADS, PEER_TABLE_KEYS, tm),
                         lambda g: (0, (blk2(g) % nb) // PEER_BLOCKS_PER_TABLE, blk2(g) // nb))
    in_specs = [any_spec,
                pl.BlockSpec((tm, d), lambda g: (blk1(g) // nb, 0), pipeline_mode=once),
                pl.BlockSpec((PEER_TE, d), lambda g: (blk1(g) % nb, 0)),
                pl.BlockSpec((None, d, PEER_TE), lambda g: (blk3(g) % nb, 0, 0)), tspec, tspec, kspec, kspec]
    out_specs = [any_spec]
    out_shape = [jax.ShapeDtypeStruct((t, d), F32)]
    scratch = [pltpu.VMEM((PEER_TE, tm), F32), pltpu.VMEM((PEER_TE, tm), F32),
               pltpu.VMEM((PEER_TE, tm), BF16), pltpu.VMEM((PEER_TE, tm), BF16), pltpu.VMEM((d, tm), F32),
               pltpu.VMEM((tm, d), F32), pltpu.SemaphoreType.DMA(()), pltpu.SemaphoreType.DMA(())]
    args = [h, hn, u, vt, *tables]
    if shift is not None:
        c, nw = shift
        parts = (n_steps - PEER_SHIFT_SLOTS + 1) // c.shape[0]
        assert parts >= 1, "more sequences than grid steps"
        while c.shape[1] % parts:
            parts -= 1
        in_specs += [any_spec, any_spec]
        out_specs.append(any_spec)
        out_shape.append(jax.ShapeDtypeStruct(c.shape, c.dtype))
        dma_sems = pltpu.SemaphoreType.DMA((PEER_SHIFT_SLOTS,))
        scratch += [pltpu.VMEM((PEER_SHIFT_SLOTS, 1, c.shape[1] // parts) + tuple(c.shape[2:]), c.dtype),
                    dma_sems, dma_sems, dma_sems]
        args += [c, nw]
    out = pl.pallas_call(
        functools.partial(_peer_kernel, n_blocks=nb, n_steps=n_steps, with_shift=shift is not None),
        grid=(n_steps,),
        in_specs=in_specs,
        out_specs=out_specs,
        out_shape=out_shape,
        scratch_shapes=scratch,
        compiler_params=_params(1),
        name="peer",
    )(*args)
    return out if shift is not None else out[0]


def _alibi_slopes():
    return 2.0 ** (-8.0 * jnp.arange(1, N_ALIBI_HEADS + 1, dtype=F32) / N_ALIBI_HEADS)


def _subkey_matrix(subkeys):
    two, nk, dh = subkeys.shape
    eye = jnp.eye(PEER_HEADS * two, dtype=subkeys.dtype).reshape(PEER_HEADS, two, PEER_HEADS, two)
    sk = jnp.einsum("hcgb,cnd->hcngbd", eye, subkeys)
    return sk.reshape(PEER_HEADS * two * nk, PEER_HEADS * two * dh)


def _tail(x2d, oa, ob, zg, w, tm_block, tr, tm_peer, shifts=(None, None, None)):
    h, hn, *s_block = _block(x2d, oa, ob, zg, w["wa"], w["wb"], w["wo"], w["n2"], tm_block, shifts[0])
    routed = _route(hn, w["wq"], w["sk"], tr, shifts[1])
    tables, s_route = routed if shifts[1] is not None else (routed, None)
    y = _peer(h, hn, w["u"], w["vt"], tables, tm_peer, shifts[2])
    y, s_peer = y if shifts[2] is not None else (y, None)
    return y, (s_block[0] if s_block else None, s_route, s_peer)


def kernel(x_prompt, x_sample, cache_a_kv, cache_b1_kv, cache_b2_kv, cache_b3_kv, norm1_w, w_in, q_norm_a,
           k_norm_a, sink_a, q_norm_b, k_norm_b, w_branch_a, w_branch_b, w_out, norm2_w, peer_wq,
           peer_subkeys, peer_u, peer_v):
    assert norm1_w.shape[0] == 1, "single layer"
    b, s, d = x_prompt.shape
    n_dec = x_sample.shape[0]
    assert x_sample.shape[1] == 1

    slopes = _alibi_slopes()
    slopes_a, slopes_b = slopes[:A_Q_HEADS], slopes[A_Q_HEADS:]
    sink = sink_a[0].astype(F32)
    ones = jnp.ones((HEAD_DIM,), F32)
    col_w = jnp.concatenate(
        [jnp.tile(q_norm_a[0], A_Q_HEADS), jnp.tile(k_norm_a[0], A_KV_HEADS), jnp.tile(ones, A_KV_HEADS),
         jnp.tile(q_norm_b[0], B_HEADS)]
        + [jnp.tile(k_norm_b[0], B_HEADS_PER_GROUP), jnp.tile(ones, B_HEADS_PER_GROUP)] * N_B_GROUPS
    ).astype(F32)
    normed = lambda flag, nh: jnp.full((nh * HEAD_DIM,), flag, F32)
    col_flag = jnp.concatenate(
        [normed(1.0, A_Q_HEADS + A_KV_HEADS), normed(0.0, A_KV_HEADS), normed(1.0, B_HEADS)]
        + [normed(1.0, B_HEADS_PER_GROUP), normed(0.0, B_HEADS_PER_GROUP)] * N_B_GROUPS)
    col_w = jnp.stack([col_w, col_flag])
    gate_w = jnp.zeros((2, GATE_W), F32)
    n1 = norm1_w[0].astype(F32)[None]
    w_cols = lambda h0, nh: w_in[0, :, h0 * HEAD_DIM:(h0 + nh) * HEAD_DIM]
    ref_kb, ref_vb = HEAD_QB + B_HEADS, HEAD_QB + 2 * B_HEADS
    w_qkv = jnp.concatenate(
        [w_cols(0, ref_kb)] + [w_cols(base + g * B_HEADS_PER_GROUP, B_HEADS_PER_GROUP)
                               for g in range(N_B_GROUPS) for base in (ref_kb, ref_vb)], axis=1).astype(BF16)
    w_gate = w_in[0, :, QKV_W:].astype(BF16)
    w = dict(wa=w_branch_a[0].astype(BF16), wb=w_branch_b[0].astype(BF16), wo=w_out[0].astype(BF16),
             n2=norm2_w[0].astype(F32)[None], wq=peer_wq[0].astype(BF16),
             sk=_subkey_matrix(peer_subkeys[0]).astype(BF16),
             u=peer_u[0].astype(BF16),
             vt=peer_v[0].astype(BF16).reshape(PEER_EXPERTS // PEER_TE, PEER_TE, d).transpose(0, 2, 1))

    xs = x_sample.reshape(n_dec, d)
    zs = _proj(xs, n1, w_qkv, col_w, False, F32, n_dec, "proj_qkv_s")
    zgs = _proj(xs, n1, w_gate, gate_w, True, BF16, n_dec, "proj_gate_s")
    zs3 = zs.reshape(n_dec, N_QKV_HEADS, HEAD_DIM)
    caches = (cache_a_kv[0], cache_b1_kv[0], cache_b2_kv[0], cache_b3_kv[0])
    kv_heads = [(HEAD_KA, A_KV_HEADS)] + [(_head_kb(g), B_HEADS_PER_GROUP) for g in range(N_B_GROUPS)]
    news = [zs3[:, h0:h0 + 2 * nh].reshape(n_dec, 2, nh, HEAD_DIM) for h0, nh in kv_heads]

    xp = x_prompt.reshape(b * s, d)
    zq = _proj(xp, n1, w_qkv, col_w, False, F32, 1024, "proj_qkv")
    zg = _proj(xp, n1, w_gate, gate_w, True, BF16, 1024, "proj_gate")
    zq3 = zq.reshape(b, s, QKV_W)
    oa, new_a = _attn_a(zq3, slopes_a, sink, (caches[0], news[0]))
    oa = oa.reshape(b * s, A_Q_HEADS * HEAD_DIM)
    ob, new_b1 = _attn_b(zq3, slopes_b, (caches[1], news[1]))
    ob = ob.reshape(b * s, B_HEADS_PER_GROUP * HEAD_DIM)
    y_prompt, new_b = _tail(xp, oa, ob, zg, w, 256, 256, 1024,
                            shifts=(None, (caches[2], news[2]), (caches[3], news[3])))
    y_prompt = y_prompt.reshape(b, s, d)

    def window(h0, nh, length):
        rows = zq3[:, s - length:, h0 * HEAD_DIM:(h0 + 2 * nh) * HEAD_DIM]
        return rows.reshape(1, b, length, 2, nh, HEAD_DIM)

    kv_prompt = [window(h0, nh, min(win, s))
                 for (h0, nh), win in zip(kv_heads, (A_WINDOW,) + tuple(w_ for w_, _ in B_GROUPS))]

    oa_s, ob_s = _sample_attn(zs3, *caches, slopes_a, sink, slopes_b)
    y_sample, _ = _tail(xs, oa_s.astype(BF16), ob_s.astype(BF16), zgs, w, n_dec, n_dec, n_dec)
    y_sample = y_sample.reshape(n_dec, 1, d)

    kv_sample = [o[None] for o in (new_a, new_b1, *new_b[1:])]

    return (y_prompt, y_sample, *kv_prompt, *kv_sample)
```
